```python
import jax, jax.numpy as jnp
from jax import lax
import numpy as np

D_MODEL = 1024
BATCH = 2
SEQ = 8192
DEPTH = 4

M_HEADS = 4
M_HEAD_DIM = 128
M_WIDTH = M_HEADS * M_HEAD_DIM
M_CONV = 5
M_CHUNK = 64
B_HEADS = 8
B_HEAD_DIM = 64
B_WIDTH = B_HEADS * B_HEAD_DIM
DILATED_PAIRS = ((128, 1), (512, 4), (2048, 16))
AB_IN = 2 * M_WIDTH + M_WIDTH + M_WIDTH + 4 * M_HEADS + 3 * B_WIDTH
AB_MIX = M_WIDTH + B_WIDTH
C_HEADS = 16
C_KV_HEADS = 4
C_HEAD_DIM = 64
C_RADIUS = 128
C_IN = (C_HEADS + 2 * C_KV_HEADS) * C_HEAD_DIM
C_MIX = C_HEADS * C_HEAD_DIM
D_FF = 2816
N_EVEN = (DEPTH + 1) // 2
N_ODD = DEPTH // 2
EPS = 1e-6

kernel_name = "hybrid_mlstm_dilated_swa_macaron_encoder"


def rmsnorm(x, g):
    xf = x.astype(jnp.float32)
    y = xf * lax.rsqrt(jnp.mean(xf * xf, axis=-1, keepdims=True) + EPS) * g.astype(jnp.float32)
    return y.astype(x.dtype)


def swiglu(h, w1, w3, w2):
    return (jax.nn.silu(h @ w1) * (h @ w3)) @ w2


def alibi_slopes(n):
    return jnp.exp2(-8.0 * jnp.arange(1, n + 1, dtype=jnp.float32) / n)


def banded_attention(q, k, v, radius, dist_unit, slopes, sink):
    N, L, K, G, E = q.shape
    Q = radius
    n = -(-L // Q)
    Lp = n * Q
    qb = jnp.pad(q, ((0, 0), (0, Lp - L), (0, 0), (0, 0), (0, 0))).reshape(N, n, Q, K, G, E)
    qb = jnp.moveaxis(qb, 1, 0)
    kp = jnp.pad(k, ((0, 0), (Q, Lp - L + Q), (0, 0), (0, 0)))
    vp = jnp.pad(v, ((0, 0), (Q, Lp - L + Q), (0, 0), (0, 0)))
    rel = jnp.arange(3 * Q)[None, :] - Q - jnp.arange(Q)[:, None]
    bias = -slopes[:, :, None, None] * (dist_unit * jnp.abs(rel)).astype(jnp.float32)
    in_window = jnp.abs(rel) <= radius
    scale = E ** -0.5

    def block(args):
        i, qi = args
        ki = lax.dynamic_slice_in_dim(kp, i * Q, 3 * Q, axis=1).astype(jnp.float32)
        vi = lax.dynamic_slice_in_dim(vp, i * Q, 3 * Q, axis=1).astype(jnp.float32)
        s = jnp.einsum('zqhge,zshe->zhgqs', qi.astype(jnp.float32), ki) * scale + bias
        kpos = i * Q - Q + jnp.arange(3 * Q)
        valid = in_window & ((kpos >= 0) & (kpos < L))[None, :]
        s = jnp.where(valid, s, -jnp.inf)
        m = s.max(-1)
        if sink is not None:
            m = jnp.maximum(m, sink[None, :, :, None])
        p = jnp.exp(s - m[..., None])
        den = p.sum(-1)
        if sink is not None:
            den = den + jnp.exp(sink[None, :, :, None] - m)
        o = jnp.einsum('zhgqs,zshe->zqhge', p, vi) / jnp.moveaxis(den, 3, 1)[..., None]
        lse = jnp.moveaxis(m + jnp.log(den), 3, 1)
        return o, lse

    o, lse = lax.map(block, (jnp.arange(n), qb))
    o = jnp.moveaxis(o, 0, 1).reshape(N, Lp, K, G, E)[:, :L]
    lse = jnp.moveaxis(lse, 0, 1).reshape(N, Lp, K, G)[:, :L]
    return o, lse


def to_residue(t, d):
    Bn, S = t.shape[:2]
    t = jnp.moveaxis(t.reshape((Bn, S // d, d) + t.shape[2:]), 2, 1)
    return t.reshape((Bn * d, S // d) + t.shape[3:])


def from_residue(t, Bn, d):
    L = t.shape[1]
    t = jnp.moveaxis(t.reshape((Bn, d, L) + t.shape[2:]), 1, 2)
    return t.reshape((Bn, L * d) + t.shape[3:])


def mlstm_chunkwise(q, k, v, i_pre, logf):
    Bn, H, S, E = q.shape
    L = M_CHUNK
    NC = S // L
    q, k, v = (t.reshape(Bn, H, NC, L, E) for t in (q, k, v))
    i_pre = i_pre.reshape(Bn, H, NC, L)
    b = jnp.cumsum(logf.reshape(Bn, H, NC, L), axis=-1)
    b_last = b[..., -1]
    a = b_last[..., None] - b + i_pre
    m_loc = a.max(-1)
    wa = jnp.exp(a - m_loc[..., None])
    C_chunk = jnp.einsum('bhcs,bhcsd,bhcse->bhcde', wa, v, k)
    n_chunk = jnp.einsum('bhcs,bhcse->bhce', wa, k)

    def step(carry, inp):
        C, n, m = carry
        Cc, nc, mc, bc = inp
        m_new = jnp.maximum(bc + m, mc)
        s_old = jnp.exp(bc + m - m_new)
        s_new = jnp.exp(mc - m_new)
        C_new = s_old[..., None, None] * C + s_new[..., None, None] * Cc
        n_new = s_old[..., None] * n + s_new[..., None] * nc
        return (C_new, n_new, m_new), (C, n, m)

    init = (jnp.zeros((Bn, H, E, E), q.dtype), jnp.zeros((Bn, H, E), q.dtype), jnp.zeros((Bn, H), q.dtype))
    cf = lambda t: jnp.moveaxis(t, 2, 0)
    _, (C_prev, n_prev, m_prev) = lax.scan(step, init, (cf(C_chunk), cf(n_chunk), cf(m_loc), cf(b_last)))
    C_prev, n_prev, m_prev = (jnp.moveaxis(t, 0, 2) for t in (C_prev, n_prev, m_prev))

    lower = jnp.tril(jnp.ones((L, L), dtype=bool))
    dmat = jnp.where(lower, b[..., :, None] - b[..., None, :] + i_pre[..., None, :], -jnp.inf)
    m_inter = b + m_prev[..., None]
    m_t = jnp.maximum(m_inter, dmat.max(-1))
    w = jnp.exp(dmat - m_t[..., None]) * jnp.einsum('bhcte,bhcse->bhcts', q, k)
    s_inter = jnp.exp(m_inter - m_t)
    num = jnp.einsum('bhcts,bhcsd->bhctd', w, v) + s_inter[..., None] * jnp.einsum('bhcde,bhcte->bhctd', C_prev, q)
    den = w.sum(-1) + s_inter * jnp.einsum('bhce,bhcte->bhct', n_prev, q)
    h = num / jnp.maximum(jnp.abs(den), jnp.exp(-m_t))[..., None]
    return h.reshape(Bn, H, S, E)


def mixer_ab(h, w_in, conv_w, conv_b, gate_b, hnorm_g, w_out):
    Bn, S, _ = h.shape
    f32 = jnp.float32
    proj = h @ w_in
    cuts = [2 * M_WIDTH, 3 * M_WIDTH, 4 * M_WIDTH, 4 * M_WIDTH + 4 * M_HEADS,
            4 * M_WIDTH + 4 * M_HEADS + B_WIDTH, 4 * M_WIDTH + 4 * M_HEADS + 2 * B_WIDTH]
    qk_m, v_m, o_m, g_m, q_b, k_b, v_b = jnp.split(proj, cuts, axis=-1)

    qk_m = lax.conv_general_dilated(qk_m, conv_w[:, None, :], (1,), 'SAME',
                                    dimension_numbers=('NWC', 'WIO', 'NWC'),
                                    feature_group_count=2 * M_WIDTH)
    qk_m = jax.nn.silu(qk_m + conv_b)
    q_m, k_m = jnp.split(qk_m, 2, axis=-1)
    heads = lambda t: t.reshape(Bn, S, M_HEADS, M_HEAD_DIM).transpose(0, 2, 1, 3).astype(f32)
    q_m, k_m, v_m = heads(q_m), heads(k_m) * (M_HEAD_DIM ** -0.5), heads(v_m)
    g = g_m.astype(f32).reshape(Bn, S, 2, 2, M_HEADS) + gate_b.astype(f32)
    g = g.transpose(2, 3, 0, 4, 1)
    i_pre, logf = g[0], jax.nn.log_sigmoid(g[1])
    fwd = mlstm_chunkwise(q_m, k_m, v_m, i_pre[0], logf[0])
    rev = lambda t: jnp.flip(t, axis=2)
    bwd = rev(mlstm_chunkwise(rev(q_m), rev(k_m), rev(v_m), rev(i_pre[1]), rev(logf[1])))
    hm = fwd + bwd
    hm = (hm * lax.rsqrt(jnp.mean(hm * hm, axis=-1, keepdims=True) + EPS)
          * hnorm_g.astype(f32).reshape(M_HEADS, 1, M_HEAD_DIM))
    hm = hm.transpose(0, 2, 1, 3).reshape(Bn, S, M_WIDTH) * jax.nn.sigmoid(o_m.astype(f32))

    qb = q_b.reshape(Bn, S, B_HEADS, 1, B_HEAD_DIM)
    kb = k_b.reshape(Bn, S, B_HEADS, B_HEAD_DIM)
    vb = v_b.reshape(Bn, S, B_HEADS, B_HEAD_DIM)
    slopes = alibi_slopes(B_HEADS).reshape(B_HEADS, 1)
    outs, lses = [], []
    for window, dil in DILATED_PAIRS:
        o, lse = banded_attention(to_residue(qb, dil), to_residue(kb, dil), to_residue(vb, dil),
                                  window // (2 * dil), dil, slopes, None)
        outs.append(from_residue(o, Bn, dil))
        lses.append(from_residue(lse, Bn, dil))
    wts = jax.nn.softmax(jnp.stack(lses), axis=0)
    ob = jnp.sum(wts[..., None] * jnp.stack(outs), axis=0).reshape(Bn, S, B_WIDTH)

    mixed = jnp.concatenate([hm, ob], axis=-1).astype(h.dtype)
    return mixed @ w_out


def mixer_c(h, w_in, sink, w_out):
    Bn, S, _ = h.shape
    G = C_HEADS // C_KV_HEADS
    proj = h @ w_in
    q, k, v = jnp.split(proj, [C_HEADS * C_HEAD_DIM, (C_HEADS + C_KV_HEADS) * C_HEAD_DIM], axis=-1)
    q = q.reshape(Bn, S, C_KV_HEADS, G, C_HEAD_DIM)
    k = k.reshape(Bn, S, C_KV_HEADS, C_HEAD_DIM)
    v = v.reshape(Bn, S, C_KV_HEADS, C_HEAD_DIM)
    o, _ = banded_attention(q, k, v, C_RADIUS, 1, alibi_slopes(C_HEADS).reshape(C_KV_HEADS, G),
                            sink.astype(jnp.float32).reshape(C_KV_HEADS, G))
    return o.reshape(Bn, S, C_MIX).astype(h.dtype) @ w_out


def setup_inputs(seed: int = 0) -> dict:
    key = jax.random.key(seed)
    ks = jax.random.split(key, 16)
    nrm = jax.random.normal
    D, F = D_MODEL, D_FF
    x = nrm(ks[0], (BATCH, SEQ, D), jnp.float32)
    norm_g = 1.0 + 0.02 * nrm(ks[1], (DEPTH, 3, D), jnp.float32)
    ffn_w1 = nrm(ks[2], (DEPTH, 2, D, F), jnp.float32) * D ** -0.5
    ffn_w3 = nrm(ks[3], (DEPTH, 2, D, F), jnp.float32) * D ** -0.5
    ffn_w2 = nrm(ks[4], (DEPTH, 2, F, D), jnp.float32) * F ** -0.5
    ab_w_in = nrm(ks[5], (N_EVEN, D, AB_IN), jnp.float32) * D ** -0.5
    ab_conv_w = nrm(ks[6], (N_EVEN, M_CONV, 2 * M_WIDTH), jnp.float32) * M_CONV ** -0.5
    ab_conv_b = 0.02 * nrm(ks[7], (N_EVEN, 2 * M_WIDTH), jnp.float32)
    ig_b = 0.1 * nrm(ks[8], (N_EVEN, 1, 2, M_HEADS), jnp.float32)
    fg_b = jnp.linspace(3.0, 6.0, M_HEADS, dtype=jnp.float32) + 0.1 * nrm(ks[9], (N_EVEN, 1, 2, M_HEADS), jnp.float32)
    ab_gate_b = jnp.concatenate([ig_b, fg_b], axis=1)
    ab_hnorm_g = 1.0 + 0.02 * nrm(ks[10], (N_EVEN, M_WIDTH), jnp.float32)
    ab_w_out = nrm(ks[11], (N_EVEN, AB_MIX, D), jnp.float32) * AB_MIX ** -0.5
    c_w_in = nrm(ks[12], (N_ODD, D, C_IN), jnp.float32) * D ** -0.5
    c_sink = 0.5 * nrm(ks[13], (N_ODD, C_HEADS), jnp.float32)
    c_w_out = nrm(ks[14], (N_ODD, C_MIX, D), jnp.float32) * C_MIX ** -0.5
    final_g = 1.0 + 0.02 * nrm(ks[15], (D,), jnp.float32)
    return {"x": x, "norm_g": norm_g, "ffn_w1": ffn_w1, "ffn_w3": ffn_w3, "ffn_w2": ffn_w2,
            "ab_w_in": ab_w_in, "ab_conv_w": ab_conv_w, "ab_conv_b": ab_conv_b, "ab_gate_b": ab_gate_b,
            "ab_hnorm_g": ab_hnorm_g, "ab_w_out": ab_w_out, "c_w_in": c_w_in, "c_sink": c_sink,
            "c_w_out": c_w_out, "final_g": final_g}


def reference(x, norm_g, ffn_w1, ffn_w3, ffn_w2, ab_w_in, ab_conv_w, ab_conv_b, ab_gate_b,
              ab_hnorm_g, ab_w_out, c_w_in, c_sink, c_w_out, final_g):
    for l in range(DEPTH):
        x = x + 0.5 * swiglu(rmsnorm(x, norm_g[l, 0]), ffn_w1[l, 0], ffn_w3[l, 0], ffn_w2[l, 0])
        h = rmsnorm(x, norm_g[l, 1])
        j = l // 2
        if l % 2 == 0:
            x = x + mixer_ab(h, ab_w_in[j], ab_conv_w[j], ab_conv_b[j], ab_gate_b[j], ab_hnorm_g[j], ab_w_out[j])
        else:
            x = x + mixer_c(h, c_w_in[j], c_sink[j], c_w_out[j])
        x = x + 0.5 * swiglu(rmsnorm(x, norm_g[l, 2]), ffn_w1[l, 1], ffn_w3[l, 1], ffn_w2[l, 1])
    return rmsnorm(x, final_g)
```

```python
import functools

import jax
import jax.numpy as jnp
from jax import lax
from jax.experimental import pallas as pl
from jax.experimental.pallas import tpu as pltpu

F32 = jnp.float32
BF16 = jnp.bfloat16

EPS = 1e-6
M_HEADS = 4
M_HEAD_DIM = 128
M_WIDTH = M_HEADS * M_HEAD_DIM
M_CONV = 5
B_HEADS = 8
B_WIDTH = 512
DILATIONS = (1, 4, 16)
B_RADIUS = 64
C_HEADS = 16
C_KV_HEADS = 4
C_GROUP = C_HEADS // C_KV_HEADS
C_RADIUS = 128
HEAD_DIM = 64
LANES = 128
SUBLANES = 8

VMEM_LIMIT_BYTES = 56 * 1024 * 1024
ROW_TILE = 512
ATTN_Q_TILE = 512
ATTN_Q_BLOCK = 128
MLSTM_CHUNK = 128
CONV_ROWS = 512


def _cparams(*sem):
    return pltpu.CompilerParams(dimension_semantics=sem, vmem_limit_bytes=VMEM_LIMIT_BYTES)


def _resident(shape, index_map):
    return pl.BlockSpec(shape, index_map, pipeline_mode=pl.Buffered(1))


def _rms(x, g):
    return x * lax.rsqrt(jnp.mean(x * x, axis=-1, keepdims=True) + EPS) * g


def _ffn_kernel(x_ref, g_ref, w1_ref, w3_ref, w2_ref, *rest, final_norm):
    o_ref = rest[-1]
    x = x_ref[...]
    h = _rms(x, g_ref[...]).astype(BF16)
    a = jnp.dot(h, w1_ref[...], preferred_element_type=F32)
    b = jnp.dot(h, w3_ref[...], preferred_element_type=F32)
    act = (a * jax.nn.sigmoid(a) * b).astype(BF16)
    y = x + 0.5 * jnp.dot(act, w2_ref[...], preferred_element_type=F32)
    if final_norm:
        y = _rms(y, rest[0][...])
    o_ref[...] = y


def _ffn(x, g, w1, w3, w2, final_g=None):
    T, D = x.shape
    F = w1.shape[1]
    tm = ROW_TILE
    row = pl.BlockSpec((tm, D), lambda i: (i, 0))
    vec = _resident((1, D), lambda i: (0, 0))
    in_specs = [row, vec, _resident((D, F), lambda i: (0, 0)), _resident((D, F), lambda i: (0, 0)),
                _resident((F, D), lambda i: (0, 0))]
    args = [x, g.reshape(1, D), w1, w3, w2]
    if final_g is not None:
        in_specs.append(vec)
        args.append(final_g.reshape(1, D))
    return pl.pallas_call(
        functools.partial(_ffn_kernel, final_norm=final_g is not None),
        grid=(T // tm,), in_specs=in_specs, out_specs=row,
        out_shape=jax.ShapeDtypeStruct((T, D), F32),
        compiler_params=_cparams("parallel"), name="ffn")(*args)


def _proj_kernel(x_ref, g_ref, w_ref, *rest, segs, gate_rows):
    x = x_ref[...]
    h = _rms(x, g_ref[...]).astype(BF16)
    outs = rest[1:] if gate_rows else rest
    off = 0
    for (width, scale), o_ref in zip(segs, outs):
        y = jnp.dot(h, w_ref[:, off:off + width], preferred_element_type=F32)
        if scale != 1.0:
            y = y * scale
        o_ref[...] = y.astype(o_ref.dtype)
        off += width
    if gate_rows:
        outs[len(segs)][...] = lax.dot_general(rest[0][...], h, (((1,), (1,)), ((), ())),
                                               preferred_element_type=F32)


def _norm_proj(x, g, w, segs, wg_t=None):
    T, D = x.shape
    N = w.shape[1]
    tm = ROW_TILE
    in_specs = [pl.BlockSpec((tm, D), lambda i: (i, 0)), _resident((1, D), lambda i: (0, 0)),
                _resident((D, N), lambda i: (0, 0))]
    args = [x, g.reshape(1, D), w]
    out_specs = [pl.BlockSpec((tm, wd), lambda i: (i, 0)) for wd, _, _ in segs]
    out_shape = [jax.ShapeDtypeStruct((T, wd), dt) for wd, dt, _ in segs]
    gate_rows = 0
    if wg_t is not None:
        gate_rows = wg_t.shape[0]
        in_specs.append(_resident((gate_rows, D), lambda i: (0, 0)))
        args.append(wg_t)
        out_specs.append(pl.BlockSpec((gate_rows, tm), lambda i: (0, i)))
        out_shape.append(jax.ShapeDtypeStruct((gate_rows, T), F32))
    return pl.pallas_call(
        functools.partial(_proj_kernel, segs=tuple((wd, sc) for wd, _, sc in segs), gate_rows=gate_rows),
        grid=(T // tm,), in_specs=in_specs, out_specs=out_specs, out_shape=out_shape,
        compiler_params=_cparams("parallel"), name="norm_proj")(*args)


def _out_c_kernel(x_ref, a_ref, w_ref, o_ref):
    o_ref[...] = x_ref[...] + jnp.dot(a_ref[...], w_ref[...], preferred_element_type=F32)


def _out_c(x, att, w):
    T, D = x.shape
    K = att.shape[1]
    tm = ROW_TILE
    row = pl.BlockSpec((tm, D), lambda i: (i, 0))
    return pl.pallas_call(
        _out_c_kernel, grid=(T // tm,),
        in_specs=[row, pl.BlockSpec((tm, K), lambda i: (i, 0)), _resident((K, D), lambda i: (0, 0))],
        out_specs=row, out_shape=jax.ShapeDtypeStruct((T, D), F32),
        compiler_params=_cparams("parallel"), name="out_proj_c")(x, att, w)


def _out_ab_kernel(x_ref, hn_ref, om_ref, o1_ref, o2_ref, o3_ref, l1_ref, l2_ref, l3_ref, wa_ref, wb_ref, o_ref):
    ma = (hn_ref[...] * jax.nn.sigmoid(om_ref[...])).astype(BF16)
    l1, l2, l3 = l1_ref[...], l2_ref[...], l3_ref[...]
    lm = jnp.maximum(jnp.maximum(l1, l2), l3)
    e1, e2, e3 = jnp.exp(l1 - lm), jnp.exp(l2 - lm), jnp.exp(l3 - lm)
    ob = ((e1 * o1_ref[...] + e2 * o2_ref[...] + e3 * o3_ref[...]) / (e1 + e2 + e3)).astype(BF16)
    o_ref[...] = (x_ref[...] + jnp.dot(ma, wa_ref[...], preferred_element_type=F32)
                  + jnp.dot(ob, wb_ref[...], preferred_element_type=F32))


def _out_ab(x, hn, om, outs, lses, wa, wb):
    T, D = x.shape
    tm = ROW_TILE
    row = pl.BlockSpec((tm, D), lambda i: (i, 0))
    half = pl.BlockSpec((tm, M_WIDTH), lambda i: (i, 0))
    wspec = _resident((M_WIDTH, D), lambda i: (0, 0))
    return pl.pallas_call(
        _out_ab_kernel, grid=(T // tm,),
        in_specs=[row] + [half] * 8 + [wspec, wspec],
        out_specs=row, out_shape=jax.ShapeDtypeStruct((T, D), F32),
        compiler_params=_cparams("parallel"), name="out_proj_ab")(x, hn, om, *outs, *lses, wa, wb)


def _attn_kernel(*refs, groups, q_tile, q_block, k_window, radius, dist_unit, seq_len, has_sink, with_lse):
    slope_ref = refs[0]
    pos = 1
    sink_ref = None
    if has_sink:
        sink_ref = refs[pos]
        pos += 1
    q_ref, k_ref, v_ref, o_ref = refs[pos:pos + 4]
    lse_ref = refs[pos + 4] if with_lse else None
    pair = pl.program_id(2)
    tile = pl.program_id(3)
    lane = lax.broadcasted_iota(jnp.int32, (q_block, LANES), 1)
    low = lane < HEAD_DIM

    def block(blk, carry):
        q0 = tile * q_tile + blk * q_block
        ks = pl.multiple_of(jnp.clip(q0 - radius, 0, seq_len - k_window), HEAD_DIM)
        rows = pl.ds(pl.multiple_of(blk * q_block, q_block), q_block)
        kt = k_ref[pl.ds(ks, k_window), :]
        vt = v_ref[pl.ds(ks, k_window), :]
        qpos = q0 + lax.broadcasted_iota(jnp.int32, (q_block, k_window), 0)
        kpos = ks + lax.broadcasted_iota(jnp.int32, (q_block, k_window), 1)
        adist = jnp.abs(kpos - qpos)
        negd = jnp.where(adist <= radius, -(dist_unit * adist).astype(F32), -jnp.inf)
        for g in range(groups):
            qt = q_ref[rows, g * LANES:(g + 1) * LANES]
            res = []
            for j in range(2):
                head = (2 * pair + j) * groups + g
                qm = jnp.where(low if j == 0 else jnp.logical_not(low), qt, jnp.zeros_like(qt))
                s = lax.dot_general(qm, kt, (((1,), (1,)), ((), ())), preferred_element_type=F32)
                s = s + slope_ref[head] * negd
                m = jnp.max(s, axis=-1, keepdims=True)
                if has_sink:
                    m = jnp.maximum(m, sink_ref[head])
                p = jnp.exp(s - m)
                den = jnp.sum(p, axis=-1, keepdims=True)
                if has_sink:
                    den = den + jnp.exp(sink_ref[head] - m)
                o = jnp.dot(p.astype(BF16), vt, preferred_element_type=F32) / den
                res.append((o, m + jnp.log(den)))
            o_ref[rows, g * LANES:(g + 1) * LANES] = jnp.where(low, res[0][0], res[1][0]).astype(o_ref.dtype)
            if with_lse:
                lse_ref[rows, g * LANES:(g + 1) * LANES] = jnp.where(low, res[0][1], res[1][1])
        return carry

    lax.fori_loop(0, q_tile // q_block, block, 0)


def _banded_attention(q, k, v, slopes, sink, *, dilation, groups, radius, with_lse, out_dtype):
    Bn, S, WQ = q.shape
    WK = k.shape[2]
    d = dilation
    Lv = S // d
    pairs = WK // LANES
    q_tile = min(ATTN_Q_TILE, Lv)
    q_block = ATTN_Q_BLOCK
    k_window = q_block + 2 * radius
    qw = groups * LANES
    qv, kv, vv = (t.reshape(Bn, Lv, d * t.shape[2]) for t in (q, k, v))
    smem = pl.BlockSpec(memory_space=pltpu.SMEM)
    qspec = pl.BlockSpec((None, q_tile, qw), lambda b, r, p, i: (b, i, r * pairs + p))
    kspec = pl.BlockSpec((None, Lv, LANES), lambda b, r, p, i: (b, 0, r * pairs + p))
    in_specs = [smem] + ([smem] if sink is not None else []) + [qspec, kspec, kspec]
    args = [slopes] + ([sink] if sink is not None else []) + [qv, kv, vv]
    out_specs = [qspec]
    out_shape = [jax.ShapeDtypeStruct(qv.shape, out_dtype)]
    if with_lse:
        out_specs.append(qspec)
        out_shape.append(jax.ShapeDtypeStruct(qv.shape, F32))
    res = pl.pallas_call(
        functools.partial(_attn_kernel, groups=groups, q_tile=q_tile, q_block=q_block, k_window=k_window,
                          radius=radius, dist_unit=d, seq_len=Lv, has_sink=sink is not None, with_lse=with_lse),
        grid=(Bn, d, pairs, Lv // q_tile), in_specs=in_specs, out_specs=out_specs, out_shape=out_shape,
        compiler_params=_cparams("parallel", "parallel", "parallel", "arbitrary"),
        name=f"banded_attention_d{d}")(*args)
    return [t.reshape(Bn, S, WQ) for t in res]


def _alibi_slopes(n):
    return jnp.exp2(-8.0 * jnp.arange(1, n + 1, dtype=F32) / n)


def _log_sigmoid(x):
    return jnp.minimum(x, 0.0) - jnp.log1p(jnp.exp(-jnp.abs(x)))


def _mlstm_chunk(q, k, v, i_r, lf_r, state, *, reverse):
    ct, n, m = state
    L = q.shape[0]
    row = lax.broadcasted_iota(jnp.int32, (L, L), 0)
    col = lax.broadcasted_iota(jnp.int32, (L, L), 1)
    seen = (col >= row) if reverse else (col <= row)
    diag = col == row
    b_c = jnp.sum(jnp.where(seen, lf_r, 0.0), axis=1, keepdims=True)
    b_r = jnp.sum(jnp.where(diag, b_c, 0.0), axis=0, keepdims=True)
    i_c = jnp.sum(jnp.where(diag, i_r, 0.0), axis=1, keepdims=True)
    b_last = jnp.sum(lf_r, axis=1, keepdims=True)

    dmat = jnp.where(seen, b_c + (i_r - b_r), -jnp.inf)
    m_inter = b_c + m
    m_t = jnp.maximum(m_inter, jnp.max(dmat, axis=1, keepdims=True))
    qk = lax.dot_general(q, k, (((1,), (1,)), ((), ())), preferred_element_type=F32)
    w = jnp.exp(dmat - m_t) * qk
    s_inter = jnp.exp(m_inter - m_t)
    inter = jnp.dot(q, ct.astype(BF16), preferred_element_type=F32)
    num = jnp.dot(w.astype(BF16), v, preferred_element_type=F32) + s_inter * inter
    den = jnp.sum(w, axis=1, keepdims=True) + s_inter * jnp.sum(q.astype(F32) * n, axis=1, keepdims=True)
    h = num / jnp.maximum(jnp.abs(den), jnp.exp(-m_t))

    a_c = b_last - b_c + i_c
    m_loc = jnp.max(a_c, axis=0, keepdims=True)
    kw = k.astype(F32) * jnp.exp(a_c - m_loc)
    ct_chunk = lax.dot_general(kw.astype(BF16), v, (((0,), (0,)), ((), ())), preferred_element_type=F32)
    n_chunk = jnp.sum(kw, axis=0, keepdims=True)
    m_new = jnp.maximum(b_last + m, m_loc)
    s_old = jnp.exp(b_last + m - m_new)
    s_new = jnp.exp(m_loc - m_new)
    return h, (s_old * ct + s_new * ct_chunk, s_old * n + s_new * n_chunk, m_new)


def _mlstm_kernel(q_ref, k_ref, v_ref, gr_ref, gb_ref, cwq_ref, cwk_ref, cbq_ref, cbk_ref, hg_ref, o_ref,
                  qs_ref, ks_ref, *, seq_len, chunk, conv_rows):
    S, L, R = seq_len, chunk, conv_rows
    E = M_HEAD_DIM
    n_conv = S // R
    halo = SUBLANES

    def conv_act(src_ref, w_ref, b_ref, r):
        r0 = pl.multiple_of(r * R, R)
        cur = src_ref[pl.ds(r0, R), :]
        prev = src_ref[pl.ds(pl.multiple_of(jnp.maximum(r0 - halo, 0), halo), halo), :]
        nxt = src_ref[pl.ds(pl.multiple_of(jnp.minimum(r0 + R, S - halo), halo), halo), :]
        prev = jnp.where(r > 0, prev, 0.0)
        nxt = jnp.where(r < n_conv - 1, nxt, 0.0)
        ext = jnp.concatenate([prev, cur, nxt], axis=0)
        n_ext = R + 2 * halo
        acc = b_ref[...] + ext[halo:halo + R] * w_ref[M_CONV // 2:M_CONV // 2 + 1, :]
        for j in range(M_CONV):
            off = j - M_CONV // 2
            if off == 0:
                continue
            shifted = pltpu.roll(ext, (-off) % n_ext, axis=0)
            acc = acc + shifted[halo:halo + R] * w_ref[j:j + 1, :]
        return acc * jax.nn.sigmoid(acc), r0

    def conv_step(r, carry):
        qa, r0 = conv_act(q_ref, cwq_ref, cbq_ref, r)
        qs_ref[pl.ds(r0, R), :] = qa.astype(BF16)
        ka, _ = conv_act(k_ref, cwk_ref, cbk_ref, r)
        ks_ref[pl.ds(r0, R), :] = (ka * (E ** -0.5)).astype(BF16)
        return carry

    lax.fori_loop(0, n_conv, conv_step, 0)

    n_chunks = S // L
    gb = gb_ref[...]

    def run_chunk(c, state, direction):
        rows = pl.ds(pl.multiple_of(c * L, L), L)
        g = gr_ref[:, rows] + gb
        i_r = g[direction:direction + 1]
        lf_r = _log_sigmoid(g[2 + direction:3 + direction])
        h, state = _mlstm_chunk(qs_ref[rows, :], ks_ref[rows, :], v_ref[rows, :], i_r, lf_r, state,
                                reverse=direction == 1)
        return rows, h, state

    def finish(tot):
        return tot * lax.rsqrt(jnp.mean(tot * tot, axis=-1, keepdims=True) + EPS) * hg_ref[...]

    def first_half(it, states):
        sf, sb = states
        rows, h, sf = run_chunk(it, sf, 0)
        o_ref[rows, :] = h
        rows, h, sb = run_chunk(n_chunks - 1 - it, sb, 1)
        o_ref[rows, :] = h
        return sf, sb

    def second_half(it, states):
        sf, sb = states
        rows, h, sf = run_chunk(it, sf, 0)
        o_ref[rows, :] = finish(o_ref[rows, :] + h)
        rows, h, sb = run_chunk(n_chunks - 1 - it, sb, 1)
        o_ref[rows, :] = finish(o_ref[rows, :] + h)
        return sf, sb

    zero = (jnp.zeros((E, E), F32), jnp.zeros((1, E), F32), jnp.zeros((1, 1), F32))
    states = lax.fori_loop(0, n_chunks // 2, first_half, (zero, zero))
    lax.fori_loop(n_chunks // 2, n_chunks, second_half, states)


def _mlstm(qk_pre, v_m, g_rows, gate_b, conv_w, conv_b, hnorm_g, Bn, S):
    H, E = M_HEADS, M_HEAD_DIM
    T = Bn * S
    seq = lambda off: pl.BlockSpec((None, S, E), lambda b, h: (b, 0, off + h))
    par = lambda rows, off: pl.BlockSpec((rows, E), lambda b, h: (0, off + h))
    qk3 = qk_pre.reshape(Bn, S, 2 * M_WIDTH)
    out = pl.pallas_call(
        functools.partial(_mlstm_kernel, seq_len=S, chunk=MLSTM_CHUNK, conv_rows=CONV_ROWS),
        grid=(Bn, H),
        in_specs=[seq(0), seq(H), seq(0),
                  pl.BlockSpec((None, 4, S), lambda b, h: (h, 0, b)),
                  pl.BlockSpec((None, 4, 1), lambda b, h: (h, 0, 0)),
                  par(M_CONV, 0), par(M_CONV, H), par(1, 0), par(1, H), par(1, 0)],
        out_specs=seq(0),
        out_shape=jax.ShapeDtypeStruct((Bn, S, M_WIDTH), F32),
        scratch_shapes=[pltpu.VMEM((S, E), BF16), pltpu.VMEM((S, E), BF16)],
        compiler_params=_cparams("parallel", "parallel"), name="mlstm")(
            qk3, qk3, v_m.reshape(Bn, S, M_WIDTH), g_rows.reshape(H, 4, T), gate_b, conv_w, conv_w,
            conv_b.reshape(1, 2 * M_WIDTH), conv_b.reshape(1, 2 * M_WIDTH), hnorm_g.reshape(1, M_WIDTH))
    return out.reshape(T, M_WIDTH)


def _mixer_ab(x, g, w_in, conv_w, conv_b, gate_b, hnorm_g, w_out, Bn, S):
    T = Bn * S
    MW, BW = M_WIDTH, B_WIDTH
    g0 = 4 * MW
    n_gate = 4 * M_HEADS
    w_main = jnp.concatenate([w_in[:, :g0], w_in[:, g0 + n_gate:]], axis=1).astype(BF16)
    wg_t = w_in[:, g0:g0 + n_gate].reshape(-1, 4, M_HEADS).transpose(2, 1, 0).reshape(n_gate, -1).astype(BF16)
    gate_b = gate_b.reshape(4, M_HEADS).T.reshape(M_HEADS, 4, 1).astype(F32)
    segs = [(2 * MW, F32, 1.0), (MW, BF16, 1.0), (MW, F32, 1.0),
            (BW, BF16, HEAD_DIM ** -0.5), (BW, BF16, 1.0), (BW, BF16, 1.0)]
    qk_pre, v_m, o_m, q_b, k_b, v_b, g_rows = _norm_proj(x, g, w_main, segs, wg_t)

    hn = _mlstm(qk_pre, v_m, g_rows, gate_b, conv_w, conv_b, hnorm_g, Bn, S)

    slopes = _alibi_slopes(B_HEADS)
    q3, k3, v3 = (t.reshape(Bn, S, BW) for t in (q_b, k_b, v_b))
    outs, lses = [], []
    for d in DILATIONS:
        o, lse = _banded_attention(q3, k3, v3, slopes, None, dilation=d, groups=1, radius=B_RADIUS,
                                   with_lse=True, out_dtype=F32)
        outs.append(o.reshape(T, BW))
        lses.append(lse.reshape(T, BW))
    w_out = w_out.astype(BF16)
    return _out_ab(x, hn, o_m, outs, lses, w_out[:MW], w_out[MW:])


def _c_head_order():
    order = []
    for pair in range(C_KV_HEADS // 2):
        for slot in range(C_GROUP):
            for j in range(2):
                order.append((2 * pair + j) * C_GROUP + slot)
    return jnp.array(order, dtype=jnp.int32)


def _mixer_c(x, g, w_in, sink, w_out, Bn, S):
    T, D = x.shape
    QW = C_HEADS * HEAD_DIM
    KW = C_KV_HEADS * HEAD_DIM
    order = _c_head_order()
    wq = w_in[:, :QW].reshape(D, C_HEADS, HEAD_DIM)[:, order].reshape(D, QW)
    w_main = jnp.concatenate([wq, w_in[:, QW:]], axis=1).astype(BF16)
    w_out = w_out.reshape(C_HEADS, HEAD_DIM, D)[order].reshape(QW, D).astype(BF16)
    segs = [(QW, BF16, HEAD_DIM ** -0.5), (KW, BF16, 1.0), (KW, BF16, 1.0)]
    q, k, v = _norm_proj(x, g, w_main, segs)
    (o,) = _banded_attention(q.reshape(Bn, S, QW), k.reshape(Bn, S, KW), v.reshape(Bn, S, KW),
                             _alibi_slopes(C_HEADS), sink.astype(F32), dilation=1, groups=C_GROUP,
                             radius=C_RADIUS, with_lse=False, out_dtype=BF16)
    return _out_c(x, o.reshape(T, QW), w_out)


def kernel(x, norm_g, ffn_w1, ffn_w3, ffn_w2, ab_w_in, ab_conv_w, ab_conv_b, ab_gate_b, ab_hnorm_g, ab_w_out,
           c_w_in, c_sink, c_w_out, final_g):
    Bn, S, D = x.shape
    depth = norm_g.shape[0]
    w1, w3, w2 = (w.astype(BF16) for w in (ffn_w1, ffn_w3, ffn_w2))
    x = x.reshape(Bn * S, D)
    for l in range(depth):
        j = l // 2
        x = _ffn(x, norm_g[l, 0], w1[l, 0], w3[l, 0], w2[l, 0])
        if l % 2 == 0:
            x = _mixer_ab(x, norm_g[l, 1], ab_w_in[j], ab_conv_w[j], ab_conv_b[j], ab_gate_b[j], ab_hnorm_g[j],
                          ab_w_out[j], Bn, S)
        else:
            x = _mixer_c(x, norm_g[l, 1], c_w_in[j], c_sink[j], c_w_out[j], Bn, S)
        x = _ffn(x, norm_g[l, 2], w1[l, 1], w3[l, 1], w2[l, 1], final_g if l == depth - 1 else None)
    return x.reshape(Bn, S, D)
```

```python
import functools

import jax
import jax.numpy as jnp
from jax import lax
from jax.experimental import pallas as pl
from jax.experimental.pallas import tpu as pltpu

F32 = jnp.float32
BF16 = jnp.bfloat16

EPS = 1e-6
M_HEADS = 4
M_HEAD_DIM = 128
M_WIDTH = M_HEADS * M_HEAD_DIM
M_CONV = 5
B_HEADS = 8
B_WIDTH = 512
DILATIONS = (1, 4, 16)
B_RADIUS = 64
C_HEADS = 16
C_KV_HEADS = 4
C_GROUP = C_HEADS // C_KV_HEADS
C_RADIUS = 128
HEAD_DIM = 64
LANES = 128
SUBLANES = 8

VMEM_LIMIT_BYTES = 56 * 1024 * 1024
ROW_TILE = 512
ATTN_Q_CHAINS = 16
ATTN_Q_BLOCK = 128
MLSTM_CHUNK = 128
CONV_ROWS = 512


def _cparams(*sem):
    return pltpu.CompilerParams(dimension_semantics=sem, vmem_limit_bytes=VMEM_LIMIT_BYTES)


def _resident(shape, index_map):
    return pl.BlockSpec(shape, index_map, pipeline_mode=pl.Buffered(1))


def _rms(x, g):
    return x * lax.rsqrt(jnp.mean(x * x, axis=-1, keepdims=True) + EPS) * g


def _ffn_kernel(x_ref, g_ref, w1_ref, w3_ref, w2_ref, *rest, final_norm):
    o_ref = rest[-1]
    x = x_ref[...]
    h = _rms(x, g_ref[...]).astype(BF16)
    a = jnp.dot(h, w1_ref[...], preferred_element_type=F32)
    b = jnp.dot(h, w3_ref[...], preferred_element_type=F32)
    act = (a * jax.nn.sigmoid(a) * b).astype(BF16)
    y = x + 0.5 * jnp.dot(act, w2_ref[...], preferred_element_type=F32)
    if final_norm:
        y = _rms(y, rest[0][...])
    o_ref[...] = y


def _ffn(x, g, w1, w3, w2, final_g=None):
    T, D = x.shape
    F = w1.shape[1]
    tm = ROW_TILE
    row = pl.BlockSpec((tm, D), lambda i: (i, 0))
    vec = _resident((1, D), lambda i: (0, 0))
    in_specs = [row, vec, _resident((D, F), lambda i: (0, 0)), _resident((D, F), lambda i: (0, 0)),
                _resident((F, D), lambda i: (0, 0))]
    args = [x, g.reshape(1, D), w1, w3, w2]
    if final_g is not None:
        in_specs.append(vec)
        args.append(final_g.reshape(1, D))
    return pl.pallas_call(
        functools.partial(_ffn_kernel, final_norm=final_g is not None),
        grid=(T // tm,), in_specs=in_specs, out_specs=row,
        out_shape=jax.ShapeDtypeStruct((T, D), F32),
        compiler_params=_cparams("parallel"), name="ffn")(*args)


def _proj_kernel(x_ref, g_ref, w_ref, *rest, segs, gate_rows):
    x = x_ref[...]
    h = _rms(x, g_ref[...]).astype(BF16)
    outs = rest[1:] if gate_rows else rest
    off = 0
    for (width, scale), o_ref in zip(segs, outs):
        y = jnp.dot(h, w_ref[:, off:off + width], preferred_element_type=F32)
        if scale != 1.0:
            y = y * scale
        o_ref[...] = y.astype(o_ref.dtype)
        off += width
    if gate_rows:
        outs[len(segs)][...] = lax.dot_general(rest[0][...], h, (((1,), (1,)), ((), ())),
                                               preferred_element_type=F32)


def _norm_proj(x, g, w, segs, wg_t=None):
    T, D = x.shape
    N = w.shape[1]
    tm = ROW_TILE
    in_specs = [pl.BlockSpec((tm, D), lambda i: (i, 0)), _resident((1, D), lambda i: (0, 0)),
                _resident((D, N), lambda i: (0, 0))]
    args = [x, g.reshape(1, D), w]
    out_specs = [pl.BlockSpec((tm, wd), lambda i: (i, 0)) for wd, _, _ in segs]
    out_shape = [jax.ShapeDtypeStruct((T, wd), dt) for wd, dt, _ in segs]
    gate_rows = 0
    if wg_t is not None:
        gate_rows = wg_t.shape[0]
        in_specs.append(_resident((gate_rows, D), lambda i: (0, 0)))
        args.append(wg_t)
        out_specs.append(pl.BlockSpec((gate_rows, tm), lambda i: (0, i)))
        out_shape.append(jax.ShapeDtypeStruct((gate_rows, T), F32))
    return pl.pallas_call(
        functools.partial(_proj_kernel, segs=tuple((wd, sc) for wd, _, sc in segs), gate_rows=gate_rows),
        grid=(T // tm,), in_specs=in_specs, out_specs=out_specs, out_shape=out_shape,
        compiler_params=_cparams("parallel"), name="norm_proj")(*args)


def _out_c_kernel(x_ref, a_ref, w_ref, o_ref):
    o_ref[...] = x_ref[...] + jnp.dot(a_ref[...], w_ref[...], preferred_element_type=F32)


def _out_c(x, att, w):
    T, D = x.shape
    K = att.shape[1]
    tm = ROW_TILE
    row = pl.BlockSpec((tm, D), lambda i: (i, 0))
    return pl.pallas_call(
        _out_c_kernel, grid=(T // tm,),
        in_specs=[row, pl.BlockSpec((tm, K), lambda i: (i, 0)), _resident((K, D), lambda i: (0, 0))],
        out_specs=row, out_shape=jax.ShapeDtypeStruct((T, D), F32),
        compiler_params=_cparams("parallel"), name="out_proj_c")(x, att, w)


def _out_ab_kernel(x_ref, hn_ref, om_ref, o1_ref, o2_ref, o3_ref, l1_ref, l2_ref, l3_ref, wa_ref, wb_ref, o_ref):
    ma = (hn_ref[...] * jax.nn.sigmoid(om_ref[...])).astype(BF16)
    l1, l2, l3 = l1_ref[...], l2_ref[...], l3_ref[...]
    lm = jnp.maximum(jnp.maximum(l1, l2), l3)
    e1, e2, e3 = jnp.exp(l1 - lm), jnp.exp(l2 - lm), jnp.exp(l3 - lm)
    ob = ((e1 * o1_ref[...] + e2 * o2_ref[...] + e3 * o3_ref[...]) / (e1 + e2 + e3)).astype(BF16)
    o_ref[...] = (x_ref[...] + jnp.dot(ma, wa_ref[...], preferred_element_type=F32)
                  + jnp.dot(ob, wb_ref[...], preferred_element_type=F32))


def _out_ab(x, hn, om, outs, lses, wa, wb):
    T, D = x.shape
    tm = ROW_TILE
    row = pl.BlockSpec((tm, D), lambda i: (i, 0))
    half = pl.BlockSpec((tm, M_WIDTH), lambda i: (i, 0))
    wspec = _resident((M_WIDTH, D), lambda i: (0, 0))
    return pl.pallas_call(
        _out_ab_kernel, grid=(T // tm,),
        in_specs=[row] + [half] * 8 + [wspec, wspec],
        out_specs=row, out_shape=jax.ShapeDtypeStruct((T, D), F32),
        compiler_params=_cparams("parallel"), name="out_proj_ab")(x, hn, om, *outs, *lses, wa, wb)


def _attn_kernel(*refs, groups, q_tile, q_block, k_window, radius, dist_unit, seq_len, has_sink, with_lse):
    slope_ref = refs[0]
    pos = 1
    sink_ref = None
    if has_sink:
        sink_ref = refs[pos]
        pos += 1
    q_ref, k_ref, v_ref, o_ref = refs[pos:pos + 4]
    pos += 4
    lse_ref = None
    if with_lse:
        lse_ref = refs[pos]
        pos += 1
    bias_ref = refs[pos]
    pair = pl.program_id(0)
    tile = pl.program_id(3)

    @pl.when((pl.program_id(1) == 0) & (pl.program_id(2) == 0) & (tile == 0))
    def _():
        sub = lax.broadcasted_iota(jnp.int32, (q_block, k_window), 0)
        ln = lax.broadcasted_iota(jnp.int32, (q_block, k_window), 1)
        for variant in range(3):
            adist = jnp.abs(ln - sub - variant * radius)
            negd = jnp.where(adist <= radius, -(dist_unit * adist).astype(F32), -jnp.inf)
            for g in range(groups):
                for j in range(2):
                    bias_ref[variant, 2 * g + j] = slope_ref[(2 * pair + j) * groups + g] * negd

    lane = lax.broadcasted_iota(jnp.int32, (q_block, LANES), 1)
    low = lane < HEAD_DIM
    ones = jnp.ones((k_window, LANES), BF16)
    for blk in range(q_tile // q_block):
        q0 = tile * q_tile + blk * q_block
        ks = pl.multiple_of(jnp.clip(q0 - radius, 0, seq_len - k_window), HEAD_DIM)
        variant = (q0 - ks) // radius
        rows = slice(blk * q_block, (blk + 1) * q_block)
        kt = k_ref[pl.ds(ks, k_window), :]
        vt = jnp.concatenate([v_ref[pl.ds(ks, k_window), :], ones], axis=1)
        for g in range(groups):
            qt = q_ref[rows, g * LANES:(g + 1) * LANES]
            res = []
            for j in range(2):
                head = (2 * pair + j) * groups + g
                qm = jnp.where(low if j == 0 else jnp.logical_not(low), qt, jnp.zeros_like(qt))
                s = lax.dot_general(qm, kt, (((1,), (1,)), ((), ())), preferred_element_type=F32)
                s = s + bias_ref[variant, 2 * g + j]
                m = jnp.max(s, axis=-1, keepdims=True)
                if has_sink:
                    m = jnp.maximum(m, sink_ref[head])
                p = jnp.exp(s - m).astype(BF16)
                od = jnp.dot(p, vt, preferred_element_type=F32)
                den = od[:, LANES:]
                if has_sink:
                    den = den + jnp.exp(sink_ref[head] - m)
                res.append((od[:, :LANES] / den, m + jnp.log(den)))
            o_ref[rows, g * LANES:(g + 1) * LANES] = jnp.where(low, res[0][0], res[1][0]).astype(o_ref.dtype)
            if with_lse:
                lse_ref[rows, g * LANES:(g + 1) * LANES] = jnp.where(low, res[0][1], res[1][1])


def _banded_attention(q, k, v, slopes, sink, *, dilation, groups, radius, with_lse, out_dtype):
    Bn, S, WQ = q.shape
    WK = k.shape[2]
    d = dilation
    Lv = S // d
    pairs = WK // LANES
    q_block = ATTN_Q_BLOCK
    q_tile = min(ATTN_Q_CHAINS // (2 * groups) * q_block, Lv)
    k_window = q_block + 2 * radius
    assert radius <= q_block and q_block % radius == 0 and Lv % q_tile == 0 and k_window <= Lv
    qw = groups * LANES
    qv, kv, vv = (t.reshape(Bn, Lv, d * t.shape[2]) for t in (q, k, v))
    smem = pl.BlockSpec(memory_space=pltpu.SMEM)
    qspec = pl.BlockSpec((None, q_tile, qw), lambda p, b, r, i: (b, i, r * pairs + p))
    kspec = pl.BlockSpec((None, Lv, LANES), lambda p, b, r, i: (b, 0, r * pairs + p))
    in_specs = [smem] + ([smem] if sink is not None else []) + [qspec, kspec, kspec]
    args = [slopes] + ([sink] if sink is not None else []) + [qv, kv, vv]
    out_specs = [qspec]
    out_shape = [jax.ShapeDtypeStruct(qv.shape, out_dtype)]
    if with_lse:
        out_specs.append(qspec)
        out_shape.append(jax.ShapeDtypeStruct(qv.shape, F32))
    res = pl.pallas_call(
        functools.partial(_attn_kernel, groups=groups, q_tile=q_tile, q_block=q_block, k_window=k_window,
                          radius=radius, dist_unit=d, seq_len=Lv, has_sink=sink is not None, with_lse=with_lse),
        grid=(pairs, Bn, d, Lv // q_tile), in_specs=in_specs, out_specs=out_specs, out_shape=out_shape,
        scratch_shapes=[pltpu.VMEM((3, 2 * groups, q_block, k_window), F32)],
        compiler_params=_cparams("arbitrary", "arbitrary", "arbitrary", "arbitrary"),
        name=f"banded_attention_d{d}")(*args)
    return [t.reshape(Bn, S, WQ) for t in res]


def _alibi_slopes(n):
    return jnp.exp2(-8.0 * jnp.arange(1, n + 1, dtype=F32) / n)


def _log_sigmoid(x):
    return jnp.minimum(x, 0.0) - jnp.log1p(jnp.exp(-jnp.abs(x)))


def _mlstm_chunk(q, k, v, i_r, lf_r, state, *, reverse):
    ct, n, m = state
    L = q.shape[0]
    row = lax.broadcasted_iota(jnp.int32, (L, L), 0)
    col = lax.broadcasted_iota(jnp.int32, (L, L), 1)
    seen = (col >= row) if reverse else (col <= row)
    diag = col == row
    b_c = jnp.sum(jnp.where(seen, lf_r, 0.0), axis=1, keepdims=True)
    b_r = jnp.sum(jnp.where(diag, b_c, 0.0), axis=0, keepdims=True)
    i_c = jnp.sum(jnp.where(diag, i_r, 0.0), axis=1, keepdims=True)
    b_last = jnp.sum(lf_r, axis=1, keepdims=True)

    dmat = jnp.where(seen, b_c + (i_r - b_r), -jnp.inf)
    m_inter = b_c + m
    m_t = jnp.maximum(m_inter, jnp.max(dmat, axis=1, keepdims=True))
    qk = lax.dot_general(q, k, (((1,), (1,)), ((), ())), preferred_element_type=F32)
    w = jnp.exp(dmat - m_t) * qk
    s_inter = jnp.exp(m_inter - m_t)
    inter = jnp.dot(q, ct.astype(BF16), preferred_element_type=F32)
    num = jnp.dot(w.astype(BF16), v, preferred_element_type=F32) + s_inter * inter
    den = jnp.sum(w, axis=1, keepdims=True) + s_inter * jnp.sum(q.astype(F32) * n, axis=1, keepdims=True)
    h = num / jnp.maximum(jnp.abs(den), jnp.exp(-m_t))

    a_c = b_last - b_c + i_c
    m_loc = jnp.max(a_c, axis=0, keepdims=True)
    kw = k.astype(F32) * jnp.exp(a_c - m_loc)
    ct_chunk = lax.dot_general(kw.astype(BF16), v, (((0,), (0,)), ((), ())), preferred_element_type=F32)
    n_chunk = jnp.sum(kw, axis=0, keepdims=True)
    m_new = jnp.maximum(b_last + m, m_loc)
    s_old = jnp.exp(b_last + m - m_new)
    s_new = jnp.exp(m_loc - m_new)
    return h, (s_old * ct + s_new * ct_chunk, s_old * n + s_new * n_chunk, m_new)


def _mlstm_kernel(q_ref, k_ref, v_ref, gr_ref, gb_ref, cwq_ref, cwk_ref, cbq_ref, cbk_ref, hg_ref, o_ref,
                  qs_ref, ks_ref, *, seq_len, chunk, conv_rows):
    S, L, R = seq_len, chunk, conv_rows
    E = M_HEAD_DIM
    n_conv = S // R
    halo = SUBLANES

    def conv_act(src_ref, w_ref, b_ref, r):
        r0 = pl.multiple_of(r * R, R)
        cur = src_ref[pl.ds(r0, R), :]
        prev = src_ref[pl.ds(pl.multiple_of(jnp.maximum(r0 - halo, 0), halo), halo), :]
        nxt = src_ref[pl.ds(pl.multiple_of(jnp.minimum(r0 + R, S - halo), halo), halo), :]
        prev = jnp.where(r > 0, prev, 0.0)
        nxt = jnp.where(r < n_conv - 1, nxt, 0.0)
        ext = jnp.concatenate([prev, cur, nxt], axis=0)
        n_ext = R + 2 * halo
        acc = b_ref[...] + ext[halo:halo + R] * w_ref[M_CONV // 2:M_CONV // 2 + 1, :]
        for j in range(M_CONV):
            off = j - M_CONV // 2
            if off == 0:
                continue
            shifted = pltpu.roll(ext, (-off) % n_ext, axis=0)
            acc = acc + shifted[halo:halo + R] * w_ref[j:j + 1, :]
        return acc * jax.nn.sigmoid(acc), r0

    def conv_step(r, carry):
        qa, r0 = conv_act(q_ref, cwq_ref, cbq_ref, r)
        qs_ref[pl.ds(r0, R), :] = qa.astype(BF16)
        ka, _ = conv_act(k_ref, cwk_ref, cbk_ref, r)
        ks_ref[pl.ds(r0, R), :] = (ka * (E ** -0.5)).astype(BF16)
        return carry

    lax.fori_loop(0, n_conv, conv_step, 0)

    n_chunks = S // L
    gb = gb_ref[...]

    def run_chunk(c, state, direction):
        rows = pl.ds(pl.multiple_of(c * L, L), L)
        g = gr_ref[:, rows] + gb
        i_r = g[direction:direction + 1]
        lf_r = _log_sigmoid(g[2 + direction:3 + direction])
        h, state = _mlstm_chunk(qs_ref[rows, :], ks_ref[rows, :], v_ref[rows, :], i_r, lf_r, state,
                                reverse=direction == 1)
        return rows, h, state

    def finish(tot):
        return tot * lax.rsqrt(jnp.mean(tot * tot, axis=-1, keepdims=True) + EPS) * hg_ref[...]

    def first_half(it, states):
        sf, sb = states
        rows, h, sf = run_chunk(it, sf, 0)
        o_ref[rows, :] = h
        rows, h, sb = run_chunk(n_chunks - 1 - it, sb, 1)
        o_ref[rows, :] = h
        return sf, sb

    def second_half(it, states):
        sf, sb = states
        rows, h, sf = run_chunk(it, sf, 0)
        o_ref[rows, :] = finish(o_ref[rows, :] + h)
        rows, h, sb = run_chunk(n_chunks - 1 - it, sb, 1)
        o_ref[rows, :] = finish(o_ref[rows, :] + h)
        return sf, sb

    zero = (jnp.zeros((E, E), F32), jnp.zeros((1, E), F32), jnp.zeros((1, 1), F32))
    states = lax.fori_loop(0, n_chunks // 2, first_half, (zero, zero))
    lax.fori_loop(n_chunks // 2, n_chunks, second_half, states)


def _mlstm(qk_pre, v_m, g_rows, gate_b, conv_w, conv_b, hnorm_g, Bn, S):
    H, E = M_HEADS, M_HEAD_DIM
    T = Bn * S
    seq = lambda off: pl.BlockSpec((None, S, E), lambda b, h: (b, 0, off + h))
    par = lambda rows, off: pl.BlockSpec((rows, E), lambda b, h: (0, off + h))
    qk3 = qk_pre.reshape(Bn, S, 2 * M_WIDTH)
    out = pl.pallas_call(
        functools.partial(_mlstm_kernel, seq_len=S, chunk=MLSTM_CHUNK, conv_rows=CONV_ROWS),
        grid=(Bn, H),
        in_specs=[seq(0), seq(H), seq(0),
                  pl.BlockSpec((None, 4, S), lambda b, h: (h, 0, b)),
                  pl.BlockSpec((None, 4, 1), lambda b, h: (h, 0, 0)),
                  par(M_CONV, 0), par(M_CONV, H), par(1, 0), par(1, H), par(1, 0)],
        out_specs=seq(0),
        out_shape=jax.ShapeDtypeStruct((Bn, S, M_WIDTH), F32),
        scratch_shapes=[pltpu.VMEM((S, E), BF16), pltpu.VMEM((S, E), BF16)],
        compiler_params=_cparams("parallel", "parallel"), name="mlstm")(
            qk3, qk3, v_m.reshape(Bn, S, M_WIDTH), g_rows.reshape(H, 4, T), gate_b, conv_w, conv_w,
            conv_b.reshape(1, 2 * M_WIDTH), conv_b.reshape(1, 2 * M_WIDTH), hnorm_g.reshape(1, M_WIDTH))
    return out.reshape(T, M_WIDTH)


def _mixer_ab(x, g, w_in, conv_w, conv_b, gate_b, hnorm_g, w_out, Bn, S):
    T = Bn * S
    MW, BW = M_WIDTH, B_WIDTH
    g0 = 4 * MW
    n_gate = 4 * M_HEADS
    w_main = jnp.concatenate([w_in[:, :g0], w_in[:, g0 + n_gate:]], axis=1).astype(BF16)
    wg_t = w_in[:, g0:g0 + n_gate].reshape(-1, 4, M_HEADS).transpose(2, 1, 0).reshape(n_gate, -1).astype(BF16)
    gate_b = gate_b.reshape(4, M_HEADS).T.reshape(M_HEADS, 4, 1).astype(F32)
    segs = [(2 * MW, F32, 1.0), (MW, BF16, 1.0), (MW, F32, 1.0),
            (BW, BF16, HEAD_DIM ** -0.5), (BW, BF16, 1.0), (BW, BF16, 1.0)]
    qk_pre, v_m, o_m, q_b, k_b, v_b, g_rows = _norm_proj(x, g, w_main, segs, wg_t)

    hn = _mlstm(qk_pre, v_m, g_rows, gate_b, conv_w, conv_b, hnorm_g, Bn, S)

    slopes = _alibi_slopes(B_HEADS)
    q3, k3, v3 = (t.reshape(Bn, S, BW) for t in (q_b, k_b, v_b))
    outs, lses = [], []
    for d in DILATIONS:
        o, lse = _banded_attention(q3, k3, v3, slopes, None, dilation=d, groups=1, radius=B_RADIUS,
                                   with_lse=True, out_dtype=F32)
        outs.append(o.reshape(T, BW))
        lses.append(lse.reshape(T, BW))
    w_out = w_out.astype(BF16)
    return _out_ab(x, hn, o_m, outs, lses, w_out[:MW], w_out[MW:])


def _c_head_order():
    order = []
    for pair in range(C_KV_HEADS // 2):
        for slot in range(C_GROUP):
            for j in range(2):
                order.append((2 * pair + j) * C_GROUP + slot)
    return jnp.array(order, dtype=jnp.int32)


def _mixer_c(x, g, w_in, sink, w_out, Bn, S):
    T, D = x.shape
    QW = C_HEADS * HEAD_DIM
    KW = C_KV_HEADS * HEAD_DIM
    order = _c_head_order()
    wq = w_in[:, :QW].reshape(D, C_HEADS, HEAD_DIM)[:, order].reshape(D, QW)
    w_main = jnp.concatenate([wq, w_in[:, QW:]], axis=1).astype(BF16)
    w_out = w_out.reshape(C_HEADS, HEAD_DIM, D)[order].reshape(QW, D).astype(BF16)
    segs = [(QW, BF16, HEAD_DIM ** -0.5), (KW, BF16, 1.0), (KW, BF16, 1.0)]
    q, k, v = _norm_proj(x, g, w_main, segs)
    (o,) = _banded_attention(q.reshape(Bn, S, QW), k.reshape(Bn, S, KW), v.reshape(Bn, S, KW),
                             _alibi_slopes(C_HEADS), sink.astype(F32), dilation=1, groups=C_GROUP,
                             radius=C_RADIUS, with_lse=False, out_dtype=BF16)
    return _out_c(x, o.reshape(T, QW), w_out)


def kernel(x, norm_g, ffn_w1, ffn_w3, ffn_w2, ab_w_in, ab_conv_w, ab_conv_b, ab_gate_b, ab_hnorm_g, ab_w_out,
           c_w_in, c_sink, c_w_out, final_g):
    Bn, S, D = x.shape
    depth = norm_g.shape[0]
    w1, w3, w2 = (w.astype(BF16) for w in (ffn_w1, ffn_w3, ffn_w2))
    x = x.reshape(Bn * S, D)
    for l in range(depth):
        j = l // 2
        x = _ffn(x, norm_g[l, 0], w1[l, 0], w3[l, 0], w2[l, 0])
        if l % 2 == 0:
            x = _mixer_ab(x, norm_g[l, 1], ab_w_in[j], ab_conv_w[j], ab_conv_b[j], ab_gate_b[j], ab_hnorm_g[j],
                          ab_w_out[j], Bn, S)
        else:
            x = _mixer_c(x, norm_g[l, 1], c_w_in[j], c_sink[j], c_w_out[j], Bn, S)
        x = _ffn(x, norm_g[l, 2], w1[l, 1], w3[l, 1], w2[l, 1], final_g if l == depth - 1 else None)
    return x.reshape(Bn, S, D)
```

```python
import functools

import jax
import jax.numpy as jnp
from jax import lax
from jax.experimental import pallas as pl
from jax.experimental.pallas import tpu as pltpu

F32 = jnp.float32
BF16 = jnp.bfloat16

EPS = 1e-6
M_HEADS = 4
M_HEAD_DIM = 128
M_WIDTH = M_HEADS * M_HEAD_DIM
M_CONV = 5
B_HEADS = 8
B_WIDTH = 512
DILATIONS = (1, 4, 16)
B_RADIUS = 64
C_HEADS = 16
C_KV_HEADS = 4
C_GROUP = C_HEADS // C_KV_HEADS
C_RADIUS = 128
HEAD_DIM = 64
LANES = 128
SUBLANES = 8

VMEM_LIMIT_BYTES = 56 * 1024 * 1024
ROW_TILE = 512
ATTN_Q_CHAINS = 16
ATTN_Q_BLOCK = 128
MLSTM_CHUNK = 128
MLSTM_HEADS_PER_STEP = 2


def _cparams(*sem):
    return pltpu.CompilerParams(dimension_semantics=sem, vmem_limit_bytes=VMEM_LIMIT_BYTES)


def _resident(shape, index_map):
    return pl.BlockSpec(shape, index_map, pipeline_mode=pl.Buffered(1))


def _rms(x, g):
    return x * lax.rsqrt(jnp.mean(x * x, axis=-1, keepdims=True) + EPS) * g


def _ffn_kernel(x_ref, g_ref, w1_ref, w3_ref, w2_ref, *rest, final_norm):
    o_ref = rest[-1]
    x = x_ref[...]
    h = _rms(x, g_ref[...]).astype(BF16)
    a = jnp.dot(h, w1_ref[...], preferred_element_type=F32)
    b = jnp.dot(h, w3_ref[...], preferred_element_type=F32)
    act = (a * jax.nn.sigmoid(a) * b).astype(BF16)
    y = x + 0.5 * jnp.dot(act, w2_ref[...], preferred_element_type=F32)
    if final_norm:
        y = _rms(y, rest[0][...])
    o_ref[...] = y


def _ffn(x, g, w1, w3, w2, final_g=None):
    T, D = x.shape
    F = w1.shape[1]
    tm = ROW_TILE
    row = pl.BlockSpec((tm, D), lambda i: (i, 0))
    vec = _resident((1, D), lambda i: (0, 0))
    in_specs = [row, vec, _resident((D, F), lambda i: (0, 0)), _resident((D, F), lambda i: (0, 0)),
                _resident((F, D), lambda i: (0, 0))]
    args = [x, g.reshape(1, D), w1, w3, w2]
    if final_g is not None:
        in_specs.append(vec)
        args.append(final_g.reshape(1, D))
    return pl.pallas_call(
        functools.partial(_ffn_kernel, final_norm=final_g is not None),
        grid=(T // tm,), in_specs=in_specs, out_specs=row,
        out_shape=jax.ShapeDtypeStruct((T, D), F32),
        compiler_params=_cparams("parallel"), name="ffn")(*args)


def _proj_kernel(x_ref, g_ref, w_ref, *outs, segs):
    h = _rms(x_ref[...], g_ref[...]).astype(BF16)
    off = 0
    for (width, scale), o_ref in zip(segs, outs):
        y = jnp.dot(h, w_ref[:, off:off + width], preferred_element_type=F32)
        if scale != 1.0:
            y = y * scale
        o_ref[...] = y.astype(o_ref.dtype)
        off += width


def _norm_proj(x, g, w, segs):
    T, D = x.shape
    N = w.shape[1]
    tm = ROW_TILE
    return pl.pallas_call(
        functools.partial(_proj_kernel, segs=tuple((wd, sc) for wd, _, sc in segs)),
        grid=(T // tm,),
        in_specs=[pl.BlockSpec((tm, D), lambda i: (i, 0)), _resident((1, D), lambda i: (0, 0)),
                  _resident((D, N), lambda i: (0, 0))],
        out_specs=[pl.BlockSpec((tm, wd), lambda i: (i, 0)) for wd, _, _ in segs],
        out_shape=[jax.ShapeDtypeStruct((T, wd), dt) for wd, dt, _ in segs],
        compiler_params=_cparams("parallel"), name="norm_proj")(x, g.reshape(1, D), w)


def _proj_ab_kernel(x_ref, xp_ref, xn_ref, g_ref, wqk_ref, wr_ref, wg_ref, cw_ref, cb_ref,
                    qa_ref, ka_ref, vm_ref, om_ref, gr_ref, *rest, tiles_per_seq):
    tok_refs = rest[0:3]
    res_refs = {d: rest[3 * (n + 1):3 * (n + 2)] for n, d in enumerate(DILATIONS[1:])}
    stage_ref = rest[-1]
    tm = x_ref.shape[0]
    halo = SUBLANES
    g = g_ref[...]
    i = pl.program_id(0)
    first = i % tiles_per_seq == 0
    last = i % tiles_per_seq == tiles_per_seq - 1

    x = x_ref[...]
    h = _rms(x, g).astype(BF16)
    x_ext = jnp.concatenate([xp_ref[...], x, xn_ref[...]], axis=0)
    h_ext = _rms(x_ext, g).astype(BF16)
    n_ext = tm + 2 * halo
    qk = jnp.dot(h_ext, wqk_ref[...], preferred_element_type=F32)
    row = lax.broadcasted_iota(jnp.int32, (n_ext, 1), 0)
    outside = ((row < halo) & first) | ((row >= halo + tm) & last)
    qk = jnp.where(outside, 0.0, qk)
    acc = cb_ref[...] + qk[halo:halo + tm] * cw_ref[M_CONV // 2:M_CONV // 2 + 1, :]
    for j in range(M_CONV):
        off = j - M_CONV // 2
        if off != 0:
            shifted = pltpu.roll(qk, (-off) % n_ext, axis=0)
            acc = acc + shifted[halo:halo + tm] * cw_ref[j:j + 1, :]
    act = acc * jax.nn.sigmoid(acc)
    qa_ref[...] = act[:, :M_WIDTH].astype(BF16)
    ka_ref[...] = (act[:, M_WIDTH:] * (M_HEAD_DIM ** -0.5)).astype(BF16)

    vm_ref[...] = jnp.dot(h, wr_ref[:, 0:M_WIDTH], preferred_element_type=F32).astype(BF16)
    om_ref[...] = jnp.dot(h, wr_ref[:, M_WIDTH:2 * M_WIDTH], preferred_element_type=F32)
    gr_ref[...] = lax.dot_general(wg_ref[...], h, (((1,), (1,)), ((), ())), preferred_element_type=F32)

    n_col = B_WIDTH // LANES
    for s, tok_ref in enumerate(tok_refs):
        off = 2 * M_WIDTH + s * B_WIDTH
        y = jnp.dot(h, wr_ref[:, off:off + B_WIDTH], preferred_element_type=F32)
        if s == 0:
            y = y * (HEAD_DIM ** -0.5)
        tok_ref[...] = y.astype(BF16)
        for c in range(n_col):
            stage_ref[c] = y[:, c * LANES:(c + 1) * LANES]
        for d, refs in res_refs.items():
            for r in range(d):
                for c in range(n_col):
                    lo = r * B_WIDTH + c * LANES
                    refs[s][:, lo:lo + LANES] = stage_ref[c, pl.ds(r, tm // d, stride=d), :].astype(BF16)


def _proj_ab(x, g, w_qk, w_rest, wg_t, conv_w, conv_b, seq_len):
    T, D = x.shape
    tm = ROW_TILE
    halo = SUBLANES
    hb = tm // halo
    row = lambda w: pl.BlockSpec((tm, w), lambda i: (i, 0))
    res = lambda d: pl.BlockSpec((tm // d, d * B_WIDTH), lambda i: (i, 0))
    full = lambda a: _resident(a.shape, lambda i: (0, 0))
    conv_b = conv_b.reshape(1, -1)
    g = g.reshape(1, D)
    out_specs = [row(M_WIDTH)] * 4 + [pl.BlockSpec((wg_t.shape[0], tm), lambda i: (0, i))] + [row(B_WIDTH)] * 3
    out_shape = ([jax.ShapeDtypeStruct((T, M_WIDTH), dt) for dt in (BF16, BF16, BF16, F32)]
                 + [jax.ShapeDtypeStruct((wg_t.shape[0], T), F32)]
                 + [jax.ShapeDtypeStruct((T, B_WIDTH), BF16)] * 3)
    for d in DILATIONS[1:]:
        out_specs += [res(d)] * 3
        out_shape += [jax.ShapeDtypeStruct((T // d, d * B_WIDTH), BF16)] * 3
    return pl.pallas_call(
        functools.partial(_proj_ab_kernel, tiles_per_seq=seq_len // tm),
        grid=(T // tm,),
        in_specs=[row(D),
                  pl.BlockSpec((halo, D), lambda i: (jnp.maximum(i * hb - 1, 0), 0)),
                  pl.BlockSpec((halo, D), lambda i: (jnp.minimum((i + 1) * hb, T // halo - 1), 0)),
                  full(g), full(w_qk), full(w_rest), full(wg_t), full(conv_w), full(conv_b)],
        out_specs=out_specs, out_shape=out_shape,
        scratch_shapes=[pltpu.VMEM((B_WIDTH // LANES, tm, LANES), F32)],
        compiler_params=_cparams("parallel"), name="proj_ab")(x, x, x, g, w_qk, w_rest, wg_t, conv_w, conv_b)


def _out_c_kernel(x_ref, a_ref, w_ref, o_ref):
    o_ref[...] = x_ref[...] + jnp.dot(a_ref[...], w_ref[...], preferred_element_type=F32)


def _out_c(x, att, w):
    T, D = x.shape
    K = att.shape[1]
    tm = ROW_TILE
    row = pl.BlockSpec((tm, D), lambda i: (i, 0))
    return pl.pallas_call(
        _out_c_kernel, grid=(T // tm,),
        in_specs=[row, pl.BlockSpec((tm, K), lambda i: (i, 0)), _resident((K, D), lambda i: (0, 0))],
        out_specs=row, out_shape=jax.ShapeDtypeStruct((T, D), F32),
        compiler_params=_cparams("parallel"), name="out_proj_c")(x, att, w)


def _out_ab_kernel(x_ref, hn_ref, om_ref, o1_ref, l1_ref, o4_ref, l4_ref, o16_ref, l16_ref, wa_ref, wb_ref, o_ref,
                   so4_ref, sl4_ref, so16_ref, sl16_ref):
    tm = x_ref.shape[0]
    n_col = B_WIDTH // LANES
    for d, pairs in ((DILATIONS[1], ((o4_ref, so4_ref), (l4_ref, sl4_ref))),
                     (DILATIONS[2], ((o16_ref, so16_ref), (l16_ref, sl16_ref)))):
        for src_ref, dst_ref in pairs:
            for r in range(d):
                for c in range(n_col):
                    lo = r * B_WIDTH + c * LANES
                    dst_ref[c, pl.ds(r, tm // d, stride=d), :] = src_ref[:, lo:lo + LANES]
    cols = []
    for c in range(n_col):
        lanes = slice(c * LANES, (c + 1) * LANES)
        l1, l2, l3 = l1_ref[:, lanes], sl4_ref[c], sl16_ref[c]
        lm = jnp.maximum(jnp.maximum(l1, l2), l3)
        e1, e2, e3 = jnp.exp(l1 - lm), jnp.exp(l2 - lm), jnp.exp(l3 - lm)
        ob = (e1 * o1_ref[:, lanes] + e2 * so4_ref[c] + e3 * so16_ref[c]) / (e1 + e2 + e3)
        cols.append(ob.astype(BF16))
    ob = jnp.concatenate(cols, axis=1)
    ma = (hn_ref[...] * jax.nn.sigmoid(om_ref[...])).astype(BF16)
    o_ref[...] = (x_ref[...] + jnp.dot(ma, wa_ref[...], preferred_element_type=F32)
                  + jnp.dot(ob, wb_ref[...], preferred_element_type=F32))


def _out_ab(x, hn, om, outs, lses, wa, wb):
    T, D = x.shape
    tm = ROW_TILE
    row = pl.BlockSpec((tm, D), lambda i: (i, 0))
    half = pl.BlockSpec((tm, M_WIDTH), lambda i: (i, 0))
    res = lambda d: pl.BlockSpec((tm // d, d * B_WIDTH), lambda i: (i, 0))
    wspec = _resident((M_WIDTH, D), lambda i: (0, 0))
    d4, d16 = DILATIONS[1:]
    stage = pltpu.VMEM((B_WIDTH // LANES, tm, LANES), F32)
    return pl.pallas_call(
        _out_ab_kernel, grid=(T // tm,),
        in_specs=[row, half, half, half, half, res(d4), res(d4), res(d16), res(d16), wspec, wspec],
        out_specs=row, out_shape=jax.ShapeDtypeStruct((T, D), F32),
        scratch_shapes=[stage] * 4,
        compiler_params=_cparams("parallel"), name="out_proj_ab")(
            x, hn, om, outs[0], lses[0], outs[1], lses[1], outs[2], lses[2], wa, wb)


def _attn_kernel(*refs, groups, q_tile, q_block, k_window, radius, dist_unit, seq_len, has_sink, with_lse):
    slope_ref = refs[0]
    pos = 1
    sink_ref = None
    if has_sink:
        sink_ref = refs[pos]
        pos += 1
    q_ref, k_ref, v_ref, o_ref = refs[pos:pos + 4]
    pos += 4
    lse_ref = None
    if with_lse:
        lse_ref = refs[pos]
        pos += 1
    bias_ref = refs[pos]
    pair = pl.program_id(0)
    tile = pl.program_id(3)

    @pl.when((pl.program_id(1) == 0) & (pl.program_id(2) == 0) & (tile == 0))
    def _():
        sub = lax.broadcasted_iota(jnp.int32, (q_block, k_window), 0)
        ln = lax.broadcasted_iota(jnp.int32, (q_block, k_window), 1)
        for variant in range(3):
            adist = jnp.abs(ln - sub - variant * radius)
            negd = jnp.where(adist <= radius, -(dist_unit * adist).astype(F32), -jnp.inf)
            for g in range(groups):
                for j in range(2):
                    bias_ref[variant, 2 * g + j] = slope_ref[(2 * pair + j) * groups + g] * negd

    lane = lax.broadcasted_iota(jnp.int32, (q_block, LANES), 1)
    low = lane < HEAD_DIM
    ones = jnp.ones((k_window, LANES), BF16)
    for blk in range(q_tile // q_block):
        q0 = tile * q_tile + blk * q_block
        ks = pl.multiple_of(jnp.clip(q0 - radius, 0, seq_len - k_window), HEAD_DIM)
        variant = (q0 - ks) // radius
        rows = slice(blk * q_block, (blk + 1) * q_block)
        kt = k_ref[pl.ds(ks, k_window), :]
        vt = jnp.concatenate([v_ref[pl.ds(ks, k_window), :], ones], axis=1)
        for g in range(groups):
            qt = q_ref[rows, g * LANES:(g + 1) * LANES]
            res = []
            for j in range(2):
                head = (2 * pair + j) * groups + g
                qm = jnp.where(low if j == 0 else jnp.logical_not(low), qt, jnp.zeros_like(qt))
                s = lax.dot_general(qm, kt, (((1,), (1,)), ((), ())), preferred_element_type=F32)
                s = s + bias_ref[variant, 2 * g + j]
                m = jnp.max(s, axis=-1, keepdims=True)
                if has_sink:
                    m = jnp.maximum(m, sink_ref[head])
                p = jnp.exp(s - m).astype(BF16)
                od = jnp.dot(p, vt, preferred_element_type=F32)
                den = od[:, LANES:]
                if has_sink:
                    den = den + jnp.exp(sink_ref[head] - m)
                res.append((od[:, :LANES] / den, m + jnp.log(den)))
            o_ref[rows, g * LANES:(g + 1) * LANES] = jnp.where(low, res[0][0], res[1][0]).astype(o_ref.dtype)
            if with_lse:
                lse_ref[rows, g * LANES:(g + 1) * LANES] = jnp.where(low, res[0][1], res[1][1])


def _banded_attention(qv, kv, vv, slopes, sink, *, dilation, groups, radius, with_lse, out_dtype):
    Bn, Lv, _ = qv.shape
    d = dilation
    pairs = kv.shape[2] // (d * LANES)
    q_block = ATTN_Q_BLOCK
    q_tile = min(ATTN_Q_CHAINS // (2 * groups) * q_block, Lv)
    k_window = q_block + 2 * radius
    assert radius <= q_block and q_block % radius == 0 and Lv % q_tile == 0 and k_window <= Lv
    qw = groups * LANES
    smem = pl.BlockSpec(memory_space=pltpu.SMEM)
    qspec = pl.BlockSpec((None, q_tile, qw), lambda p, b, r, i: (b, i, r * pairs + p))
    kspec = pl.BlockSpec((None, Lv, LANES), lambda p, b, r, i: (b, 0, r * pairs + p))
    in_specs = [smem] + ([smem] if sink is not None else []) + [qspec, kspec, kspec]
    args = [slopes] + ([sink] if sink is not None else []) + [qv, kv, vv]
    out_specs = [qspec]
    out_shape = [jax.ShapeDtypeStruct(qv.shape, out_dtype)]
    if with_lse:
        out_specs.append(qspec)
        out_shape.append(jax.ShapeDtypeStruct(qv.shape, F32))
    return pl.pallas_call(
        functools.partial(_attn_kernel, groups=groups, q_tile=q_tile, q_block=q_block, k_window=k_window,
                          radius=radius, dist_unit=d, seq_len=Lv, has_sink=sink is not None, with_lse=with_lse),
        grid=(pairs, Bn, d, Lv // q_tile), in_specs=in_specs, out_specs=out_specs, out_shape=out_shape,
        scratch_shapes=[pltpu.VMEM((3, 2 * groups, q_block, k_window), F32)],
        compiler_params=_cparams("arbitrary", "arbitrary", "arbitrary", "arbitrary"),
        name=f"banded_attention_d{d}")(*args)


def _alibi_slopes(n):
    return jnp.exp2(-8.0 * jnp.arange(1, n + 1, dtype=F32) / n)


def _log_sigmoid(x):
    return jnp.minimum(x, 0.0) - jnp.log1p(jnp.exp(-jnp.abs(x)))


def _mlstm_chunk(q, k, v1, i_r, lf_r, state, seen, diag):
    ct, m = state
    E = q.shape[1]
    b_c = jnp.sum(jnp.where(seen, lf_r, 0.0), axis=1, keepdims=True)
    b_r = jnp.sum(jnp.where(diag, b_c, 0.0), axis=0, keepdims=True)
    b_last = jnp.sum(lf_r, axis=1, keepdims=True)
    u_r = i_r - b_r
    um = jnp.where(seen, u_r, -jnp.inf)
    g_t = jnp.maximum(m, jnp.max(um, axis=1, keepdims=True))
    qk = lax.dot_general(q, k, (((1,), (1,)), ((), ())), preferred_element_type=F32)
    w = jnp.exp(um - g_t) * qk
    tot = (jnp.dot(w.astype(BF16), v1, preferred_element_type=F32)
           + jnp.exp(m - g_t) * jnp.dot(q, ct.astype(BF16), preferred_element_type=F32))
    h = tot[:, :E] / jnp.maximum(jnp.abs(tot[:, E:]), jnp.exp(-(b_c + g_t)))

    a_r = b_last + u_r
    m_new = jnp.maximum(b_last + m, jnp.max(a_r, axis=1, keepdims=True))
    kw_t = (k.astype(F32).T * jnp.exp(a_r - m_new)).astype(BF16)
    ct = jnp.exp(b_last + m - m_new) * ct + jnp.dot(kw_t, v1, preferred_element_type=F32)
    return h, (ct, m_new)


def _mlstm_kernel(q_ref, k_ref, v_ref, gr_ref, gb_ref, hg_ref, o_ref, *, seq_len, chunk, heads):
    S, L, E = seq_len, chunk, M_HEAD_DIM
    n_chunks = S // L
    row = lax.broadcasted_iota(jnp.int32, (L, L), 0)
    col = lax.broadcasted_iota(jnp.int32, (L, L), 1)
    seen = (col <= row, col >= row)
    diag = col == row
    ones = jnp.ones((L, E), BF16)

    def run_chunk(c, hh, direction, state):
        rows = pl.ds(pl.multiple_of(c * L, L), L)
        lanes = slice(hh * E, (hh + 1) * E)
        g = gr_ref[4 * hh:4 * hh + 4, rows] + gb_ref[4 * hh:4 * hh + 4, :]
        v1 = jnp.concatenate([v_ref[rows, lanes], ones], axis=1)
        h, state = _mlstm_chunk(q_ref[rows, lanes], k_ref[rows, lanes], v1, g[direction:direction + 1],
                                _log_sigmoid(g[2 + direction:3 + direction]), state, seen[direction], diag)
        return rows, lanes, h, state

    def finish(tot, lanes):
        return tot * lax.rsqrt(jnp.mean(tot * tot, axis=-1, keepdims=True) + EPS) * hg_ref[:, lanes]

    def sweep(second):
        def body(it, states):
            new = []
            for n, state in enumerate(states):
                hh, direction = divmod(n, 2)
                c = it if direction == 0 else n_chunks - 1 - it
                rows, lanes, h, state = run_chunk(c, hh, direction, state)
                o_ref[rows, lanes] = finish(o_ref[rows, lanes] + h, lanes) if second else h
                new.append(state)
            return tuple(new)
        return body

    zero = (jnp.zeros((E, 2 * E), F32), jnp.zeros((1, 1), F32))
    states = lax.fori_loop(0, n_chunks // 2, sweep(False), (zero,) * (2 * heads))
    lax.fori_loop(n_chunks // 2, n_chunks, sweep(True), states)


def _mlstm(q, k, v, g_rows, gate_b, hnorm_g, Bn, S):
    E = M_HEAD_DIM
    T = Bn * S
    hs = MLSTM_HEADS_PER_STEP
    groups = M_HEADS // hs
    seq = pl.BlockSpec((None, S, hs * E), lambda b, h: (b, 0, h))
    q3, k3, v3 = (t.reshape(Bn, S, M_WIDTH) for t in (q, k, v))
    out = pl.pallas_call(
        functools.partial(_mlstm_kernel, seq_len=S, chunk=MLSTM_CHUNK, heads=hs),
        grid=(Bn, groups),
        in_specs=[seq, seq, seq,
                  pl.BlockSpec((None, 4 * hs, S), lambda b, h: (h, 0, b)),
                  pl.BlockSpec((None, 4 * hs, 1), lambda b, h: (h, 0, 0)),
                  pl.BlockSpec((1, hs * E), lambda b, h: (0, h))],
        out_specs=seq,
        out_shape=jax.ShapeDtypeStruct((Bn, S, M_WIDTH), F32),
        compiler_params=_cparams("parallel", "parallel"), name="mlstm")(
            q3, k3, v3, g_rows.reshape(groups, 4 * hs, T), gate_b.reshape(groups, 4 * hs, 1),
            hnorm_g.reshape(1, M_WIDTH))
    return out.reshape(T, M_WIDTH)


def _mixer_ab(x, g, w_in, conv_w, conv_b, gate_b, hnorm_g, w_out, Bn, S):
    T = Bn * S
    MW, BW = M_WIDTH, B_WIDTH
    g0 = 4 * MW
    n_gate = 4 * M_HEADS
    w_qk = w_in[:, :2 * MW].astype(BF16)
    w_rest = jnp.concatenate([w_in[:, 2 * MW:g0], w_in[:, g0 + n_gate:]], axis=1).astype(BF16)
    wg_t = w_in[:, g0:g0 + n_gate].reshape(-1, 4, M_HEADS).transpose(2, 1, 0).reshape(n_gate, -1).astype(BF16)
    gate_b = gate_b.reshape(4, M_HEADS).T.astype(F32)
    (q_m, k_m, v_m, o_m, g_rows, q1, k1, v1, q4, k4, v4, q16, k16, v16) = _proj_ab(
        x, g, w_qk, w_rest, wg_t, conv_w, conv_b, S)

    hn = _mlstm(q_m, k_m, v_m, g_rows, gate_b, hnorm_g, Bn, S)

    slopes = _alibi_slopes(B_HEADS)
    outs, lses = [], []
    for d, qkv in zip(DILATIONS, ((q1, k1, v1), (q4, k4, v4), (q16, k16, v16))):
        qv, kv, vv = (t.reshape(Bn, S // d, d * BW) for t in qkv)
        o, lse = _banded_attention(qv, kv, vv, slopes, None, dilation=d, groups=1, radius=B_RADIUS,
                                   with_lse=True, out_dtype=F32)
        outs.append(o.reshape(T // d, d * BW))
        lses.append(lse.reshape(T // d, d * BW))
    w_out = w_out.astype(BF16)
    return _out_ab(x, hn, o_m, outs, lses, w_out[:MW], w_out[MW:])


def _c_head_order():
    order = []
    for pair in range(C_KV_HEADS // 2):
        for slot in range(C_GROUP):
            for j in range(2):
                order.append((2 * pair + j) * C_GROUP + slot)
    return jnp.array(order, dtype=jnp.int32)


def _mixer_c(x, g, w_in, sink, w_out, Bn, S):
    T, D = x.shape
    QW = C_HEADS * HEAD_DIM
    KW = C_KV_HEADS * HEAD_DIM
    order = _c_head_order()
    wq = w_in[:, :QW].reshape(D, C_HEADS, HEAD_DIM)[:, order].reshape(D, QW)
    w_main = jnp.concatenate([wq, w_in[:, QW:]], axis=1).astype(BF16)
    w_out = w_out.reshape(C_HEADS, HEAD_DIM, D)[order].reshape(QW, D).astype(BF16)
    segs = [(QW, BF16, HEAD_DIM ** -0.5), (KW, BF16, 1.0), (KW, BF16, 1.0)]
    q, k, v = _norm_proj(x, g, w_main, segs)
    (o,) = _banded_attention(q.reshape(Bn, S, QW), k.reshape(Bn, S, KW), v.reshape(Bn, S, KW),
                             _alibi_slopes(C_HEADS), sink.astype(F32), dilation=1, groups=C_GROUP,
                             radius=C_RADIUS, with_lse=False, out_dtype=BF16)
    return _out_c(x, o.reshape(T, QW), w_out)


def kernel(x, norm_g, ffn_w1, ffn_w3, ffn_w2, ab_w_in, ab_conv_w, ab_conv_b, ab_gate_b, ab_hnorm_g, ab_w_out,
           c_w_in, c_sink, c_w_out, final_g):
    Bn, S, D = x.shape
    depth = norm_g.shape[0]
    w1, w3, w2 = (w.astype(BF16) for w in (ffn_w1, ffn_w3, ffn_w2))
    x = x.reshape(Bn * S, D)
    for l in range(depth):
        j = l // 2
        x = _ffn(x, norm_g[l, 0], w1[l, 0], w3[l, 0], w2[l, 0])
        if l % 2 == 0:
            x = _mixer_ab(x, norm_g[l, 1], ab_w_in[j], ab_conv_w[j], ab_conv_b[j], ab_gate_b[j], ab_hnorm_g[j],
                          ab_w_out[j], Bn, S)
        else:
            x = _mixer_c(x, norm_g[l, 1], c_w_in[j], c_sink[j], c_w_out[j], Bn, S)
        x = _ffn(x, norm_g[l, 2], w1[l, 1], w3[l, 1], w2[l, 1], final_g if l == depth - 1 else None)
    return x.reshape(Bn, S, D)
```

```python
import functools

import jax
import jax.numpy as jnp
from jax import lax
from jax.experimental import pallas as pl
from jax.experimental.pallas import tpu as pltpu

F32 = jnp.float32
BF16 = jnp.bfloat16

EPS = 1e-6
M_HEADS = 4
M_HEAD_DIM = 128
M_WIDTH = M_HEADS * M_HEAD_DIM
M_CONV = 5
B_HEADS = 8
B_WIDTH = 512
DILATIONS = (1, 4, 16)
B_RADIUS = 64
C_HEADS = 16
C_KV_HEADS = 4
C_GROUP = C_HEADS // C_KV_HEADS
C_RADIUS = 128
HEAD_DIM = 64
LANES = 128
SUBLANES = 8

VMEM_LIMIT_BYTES = 56 * 1024 * 1024
ROW_TILE = 512
ATTN_Q_CHAINS = 16
ATTN_Q_BLOCK = 128
MLSTM_CHUNK = 256
CONV_COLS = 256
MLSTM_HEADS_PER_STEP = 2


def _cparams(*sem):
    return pltpu.CompilerParams(dimension_semantics=sem, vmem_limit_bytes=VMEM_LIMIT_BYTES)


def _resident(shape, index_map):
    return pl.BlockSpec(shape, index_map, pipeline_mode=pl.Buffered(1))


def _rms(x, g):
    return x * lax.rsqrt(jnp.mean(x * x, axis=-1, keepdims=True) + EPS) * g


def _ffn_kernel(x_ref, g_ref, w1_ref, w3_ref, w2_ref, *rest, final_norm):
    o_ref = rest[-1]
    x = x_ref[...]
    h = _rms(x, g_ref[...]).astype(BF16)
    a = jnp.dot(h, w1_ref[...], preferred_element_type=F32)
    b = jnp.dot(h, w3_ref[...], preferred_element_type=F32)
    act = (a * jax.nn.sigmoid(a) * b).astype(BF16)
    y = x + 0.5 * jnp.dot(act, w2_ref[...], preferred_element_type=F32)
    if final_norm:
        y = _rms(y, rest[0][...])
    o_ref[...] = y


def _ffn(x, g, w1, w3, w2, final_g=None):
    T, D = x.shape
    F = w1.shape[1]
    tm = ROW_TILE
    row = pl.BlockSpec((tm, D), lambda i: (i, 0))
    vec = _resident((1, D), lambda i: (0, 0))
    in_specs = [row, vec, _resident((D, F), lambda i: (0, 0)), _resident((D, F), lambda i: (0, 0)),
                _resident((F, D), lambda i: (0, 0))]
    args = [x, g.reshape(1, D), w1, w3, w2]
    if final_g is not None:
        in_specs.append(vec)
        args.append(final_g.reshape(1, D))
    return pl.pallas_call(
        functools.partial(_ffn_kernel, final_norm=final_g is not None),
        grid=(T // tm,), in_specs=in_specs, out_specs=row,
        out_shape=jax.ShapeDtypeStruct((T, D), F32),
        compiler_params=_cparams("parallel"), name="ffn")(*args)


def _proj_kernel(x_ref, g_ref, w_ref, *outs, segs):
    h = _rms(x_ref[...], g_ref[...]).astype(BF16)
    off = 0
    for (width, scale), o_ref in zip(segs, outs):
        y = jnp.dot(h, w_ref[:, off:off + width], preferred_element_type=F32)
        if scale != 1.0:
            y = y * scale
        o_ref[...] = y.astype(o_ref.dtype)
        off += width


def _norm_proj(x, g, w, segs):
    T, D = x.shape
    N = w.shape[1]
    tm = ROW_TILE
    return pl.pallas_call(
        functools.partial(_proj_kernel, segs=tuple((wd, sc) for wd, _, sc in segs)),
        grid=(T // tm,),
        in_specs=[pl.BlockSpec((tm, D), lambda i: (i, 0)), _resident((1, D), lambda i: (0, 0)),
                  _resident((D, N), lambda i: (0, 0))],
        out_specs=[pl.BlockSpec((tm, wd), lambda i: (i, 0)) for wd, _, _ in segs],
        out_shape=[jax.ShapeDtypeStruct((T, wd), dt) for wd, dt, _ in segs],
        compiler_params=_cparams("parallel"), name="norm_proj")(x, g.reshape(1, D), w)


def _proj_ab_kernel(x_ref, xp_ref, xn_ref, g_ref, wqk_ref, wr_ref, wg_ref, cw_ref, cb_ref,
                    qa_ref, ka_ref, vm_ref, om_ref, gr_ref, *rest, tiles_per_seq):
    tok_refs = rest[0:3]
    res_refs = {d: rest[3 * (n + 1):3 * (n + 2)] for n, d in enumerate(DILATIONS[1:])}
    stage_ref = rest[-1]
    tm = x_ref.shape[0]
    halo = SUBLANES
    g = g_ref[...]
    i = pl.program_id(0)
    first = i % tiles_per_seq == 0
    last = i % tiles_per_seq == tiles_per_seq - 1

    x = x_ref[...]
    h = _rms(x, g).astype(BF16)
    x_ext = jnp.concatenate([xp_ref[...], x, xn_ref[...]], axis=0)
    h_ext = _rms(x_ext, g).astype(BF16)
    n_ext = tm + 2 * halo
    row = lax.broadcasted_iota(jnp.int32, (n_ext, 1), 0)
    outside = ((row < halo) & first) | ((row >= halo + tm) & last)
    n_conv = 2 * M_WIDTH // CONV_COLS
    qk_blocks = [jnp.dot(h_ext, wqk_ref[:, blk * CONV_COLS:(blk + 1) * CONV_COLS], preferred_element_type=F32)
                 for blk in range(n_conv)]
    y_vm = jnp.dot(h, wr_ref[:, 0:M_WIDTH], preferred_element_type=F32)
    y_om = jnp.dot(h, wr_ref[:, M_WIDTH:2 * M_WIDTH], preferred_element_type=F32)
    y_g = lax.dot_general(wg_ref[...], h, (((1,), (1,)), ((), ())), preferred_element_type=F32)
    ys = [jnp.dot(h, wr_ref[:, 2 * M_WIDTH + s * B_WIDTH:2 * M_WIDTH + (s + 1) * B_WIDTH],
                  preferred_element_type=F32) for s in range(len(tok_refs))]

    for blk in range(n_conv):
        cols = slice(blk * CONV_COLS, (blk + 1) * CONV_COLS)
        qk = jnp.where(outside, 0.0, qk_blocks[blk])
        acc = cb_ref[:, cols] + qk[halo:halo + tm] * cw_ref[M_CONV // 2:M_CONV // 2 + 1, cols]
        for j in range(M_CONV):
            off = j - M_CONV // 2
            if off != 0:
                shifted = pltpu.roll(qk, (-off) % n_ext, axis=0)
                acc = acc + shifted[halo:halo + tm] * cw_ref[j:j + 1, cols]
        act = acc * jax.nn.sigmoid(acc)
        if blk * CONV_COLS < M_WIDTH:
            qa_ref[:, cols] = act.astype(BF16)
        else:
            kcols = slice(blk * CONV_COLS - M_WIDTH, (blk + 1) * CONV_COLS - M_WIDTH)
            ka_ref[:, kcols] = (act * (M_HEAD_DIM ** -0.5)).astype(BF16)

    vm_ref[...] = y_vm.astype(BF16)
    om_ref[...] = y_om
    gr_ref[...] = y_g

    n_col = B_WIDTH // LANES
    for s, tok_ref in enumerate(tok_refs):
        y = ys[s] * (HEAD_DIM ** -0.5) if s == 0 else ys[s]
        tok_ref[...] = y.astype(BF16)
        for c in range(n_col):
            stage_ref[s, c] = y[:, c * LANES:(c + 1) * LANES]
        for d, refs in res_refs.items():
            for r in range(d):
                for c in range(n_col):
                    lo = r * B_WIDTH + c * LANES
                    refs[s][:, lo:lo + LANES] = stage_ref[s, c, pl.ds(r, tm // d, stride=d), :].astype(BF16)


def _proj_ab(x, g, w_qk, w_rest, wg_t, conv_w, conv_b, seq_len):
    T, D = x.shape
    tm = ROW_TILE
    halo = SUBLANES
    hb = tm // halo
    row = lambda w: pl.BlockSpec((tm, w), lambda i: (i, 0))
    res = lambda d: pl.BlockSpec((tm // d, d * B_WIDTH), lambda i: (i, 0))
    full = lambda a: _resident(a.shape, lambda i: (0, 0))
    conv_b = conv_b.reshape(1, -1)
    g = g.reshape(1, D)
    out_specs = [row(M_WIDTH)] * 4 + [pl.BlockSpec((wg_t.shape[0], tm), lambda i: (0, i))] + [row(B_WIDTH)] * 3
    out_shape = ([jax.ShapeDtypeStruct((T, M_WIDTH), dt) for dt in (BF16, BF16, BF16, F32)]
                 + [jax.ShapeDtypeStruct((wg_t.shape[0], T), F32)]
                 + [jax.ShapeDtypeStruct((T, B_WIDTH), BF16)] * 3)
    for d in DILATIONS[1:]:
        out_specs += [res(d)] * 3
        out_shape += [jax.ShapeDtypeStruct((T // d, d * B_WIDTH), BF16)] * 3
    return pl.pallas_call(
        functools.partial(_proj_ab_kernel, tiles_per_seq=seq_len // tm),
        grid=(T // tm,),
        in_specs=[row(D),
                  pl.BlockSpec((halo, D), lambda i: (jnp.maximum(i * hb - 1, 0), 0)),
                  pl.BlockSpec((halo, D), lambda i: (jnp.minimum((i + 1) * hb, T // halo - 1), 0)),
                  full(g), full(w_qk), full(w_rest), full(wg_t), full(conv_w), full(conv_b)],
        out_specs=out_specs, out_shape=out_shape,
        scratch_shapes=[pltpu.VMEM((3, B_WIDTH // LANES, tm, LANES), F32)],
        compiler_params=_cparams("parallel"), name="proj_ab")(x, x, x, g, w_qk, w_rest, wg_t, conv_w, conv_b)


def _out_c_kernel(x_ref, a_ref, w_ref, o_ref):
    o_ref[...] = x_ref[...] + jnp.dot(a_ref[...], w_ref[...], preferred_element_type=F32)


def _out_c(x, att, w):
    T, D = x.shape
    K = att.shape[1]
    tm = ROW_TILE
    row = pl.BlockSpec((tm, D), lambda i: (i, 0))
    return pl.pallas_call(
        _out_c_kernel, grid=(T // tm,),
        in_specs=[row, pl.BlockSpec((tm, K), lambda i: (i, 0)), _resident((K, D), lambda i: (0, 0))],
        out_specs=row, out_shape=jax.ShapeDtypeStruct((T, D), F32),
        compiler_params=_cparams("parallel"), name="out_proj_c")(x, att, w)


def _out_ab_kernel(x_ref, hn_ref, om_ref, o1_ref, l1_ref, o4_ref, l4_ref, o16_ref, l16_ref, wa_ref, wb_ref, o_ref,
                   so4_ref, sl4_ref, so16_ref, sl16_ref):
    tm = x_ref.shape[0]
    n_col = B_WIDTH // LANES
    for d, pairs in ((DILATIONS[1], ((o4_ref, so4_ref), (l4_ref, sl4_ref))),
                     (DILATIONS[2], ((o16_ref, so16_ref), (l16_ref, sl16_ref)))):
        for src_ref, dst_ref in pairs:
            for r in range(d):
                for c in range(n_col):
                    lo = r * B_WIDTH + c * LANES
                    dst_ref[c, pl.ds(r, tm // d, stride=d), :] = src_ref[:, lo:lo + LANES]
    cols = []
    for c in range(n_col):
        lanes = slice(c * LANES, (c + 1) * LANES)
        l1, l2, l3 = l1_ref[:, lanes], sl4_ref[c], sl16_ref[c]
        lm = jnp.maximum(jnp.maximum(l1, l2), l3)
        e1, e2, e3 = jnp.exp(l1 - lm), jnp.exp(l2 - lm), jnp.exp(l3 - lm)
        ob = (e1 * o1_ref[:, lanes] + e2 * so4_ref[c] + e3 * so16_ref[c]) / (e1 + e2 + e3)
        cols.append(ob.astype(BF16))
    ob = jnp.concatenate(cols, axis=1)
    ma = (hn_ref[...] * jax.nn.sigmoid(om_ref[...])).astype(BF16)
    o_ref[...] = (x_ref[...] + jnp.dot(ma, wa_ref[...], preferred_element_type=F32)
                  + jnp.dot(ob, wb_ref[...], preferred_element_type=F32))


def _out_ab(x, hn, om, outs, lses, wa, wb):
    T, D = x.shape
    tm = ROW_TILE
    row = pl.BlockSpec((tm, D), lambda i: (i, 0))
    half = pl.BlockSpec((tm, M_WIDTH), lambda i: (i, 0))
    res = lambda d: pl.BlockSpec((tm // d, d * B_WIDTH), lambda i: (i, 0))
    wspec = _resident((M_WIDTH, D), lambda i: (0, 0))
    d4, d16 = DILATIONS[1:]
    stage = pltpu.VMEM((B_WIDTH // LANES, tm, LANES), F32)
    return pl.pallas_call(
        _out_ab_kernel, grid=(T // tm,),
        in_specs=[row, half, half, half, half, res(d4), res(d4), res(d16), res(d16), wspec, wspec],
        out_specs=row, out_shape=jax.ShapeDtypeStruct((T, D), F32),
        scratch_shapes=[stage] * 4,
        compiler_params=_cparams("parallel"), name="out_proj_ab")(
            x, hn, om, outs[0], lses[0], outs[1], lses[1], outs[2], lses[2], wa, wb)


def _attn_kernel(*refs, groups, q_tile, q_block, k_window, radius, dist_unit, seq_len, has_sink, with_lse):
    slope_ref = refs[0]
    pos = 1
    sink_ref = None
    if has_sink:
        sink_ref = refs[pos]
        pos += 1
    q_ref, k_ref, v_ref, o_ref = refs[pos:pos + 4]
    pos += 4
    lse_ref = None
    if with_lse:
        lse_ref = refs[pos]
        pos += 1
    bias_ref = refs[pos]
    pair = pl.program_id(0)
    tile = pl.program_id(3)

    @pl.when((pl.program_id(1) == 0) & (pl.program_id(2) == 0) & (tile == 0))
    def _():
        sub = lax.broadcasted_iota(jnp.int32, (q_block, k_window), 0)
        ln = lax.broadcasted_iota(jnp.int32, (q_block, k_window), 1)
        for variant in range(3):
            adist = jnp.abs(ln - sub - variant * radius)
            negd = jnp.where(adist <= radius, -(dist_unit * adist).astype(F32), -jnp.inf)
            for g in range(groups):
                for j in range(2):
                    bias_ref[variant, 2 * g + j] = slope_ref[(2 * pair + j) * groups + g] * negd

    lane = lax.broadcasted_iota(jnp.int32, (q_block, LANES), 1)
    low = lane < HEAD_DIM
    ones = jnp.ones((k_window, LANES), BF16)
    for blk in range(q_tile // q_block):
        q0 = tile * q_tile + blk * q_block
        ks = pl.multiple_of(jnp.clip(q0 - radius, 0, seq_len - k_window), HEAD_DIM)
        variant = (q0 - ks) // radius
        rows = slice(blk * q_block, (blk + 1) * q_block)
        kt = k_ref[pl.ds(ks, k_window), :]
        vt = jnp.concatenate([v_ref[pl.ds(ks, k_window), :], ones], axis=1)
        for g in range(groups):
            qt = q_ref[rows, g * LANES:(g + 1) * LANES]
            res = []
            for j in range(2):
                head = (2 * pair + j) * groups + g
                qm = jnp.where(low if j == 0 else jnp.logical_not(low), qt, jnp.zeros_like(qt))
                s = lax.dot_general(qm, kt, (((1,), (1,)), ((), ())), preferred_element_type=F32)
                s = s + bias_ref[variant, 2 * g + j]
                m = jnp.max(s, axis=-1, keepdims=True)
                if has_sink:
                    m = jnp.maximum(m, sink_ref[head])
                p = jnp.exp(s - m).astype(BF16)
                od = jnp.dot(p, vt, preferred_element_type=F32)
                den = od[:, LANES:]
                if has_sink:
                    den = den + jnp.exp(sink_ref[head] - m)
                res.append((od[:, :LANES] / den, m + jnp.log(den)))
            o_ref[rows, g * LANES:(g + 1) * LANES] = jnp.where(low, res[0][0], res[1][0]).astype(o_ref.dtype)
            if with_lse:
                lse_ref[rows, g * LANES:(g + 1) * LANES] = jnp.where(low, res[0][1], res[1][1])


def _banded_attention(qv, kv, vv, slopes, sink, *, dilation, groups, radius, with_lse, out_dtype):
    Bn, Lv, _ = qv.shape
    d = dilation
    pairs = kv.shape[2] // (d * LANES)
    q_block = ATTN_Q_BLOCK
    q_tile = min(ATTN_Q_CHAINS // (2 * groups) * q_block, Lv)
    k_window = q_block + 2 * radius
    assert radius <= q_block and q_block % radius == 0 and Lv % q_tile == 0 and k_window <= Lv
    qw = groups * LANES
    smem = pl.BlockSpec(memory_space=pltpu.SMEM)
    qspec = pl.BlockSpec((None, q_tile, qw), lambda p, b, r, i: (b, i, r * pairs + p))
    kspec = pl.BlockSpec((None, Lv, LANES), lambda p, b, r, i: (b, 0, r * pairs + p))
    in_specs = [smem] + ([smem] if sink is not None else []) + [qspec, kspec, kspec]
    args = [slopes] + ([sink] if sink is not None else []) + [qv, kv, vv]
    out_specs = [qspec]
    out_shape = [jax.ShapeDtypeStruct(qv.shape, out_dtype)]
    if with_lse:
        out_specs.append(qspec)
        out_shape.append(jax.ShapeDtypeStruct(qv.shape, F32))
    return pl.pallas_call(
        functools.partial(_attn_kernel, groups=groups, q_tile=q_tile, q_block=q_block, k_window=k_window,
                          radius=radius, dist_unit=d, seq_len=Lv, has_sink=sink is not None, with_lse=with_lse),
        grid=(pairs, Bn, d, Lv // q_tile), in_specs=in_specs, out_specs=out_specs, out_shape=out_shape,
        scratch_shapes=[pltpu.VMEM((3, 2 * groups, q_block, k_window), F32)],
        compiler_params=_cparams("arbitrary", "arbitrary", "arbitrary", "arbitrary"),
        name=f"banded_attention_d{d}")(*args)


def _alibi_slopes(n):
    return jnp.exp2(-8.0 * jnp.arange(1, n + 1, dtype=F32) / n)


def _log_sigmoid(x):
    return jnp.minimum(x, 0.0) - jnp.log1p(jnp.exp(-jnp.abs(x)))


def _mlstm_chunks(chains, seen, diag):
    nt = (((1,), (1,)), ((), ()))
    E = chains[0]["q"].shape[1]
    mask = [seen[c["rev"]] for c in chains]
    b_c = [jnp.sum(jnp.where(mk, c["lf_r"], 0.0), axis=1, keepdims=True) for c, mk in zip(chains, mask)]
    qk = [lax.dot_general(c["q"], c["k"], nt, preferred_element_type=F32) for c in chains]
    inter = [jnp.dot(c["q"], c["ct"].astype(BF16), preferred_element_type=F32) for c in chains]
    k_t = [c["k"].astype(F32).T for c in chains]
    b_last = [jnp.sum(c["lf_r"], axis=1, keepdims=True) for c in chains]
    b_r = [jnp.sum(jnp.where(diag, x, 0.0), axis=0, keepdims=True) for x in b_c]
    u_r = [c["i_r"] - x for c, x in zip(chains, b_r)]
    um = [jnp.where(mk, x, -jnp.inf) for mk, x in zip(mask, u_r)]
    g_t = [jnp.maximum(c["m"], jnp.max(x, axis=1, keepdims=True)) for c, x in zip(chains, um)]
    a_r = [x + y for x, y in zip(b_last, u_r)]
    m_new = [jnp.maximum(x + c["m"], jnp.max(y, axis=1, keepdims=True)) for c, x, y in zip(chains, b_last, a_r)]
    kw_t = [(x * jnp.exp(y - z)).astype(BF16) for x, y, z in zip(k_t, a_r, m_new)]
    ct = [jnp.exp(x + c["m"] - z) * c["ct"] + jnp.dot(y, c["v1"], preferred_element_type=F32)
          for c, x, y, z in zip(chains, b_last, kw_t, m_new)]
    w = [(jnp.exp(x - y) * z).astype(BF16) for x, y, z in zip(um, g_t, qk)]
    tot = [jnp.dot(x, c["v1"], preferred_element_type=F32) + jnp.exp(c["m"] - y) * z
           for c, x, y, z in zip(chains, w, g_t, inter)]
    h = [x[:, :E] / jnp.maximum(jnp.abs(x[:, E:]), jnp.exp(-(y + z))) for x, y, z in zip(tot, b_c, g_t)]
    return list(zip(h, ct, m_new))


def _mlstm_kernel(q_ref, k_ref, v_ref, gr_ref, gb_ref, hg_ref, o_ref, *, seq_len, chunk, heads):
    S, L, E = seq_len, chunk, M_HEAD_DIM
    n_chunks = S // L
    row = lax.broadcasted_iota(jnp.int32, (L, L), 0)
    col = lax.broadcasted_iota(jnp.int32, (L, L), 1)
    seen = (col <= row, col >= row)
    diag = col == row
    ones = jnp.ones((L, E), BF16)

    def finish(tot, lanes):
        return tot * lax.rsqrt(jnp.mean(tot * tot, axis=-1, keepdims=True) + EPS) * hg_ref[:, lanes]

    def sweep(second):
        def body(it, states):
            chains, where = [], []
            for n, (ct, m) in enumerate(states):
                hh, rev = divmod(n, 2)
                c = it if rev == 0 else n_chunks - 1 - it
                rows = pl.ds(pl.multiple_of(c * L, L), L)
                lanes = slice(hh * E, (hh + 1) * E)
                g = gr_ref[4 * hh:4 * hh + 4, rows] + gb_ref[4 * hh:4 * hh + 4, :]
                where.append((rows, lanes, o_ref[rows, lanes] if second else None))
                chains.append(dict(q=q_ref[rows, lanes], k=k_ref[rows, lanes],
                                   v1=jnp.concatenate([v_ref[rows, lanes], ones], axis=1),
                                   i_r=g[rev:rev + 1], lf_r=_log_sigmoid(g[2 + rev:3 + rev]), ct=ct, m=m, rev=rev))
            out = _mlstm_chunks(chains, seen, diag)
            for (rows, lanes, prev), (h, _, _) in zip(where, out):
                o_ref[rows, lanes] = finish(prev + h, lanes) if second else h
            return tuple((ct, m) for _, ct, m in out)
        return body

    zero = (jnp.zeros((E, 2 * E), F32), jnp.zeros((1, 1), F32))
    states = lax.fori_loop(0, n_chunks // 2, sweep(False), (zero,) * (2 * heads))
    lax.fori_loop(n_chunks // 2, n_chunks, sweep(True), states)


def _mlstm(q, k, v, g_rows, gate_b, hnorm_g, Bn, S):
    E = M_HEAD_DIM
    T = Bn * S
    hs = MLSTM_HEADS_PER_STEP
    groups = M_HEADS // hs
    seq = pl.BlockSpec((None, S, hs * E), lambda b, h: (b, 0, h))
    q3, k3, v3 = (t.reshape(Bn, S, M_WIDTH) for t in (q, k, v))
    out = pl.pallas_call(
        functools.partial(_mlstm_kernel, seq_len=S, chunk=MLSTM_CHUNK, heads=hs),
        grid=(Bn, groups),
        in_specs=[seq, seq, seq,
                  pl.BlockSpec((None, 4 * hs, S), lambda b, h: (h, 0, b)),
                  pl.BlockSpec((None, 4 * hs, 1), lambda b, h: (h, 0, 0)),
                  pl.BlockSpec((1, hs * E), lambda b, h: (0, h))],
        out_specs=seq,
        out_shape=jax.ShapeDtypeStruct((Bn, S, M_WIDTH), F32),
        compiler_params=_cparams("parallel", "parallel"), name="mlstm")(
            q3, k3, v3, g_rows.reshape(groups, 4 * hs, T), gate_b.reshape(groups, 4 * hs, 1),
            hnorm_g.reshape(1, M_WIDTH))
    return out.reshape(T, M_WIDTH)


def _mixer_ab(x, g, w_in, conv_w, conv_b, gate_b, hnorm_g, w_out, Bn, S):
    T = Bn * S
    MW, BW = M_WIDTH, B_WIDTH
    g0 = 4 * MW
    n_gate = 4 * M_HEADS
    w_qk = w_in[:, :2 * MW].astype(BF16)
    w_rest = jnp.concatenate([w_in[:, 2 * MW:g0], w_in[:, g0 + n_gate:]], axis=1).astype(BF16)
    wg_t = w_in[:, g0:g0 + n_gate].reshape(-1, 4, M_HEADS).transpose(2, 1, 0).reshape(n_gate, -1).astype(BF16)
    gate_b = gate_b.reshape(4, M_HEADS).T.astype(F32)
    (q_m, k_m, v_m, o_m, g_rows, q1, k1, v1, q4, k4, v4, q16, k16, v16) = _proj_ab(
        x, g, w_qk, w_rest, wg_t, conv_w, conv_b, S)

    hn = _mlstm(q_m, k_m, v_m, g_rows, gate_b, hnorm_g, Bn, S)

    slopes = _alibi_slopes(B_HEADS)
    outs, lses = [], []
    for d, qkv in zip(DILATIONS, ((q1, k1, v1), (q4, k4, v4), (q16, k16, v16))):
        qv, kv, vv = (t.reshape(Bn, S // d, d * BW) for t in qkv)
        o, lse = _banded_attention(qv, kv, vv, slopes, None, dilation=d, groups=1, radius=B_RADIUS,
                                   with_lse=True, out_dtype=F32)
        outs.append(o.reshape(T // d, d * BW))
        lses.append(lse.reshape(T // d, d * BW))
    w_out = w_out.astype(BF16)
    return _out_ab(x, hn, o_m, outs, lses, w_out[:MW], w_out[MW:])


def _c_head_order():
    order = []
    for pair in range(C_KV_HEADS // 2):
        for slot in range(C_GROUP):
            for j in range(2):
                order.append((2 * pair + j) * C_GROUP + slot)
    return jnp.array(order, dtype=jnp.int32)


def _mixer_c(x, g, w_in, sink, w_out, Bn, S):
    T, D = x.shape
    QW = C_HEADS * HEAD_DIM
    KW = C_KV_HEADS * HEAD_DIM
    order = _c_head_order()
    wq = w_in[:, :QW].reshape(D, C_HEADS, HEAD_DIM)[:, order].reshape(D, QW)
    w_main = jnp.concatenate([wq, w_in[:, QW:]], axis=1).astype(BF16)
    w_out = w_out.reshape(C_HEADS, HEAD_DIM, D)[order].reshape(QW, D).astype(BF16)
    segs = [(QW, BF16, HEAD_DIM ** -0.5), (KW, BF16, 1.0), (KW, BF16, 1.0)]
    q, k, v = _norm_proj(x, g, w_main, segs)
    (o,) = _banded_attention(q.reshape(Bn, S, QW), k.reshape(Bn, S, KW), v.reshape(Bn, S, KW),
                             _alibi_slopes(C_HEADS), sink.astype(F32), dilation=1, groups=C_GROUP,
                             radius=C_RADIUS, with_lse=False, out_dtype=BF16)
    return _out_c(x, o.reshape(T, QW), w_out)


def kernel(x, norm_g, ffn_w1, ffn_w3, ffn_w2, ab_w_in, ab_conv_w, ab_conv_b, ab_gate_b, ab_hnorm_g, ab_w_out,
           c_w_in, c_sink, c_w_out, final_g):
    Bn, S, D = x.shape
    depth = norm_g.shape[0]
    w1, w3, w2 = (w.astype(BF16) for w in (ffn_w1, ffn_w3, ffn_w2))
    x = x.reshape(Bn * S, D)
    for l in range(depth):
        j = l // 2
        x = _ffn(x, norm_g[l, 0], w1[l, 0], w3[l, 0], w2[l, 0])
        if l % 2 == 0:
            x = _mixer_ab(x, norm_g[l, 1], ab_w_in[j], ab_conv_w[j], ab_conv_b[j], ab_gate_b[j], ab_hnorm_g[j],
                          ab_w_out[j], Bn, S)
        else:
            x = _mixer_c(x, norm_g[l, 1], c_w_in[j], c_sink[j], c_w_out[j], Bn, S)
        x = _ffn(x, norm_g[l, 2], w1[l, 1], w3[l, 1], w2[l, 1], final_g if l == depth - 1 else None)
    return x.reshape(Bn, S, D)
```

```python
import functools

import jax
import jax.numpy as jnp
from jax import lax
from jax.experimental import pallas as pl
from jax.experimental.pallas import tpu as pltpu

F32 = jnp.float32
BF16 = jnp.bfloat16

EPS = 1e-6
M_HEADS = 4
M_HEAD_DIM = 128
M_WIDTH = M_HEADS * M_HEAD_DIM
M_CONV = 5
B_HEADS = 8
B_WIDTH = 512
DILATIONS = (1, 4, 16)
B_RADIUS = 64
C_HEADS = 16
C_KV_HEADS = 4
C_GROUP = C_HEADS // C_KV_HEADS
C_RADIUS = 128
HEAD_DIM = 64
LANES = 128
SUBLANES = 8

VMEM_LIMIT_BYTES = 56 * 1024 * 1024
ROW_TILE = 512
B_Q_BLOCK, B_CHAINS = 128, 32
C_Q_BLOCK, C_CHAINS = 128, 32
MLSTM_CHUNK = 256
CONV_COLS = 256
MLSTM_HEADS_PER_STEP = 2


def _cparams(*sem):
    return pltpu.CompilerParams(dimension_semantics=sem, vmem_limit_bytes=VMEM_LIMIT_BYTES)


def _resident(shape, index_map):
    return pl.BlockSpec(shape, index_map, pipeline_mode=pl.Buffered(1))


def _rms(x, g):
    return x * lax.rsqrt(jnp.mean(x * x, axis=-1, keepdims=True) + EPS) * g


def _ffn_kernel(x_ref, g_ref, w1_ref, w3_ref, w2_ref, *rest, final_norm):
    o_ref = rest[-1]
    x = x_ref[...]
    h = _rms(x, g_ref[...]).astype(BF16)
    a = jnp.dot(h, w1_ref[...], preferred_element_type=F32)
    b = jnp.dot(h, w3_ref[...], preferred_element_type=F32)
    act = (a * jax.nn.sigmoid(a) * b).astype(BF16)
    y = x + 0.5 * jnp.dot(act, w2_ref[...], preferred_element_type=F32)
    if final_norm:
        y = _rms(y, rest[0][...])
    o_ref[...] = y


def _ffn(x, g, w1, w3, w2, final_g=None):
    T, D = x.shape
    F = w1.shape[1]
    tm = ROW_TILE
    row = pl.BlockSpec((tm, D), lambda i: (i, 0))
    vec = _resident((1, D), lambda i: (0, 0))
    in_specs = [row, vec, _resident((D, F), lambda i: (0, 0)), _resident((D, F), lambda i: (0, 0)),
                _resident((F, D), lambda i: (0, 0))]
    args = [x, g.reshape(1, D), w1, w3, w2]
    if final_g is not None:
        in_specs.append(vec)
        args.append(final_g.reshape(1, D))
    return pl.pallas_call(
        functools.partial(_ffn_kernel, final_norm=final_g is not None),
        grid=(T // tm,), in_specs=in_specs, out_specs=row,
        out_shape=jax.ShapeDtypeStruct((T, D), F32),
        compiler_params=_cparams("parallel"), name="ffn")(*args)


def _proj_kernel(x_ref, g_ref, w_ref, *outs, segs):
    h = _rms(x_ref[...], g_ref[...]).astype(BF16)
    off = 0
    for (width, scale), o_ref in zip(segs, outs):
        y = jnp.dot(h, w_ref[:, off:off + width], preferred_element_type=F32)
        if scale != 1.0:
            y = y * scale
        o_ref[...] = y.astype(o_ref.dtype)
        off += width


def _norm_proj(x, g, w, segs):
    T, D = x.shape
    N = w.shape[1]
    tm = ROW_TILE
    return pl.pallas_call(
        functools.partial(_proj_kernel, segs=tuple((wd, sc) for wd, _, sc in segs)),
        grid=(T // tm,),
        in_specs=[pl.BlockSpec((tm, D), lambda i: (i, 0)), _resident((1, D), lambda i: (0, 0)),
                  _resident((D, N), lambda i: (0, 0))],
        out_specs=[pl.BlockSpec((tm, wd), lambda i: (i, 0)) for wd, _, _ in segs],
        out_shape=[jax.ShapeDtypeStruct((T, wd), dt) for wd, dt, _ in segs],
        compiler_params=_cparams("parallel"), name="norm_proj")(x, g.reshape(1, D), w)


def _proj_ab_kernel(x_ref, xp_ref, xn_ref, g_ref, wqk_ref, wr_ref, wg_ref, cw_ref, cb_ref,
                    qa_ref, ka_ref, vm_ref, om_ref, gr_ref, *rest, tiles_per_seq):
    tok_refs = rest[0:3]
    res_refs = {d: rest[3 * (n + 1):3 * (n + 2)] for n, d in enumerate(DILATIONS[1:])}
    stage_ref = rest[-1]
    tm = x_ref.shape[0]
    halo = SUBLANES
    g = g_ref[...]
    i = pl.program_id(0)
    first = i % tiles_per_seq == 0
    last = i % tiles_per_seq == tiles_per_seq - 1

    x = x_ref[...]
    h = _rms(x, g).astype(BF16)
    x_ext = jnp.concatenate([xp_ref[...], x, xn_ref[...]], axis=0)
    h_ext = _rms(x_ext, g).astype(BF16)
    n_ext = tm + 2 * halo
    row = lax.broadcasted_iota(jnp.int32, (n_ext, 1), 0)
    outside = ((row < halo) & first) | ((row >= halo + tm) & last)
    n_conv = 2 * M_WIDTH // CONV_COLS
    qk_blocks = [jnp.dot(h_ext, wqk_ref[:, blk * CONV_COLS:(blk + 1) * CONV_COLS], preferred_element_type=F32)
                 for blk in range(n_conv)]
    y_vm = jnp.dot(h, wr_ref[:, 0:M_WIDTH], preferred_element_type=F32)
    y_om = jnp.dot(h, wr_ref[:, M_WIDTH:2 * M_WIDTH], preferred_element_type=F32)
    y_g = lax.dot_general(wg_ref[...], h, (((1,), (1,)), ((), ())), preferred_element_type=F32)
    ys = [jnp.dot(h, wr_ref[:, 2 * M_WIDTH + s * B_WIDTH:2 * M_WIDTH + (s + 1) * B_WIDTH],
                  preferred_element_type=F32) for s in range(len(tok_refs))]

    for blk in range(n_conv):
        cols = slice(blk * CONV_COLS, (blk + 1) * CONV_COLS)
        qk = jnp.where(outside, 0.0, qk_blocks[blk])
        acc = cb_ref[:, cols] + qk[halo:halo + tm] * cw_ref[M_CONV // 2:M_CONV // 2 + 1, cols]
        for j in range(M_CONV):
            off = j - M_CONV // 2
            if off != 0:
                shifted = pltpu.roll(qk, (-off) % n_ext, axis=0)
                acc = acc + shifted[halo:halo + tm] * cw_ref[j:j + 1, cols]
        act = acc * jax.nn.sigmoid(acc)
        if blk * CONV_COLS < M_WIDTH:
            qa_ref[:, cols] = act.astype(BF16)
        else:
            kcols = slice(blk * CONV_COLS - M_WIDTH, (blk + 1) * CONV_COLS - M_WIDTH)
            ka_ref[:, kcols] = (act * (M_HEAD_DIM ** -0.5)).astype(BF16)

    vm_ref[...] = y_vm.astype(BF16)
    om_ref[...] = y_om.astype(BF16)
    gr_ref[...] = y_g

    n_col = B_WIDTH // LANES
    for s, tok_ref in enumerate(tok_refs):
        y = ys[s] * (HEAD_DIM ** -0.5) if s == 0 else ys[s]
        tok_ref[...] = y.astype(BF16)
        for c in range(n_col):
            stage_ref[s, c] = y[:, c * LANES:(c + 1) * LANES]
        for d, refs in res_refs.items():
            for r in range(d):
                for c in range(n_col):
                    lo = r * B_WIDTH + c * LANES
                    refs[s][:, lo:lo + LANES] = stage_ref[s, c, pl.ds(r, tm // d, stride=d), :].astype(BF16)


def _proj_ab(x, g, w_qk, w_rest, wg_t, conv_w, conv_b, seq_len):
    T, D = x.shape
    tm = ROW_TILE
    halo = SUBLANES
    hb = tm // halo
    row = lambda w: pl.BlockSpec((tm, w), lambda i: (i, 0))
    res = lambda d: pl.BlockSpec((tm // d, d * B_WIDTH), lambda i: (i, 0))
    full = lambda a: _resident(a.shape, lambda i: (0, 0))
    conv_b = conv_b.reshape(1, -1)
    g = g.reshape(1, D)
    out_specs = [row(M_WIDTH)] * 4 + [pl.BlockSpec((wg_t.shape[0], tm), lambda i: (0, i))] + [row(B_WIDTH)] * 3
    out_shape = ([jax.ShapeDtypeStruct((T, M_WIDTH), BF16)] * 4
                 + [jax.ShapeDtypeStruct((wg_t.shape[0], T), F32)]
                 + [jax.ShapeDtypeStruct((T, B_WIDTH), BF16)] * 3)
    for d in DILATIONS[1:]:
        out_specs += [res(d)] * 3
        out_shape += [jax.ShapeDtypeStruct((T // d, d * B_WIDTH), BF16)] * 3
    return pl.pallas_call(
        functools.partial(_proj_ab_kernel, tiles_per_seq=seq_len // tm),
        grid=(T // tm,),
        in_specs=[row(D),
                  pl.BlockSpec((halo, D), lambda i: (jnp.maximum(i * hb - 1, 0), 0)),
                  pl.BlockSpec((halo, D), lambda i: (jnp.minimum((i + 1) * hb, T // halo - 1), 0)),
                  full(g), full(w_qk), full(w_rest), full(wg_t), full(conv_w), full(conv_b)],
        out_specs=out_specs, out_shape=out_shape,
        scratch_shapes=[pltpu.VMEM((3, B_WIDTH // LANES, tm, LANES), F32)],
        compiler_params=_cparams("parallel"), name="proj_ab")(x, x, x, g, w_qk, w_rest, wg_t, conv_w, conv_b)


def _out_c_kernel(x_ref, a_ref, w_ref, o_ref):
    o_ref[...] = x_ref[...] + jnp.dot(a_ref[...], w_ref[...], preferred_element_type=F32)


def _out_c(x, att, w):
    T, D = x.shape
    K = att.shape[1]
    tm = ROW_TILE
    row = pl.BlockSpec((tm, D), lambda i: (i, 0))
    return pl.pallas_call(
        _out_c_kernel, grid=(T // tm,),
        in_specs=[row, pl.BlockSpec((tm, K), lambda i: (i, 0)), _resident((K, D), lambda i: (0, 0))],
        out_specs=row, out_shape=jax.ShapeDtypeStruct((T, D), F32),
        compiler_params=_cparams("parallel"), name="out_proj_c")(x, att, w)


def _out_ab_kernel(x_ref, hn_ref, om_ref, o1_ref, l1_ref, o4_ref, l4_ref, o16_ref, l16_ref, wa_ref, wb_ref, o_ref,
                   so4_ref, sl4_ref, so16_ref, sl16_ref):
    tm = x_ref.shape[0]
    n_col = B_WIDTH // LANES
    for d, pairs in ((DILATIONS[1], ((o4_ref, so4_ref), (l4_ref, sl4_ref))),
                     (DILATIONS[2], ((o16_ref, so16_ref), (l16_ref, sl16_ref)))):
        for src_ref, dst_ref in pairs:
            for r in range(d):
                for c in range(n_col):
                    lo = r * B_WIDTH + c * LANES
                    dst_ref[c, pl.ds(r, tm // d, stride=d), :] = src_ref[:, lo:lo + LANES].astype(F32)
    cols = []
    for c in range(n_col):
        lanes = slice(c * LANES, (c + 1) * LANES)
        l1, l2, l3 = l1_ref[:, lanes], sl4_ref[c], sl16_ref[c]
        lm = jnp.maximum(jnp.maximum(l1, l2), l3)
        e1, e2, e3 = jnp.exp(l1 - lm), jnp.exp(l2 - lm), jnp.exp(l3 - lm)
        ob = (e1 * o1_ref[:, lanes].astype(F32) + e2 * so4_ref[c] + e3 * so16_ref[c]) / (e1 + e2 + e3)
        cols.append(ob.astype(BF16))
    ob = jnp.concatenate(cols, axis=1)
    ma = (hn_ref[...].astype(F32) * jax.nn.sigmoid(om_ref[...].astype(F32))).astype(BF16)
    o_ref[...] = (x_ref[...] + jnp.dot(ma, wa_ref[...], preferred_element_type=F32)
                  + jnp.dot(ob, wb_ref[...], preferred_element_type=F32))


def _out_ab(x, hn, om, outs, lses, wa, wb):
    T, D = x.shape
    tm = ROW_TILE
    row = pl.BlockSpec((tm, D), lambda i: (i, 0))
    half = pl.BlockSpec((tm, M_WIDTH), lambda i: (i, 0))
    res = lambda d: pl.BlockSpec((tm // d, d * B_WIDTH), lambda i: (i, 0))
    wspec = _resident((M_WIDTH, D), lambda i: (0, 0))
    d4, d16 = DILATIONS[1:]
    stage = pltpu.VMEM((B_WIDTH // LANES, tm, LANES), F32)
    return pl.pallas_call(
        _out_ab_kernel, grid=(T // tm,),
        in_specs=[row, half, half, half, half, res(d4), res(d4), res(d16), res(d16), wspec, wspec],
        out_specs=row, out_shape=jax.ShapeDtypeStruct((T, D), F32),
        scratch_shapes=[stage] * 4,
        compiler_params=_cparams("parallel"), name="out_proj_ab")(
            x, hn, om, outs[0], lses[0], outs[1], lses[1], outs[2], lses[2], wa, wb)


def _attn_kernel(*refs, groups, q_tile, q_block, k_window, radius, dist_unit, seq_len, has_sink, with_lse):
    slope_ref = refs[0]
    pos = 1
    sink_ref = None
    if has_sink:
        sink_ref = refs[pos]
        pos += 1
    q_ref, k_ref, v_ref, o_ref = refs[pos:pos + 4]
    pos += 4
    lse_ref = None
    if with_lse:
        lse_ref = refs[pos]
        pos += 1
    bias_ref = refs[pos]
    pair = pl.program_id(0)
    tile = pl.program_id(3)

    shift_step = min(q_block, radius)

    @pl.when((pl.program_id(1) == 0) & (pl.program_id(2) == 0) & (tile == 0))
    def _():
        sub = lax.broadcasted_iota(jnp.int32, (q_block, k_window), 0)
        ln = lax.broadcasted_iota(jnp.int32, (q_block, k_window), 1)
        for variant in range(2 * radius // shift_step + 1):
            adist = jnp.abs(ln - sub - variant * shift_step)
            negd = jnp.where(adist <= radius, -(dist_unit * adist).astype(F32), -jnp.inf)
            for g in range(groups):
                for j in range(2):
                    bias_ref[variant, 2 * g + j] = slope_ref[(2 * pair + j) * groups + g] * negd

    lane = lax.broadcasted_iota(jnp.int32, (q_block, LANES), 1)
    low = lane < HEAD_DIM
    ones = jnp.ones((k_window, LANES), BF16)
    chains = []
    for blk in range(q_tile // q_block):
        q0 = tile * q_tile + blk * q_block
        ks = pl.multiple_of(jnp.clip(q0 - radius, 0, seq_len - k_window), HEAD_DIM)
        variant = (q0 - ks) // shift_step
        rows = slice(blk * q_block, (blk + 1) * q_block)
        kt = k_ref[pl.ds(ks, k_window), :]
        vt = jnp.concatenate([v_ref[pl.ds(ks, k_window), :], ones], axis=1)
        for g in range(groups):
            qt = q_ref[rows, g * LANES:(g + 1) * LANES]
            for j in range(2):
                qm = jnp.where(low if j == 0 else jnp.logical_not(low), qt, jnp.zeros_like(qt))
                chains.append(dict(qm=qm, kt=kt, vt=vt, variant=variant, slot=2 * g + j,
                                   head=(2 * pair + j) * groups + g))
    s = [lax.dot_general(c["qm"], c["kt"], (((1,), (1,)), ((), ())), preferred_element_type=F32)
         + bias_ref[c["variant"], c["slot"]] for c in chains]
    m = [jnp.max(x, axis=-1, keepdims=True) for x in s]
    if has_sink:
        m = [jnp.maximum(x, sink_ref[c["head"]]) for c, x in zip(chains, m)]
    p = [jnp.exp(x - y).astype(BF16) for x, y in zip(s, m)]
    od = [jnp.dot(x, c["vt"], preferred_element_type=F32) for c, x in zip(chains, p)]
    den = [x[:, LANES:] for x in od]
    if has_sink:
        den = [x + jnp.exp(sink_ref[c["head"]] - y) for c, x, y in zip(chains, den, m)]
    o = [x[:, :LANES] / y for x, y in zip(od, den)]
    lse = [x + jnp.log(y) for x, y in zip(m, den)] if with_lse else None
    for n in range(0, len(chains), 2):
        blk, g = divmod(n // 2, groups)
        rows = slice(blk * q_block, (blk + 1) * q_block)
        cols = slice(g * LANES, (g + 1) * LANES)
        o_ref[rows, cols] = jnp.where(low, o[n], o[n + 1]).astype(o_ref.dtype)
        if with_lse:
            lse_ref[rows, cols] = jnp.where(low, lse[n], lse[n + 1])


def _banded_attention(qv, kv, vv, slopes, sink, *, dilation, groups, radius, with_lse, out_dtype, q_block, chains):
    Bn, Lv, _ = qv.shape
    d = dilation
    pairs = kv.shape[2] // (d * LANES)
    q_tile = min(chains // (2 * groups) * q_block, Lv)
    k_window = q_block + 2 * radius
    shift_step = min(q_block, radius)
    assert max(q_block, radius) % shift_step == 0 and Lv % q_tile == 0 and k_window <= Lv
    qw = groups * LANES
    smem = pl.BlockSpec(memory_space=pltpu.SMEM)
    qspec = pl.BlockSpec((None, q_tile, qw), lambda p, b, r, i: (b, i, r * pairs + p))
    kspec = pl.BlockSpec((None, Lv, LANES), lambda p, b, r, i: (b, 0, r * pairs + p))
    in_specs = [smem] + ([smem] if sink is not None else []) + [qspec, kspec, kspec]
    args = [slopes] + ([sink] if sink is not None else []) + [qv, kv, vv]
    out_specs = [qspec]
    out_shape = [jax.ShapeDtypeStruct(qv.shape, out_dtype)]
    if with_lse:
        out_specs.append(qspec)
        out_shape.append(jax.ShapeDtypeStruct(qv.shape, F32))
    return pl.pallas_call(
        functools.partial(_attn_kernel, groups=groups, q_tile=q_tile, q_block=q_block, k_window=k_window,
                          radius=radius, dist_unit=d, seq_len=Lv, has_sink=sink is not None, with_lse=with_lse),
        grid=(pairs, Bn, d, Lv // q_tile), in_specs=in_specs, out_specs=out_specs, out_shape=out_shape,
        scratch_shapes=[pltpu.VMEM((2 * radius // shift_step + 1, 2 * groups, q_block, k_window), F32)],
        compiler_params=_cparams("arbitrary", "arbitrary", "arbitrary", "arbitrary"),
        name=f"banded_attention_d{d}")(*args)


def _alibi_slopes(n):
    return jnp.exp2(-8.0 * jnp.arange(1, n + 1, dtype=F32) / n)


def _log_sigmoid(x):
    return jnp.minimum(x, 0.0) - jnp.log1p(jnp.exp(-jnp.abs(x)))


def _mlstm_chunks(chains, seen, diag):
    nt = (((1,), (1,)), ((), ()))
    E = chains[0]["q"].shape[1]
    mask = [seen[c["rev"]] for c in chains]
    b_c = [jnp.sum(jnp.where(mk, c["lf_r"], 0.0), axis=1, keepdims=True) for c, mk in zip(chains, mask)]
    qk = [lax.dot_general(c["q"], c["k"], nt, preferred_element_type=F32) for c in chains]
    inter = [jnp.dot(c["q"], c["ct"].astype(BF16), preferred_element_type=F32) for c in chains]
    k_t = [c["k"].astype(F32).T for c in chains]
    b_last = [jnp.sum(c["lf_r"], axis=1, keepdims=True) for c in chains]
    b_r = [jnp.sum(jnp.where(diag, x, 0.0), axis=0, keepdims=True) for x in b_c]
    u_r = [c["i_r"] - x for c, x in zip(chains, b_r)]
    um = [jnp.where(mk, x, -jnp.inf) for mk, x in zip(mask, u_r)]
    g_t = [jnp.maximum(c["m"], jnp.max(x, axis=1, keepdims=True)) for c, x in zip(chains, um)]
    a_r = [x + y for x, y in zip(b_last, u_r)]
    m_new = [jnp.maximum(x + c["m"], jnp.max(y, axis=1, keepdims=True)) for c, x, y in zip(chains, b_last, a_r)]
    kw_t = [(x * jnp.exp(y - z)).astype(BF16) for x, y, z in zip(k_t, a_r, m_new)]
    ct = [jnp.exp(x + c["m"] - z) * c["ct"] + jnp.dot(y, c["v1"], preferred_element_type=F32)
          for c, x, y, z in zip(chains, b_last, kw_t, m_new)]
    w = [(jnp.exp(x - y) * z).astype(BF16) for x, y, z in zip(um, g_t, qk)]
    tot = [jnp.dot(x, c["v1"], preferred_element_type=F32) + jnp.exp(c["m"] - y) * z
           for c, x, y, z in zip(chains, w, g_t, inter)]
    h = [x[:, :E] / jnp.maximum(jnp.abs(x[:, E:]), jnp.exp(-(y + z))) for x, y, z in zip(tot, b_c, g_t)]
    return list(zip(h, ct, m_new))


def _mlstm_kernel(q_ref, k_ref, v_ref, gr_ref, gb_ref, hg_ref, o_ref, acc_ref, *, seq_len, chunk, heads):
    S, L, E = seq_len, chunk, M_HEAD_DIM
    n_chunks = S // L
    row = lax.broadcasted_iota(jnp.int32, (L, L), 0)
    col = lax.broadcasted_iota(jnp.int32, (L, L), 1)
    seen = (col <= row, col >= row)
    diag = col == row
    ones = jnp.ones((L, E), BF16)

    def finish(tot, lanes):
        return tot * lax.rsqrt(jnp.mean(tot * tot, axis=-1, keepdims=True) + EPS) * hg_ref[:, lanes]

    def sweep(second):
        def body(it, states):
            chains, where = [], []
            for n, (ct, m) in enumerate(states):
                hh, rev = divmod(n, 2)
                c = it if rev == 0 else n_chunks - 1 - it
                rows = pl.ds(pl.multiple_of(c * L, L), L)
                lanes = slice(hh * E, (hh + 1) * E)
                g = gr_ref[4 * hh:4 * hh + 4, rows] + gb_ref[4 * hh:4 * hh + 4, :]
                where.append((rows, lanes, acc_ref[rows, lanes] if second else None))
                chains.append(dict(q=q_ref[rows, lanes], k=k_ref[rows, lanes],
                                   v1=jnp.concatenate([v_ref[rows, lanes], ones], axis=1),
                                   i_r=g[rev:rev + 1], lf_r=_log_sigmoid(g[2 + rev:3 + rev]), ct=ct, m=m, rev=rev))
            out = _mlstm_chunks(chains, seen, diag)
            for (rows, lanes, prev), (h, _, _) in zip(where, out):
                if second:
                    o_ref[rows, lanes] = finish(prev + h, lanes).astype(o_ref.dtype)
                else:
                    acc_ref[rows, lanes] = h
            return tuple((ct, m) for _, ct, m in out)
        return body

    zero = (jnp.zeros((E, 2 * E), F32), jnp.zeros((1, 1), F32))
    states = lax.fori_loop(0, n_chunks // 2, sweep(False), (zero,) * (2 * heads))
    lax.fori_loop(n_chunks // 2, n_chunks, sweep(True), states)


def _mlstm(q, k, v, g_rows, gate_b, hnorm_g, Bn, S):
    E = M_HEAD_DIM
    T = Bn * S
    hs = MLSTM_HEADS_PER_STEP
    groups = M_HEADS // hs
    seq = pl.BlockSpec((None, S, hs * E), lambda b, h: (b, 0, h))
    q3, k3, v3 = (t.reshape(Bn, S, M_WIDTH) for t in (q, k, v))
    out = pl.pallas_call(
        functools.partial(_mlstm_kernel, seq_len=S, chunk=MLSTM_CHUNK, heads=hs),
        grid=(Bn, groups),
        in_specs=[seq, seq, seq,
                  pl.BlockSpec((None, 4 * hs, S), lambda b, h: (h, 0, b)),
                  pl.BlockSpec((None, 4 * hs, 1), lambda b, h: (h, 0, 0)),
                  pl.BlockSpec((1, hs * E), lambda b, h: (0, h))],
        out_specs=seq,
        out_shape=jax.ShapeDtypeStruct((Bn, S, M_WIDTH), BF16),
        scratch_shapes=[pltpu.VMEM((S, hs * E), F32)],
        compiler_params=_cparams("parallel", "parallel"), name="mlstm")(
            q3, k3, v3, g_rows.reshape(groups, 4 * hs, T), gate_b.reshape(groups, 4 * hs, 1),
            hnorm_g.reshape(1, M_WIDTH))
    return out.reshape(T, M_WIDTH)


def _mixer_ab(x, g, w_in, conv_w, conv_b, gate_b, hnorm_g, w_out, Bn, S):
    T = Bn * S
    MW, BW = M_WIDTH, B_WIDTH
    g0 = 4 * MW
    n_gate = 4 * M_HEADS
    w_qk = w_in[:, :2 * MW].astype(BF16)
    w_rest = jnp.concatenate([w_in[:, 2 * MW:g0], w_in[:, g0 + n_gate:]], axis=1).astype(BF16)
    wg_t = w_in[:, g0:g0 + n_gate].reshape(-1, 4, M_HEADS).transpose(2, 1, 0).reshape(n_gate, -1).astype(BF16)
    gate_b = gate_b.reshape(4, M_HEADS).T.astype(F32)
    (q_m, k_m, v_m, o_m, g_rows, q1, k1, v1, q4, k4, v4, q16, k16, v16) = _proj_ab(
        x, g, w_qk, w_rest, wg_t, conv_w, conv_b, S)

    hn = _mlstm(q_m, k_m, v_m, g_rows, gate_b, hnorm_g, Bn, S)

    slopes = _alibi_slopes(B_HEADS)
    outs, lses = [], []
    for d, qkv in zip(DILATIONS, ((q1, k1, v1), (q4, k4, v4), (q16, k16, v16))):
        qv, kv, vv = (t.reshape(Bn, S // d, d * BW) for t in qkv)
        o, lse = _banded_attention(qv, kv, vv, slopes, None, dilation=d, groups=1, radius=B_RADIUS,
                                   with_lse=True, out_dtype=BF16, q_block=B_Q_BLOCK, chains=B_CHAINS)
        outs.append(o.reshape(T // d, d * BW))
        lses.append(lse.reshape(T // d, d * BW))
    w_out = w_out.astype(BF16)
    return _out_ab(x, hn, o_m, outs, lses, w_out[:MW], w_out[MW:])


def _c_head_order():
    order = []
    for pair in range(C_KV_HEADS // 2):
        for slot in range(C_GROUP):
            for j in range(2):
                order.append((2 * pair + j) * C_GROUP + slot)
    return jnp.array(order, dtype=jnp.int32)


def _mixer_c(x, g, w_in, sink, w_out, Bn, S):
    T, D = x.shape
    QW = C_HEADS * HEAD_DIM
    KW = C_KV_HEADS * HEAD_DIM
    order = _c_head_order()
    wq = w_in[:, :QW].reshape(D, C_HEADS, HEAD_DIM)[:, order].reshape(D, QW)
    w_main = jnp.concatenate([wq, w_in[:, QW:]], axis=1).astype(BF16)
    w_out = w_out.reshape(C_HEADS, HEAD_DIM, D)[order].reshape(QW, D).astype(BF16)
    segs = [(QW, BF16, HEAD_DIM ** -0.5), (KW, BF16, 1.0), (KW, BF16, 1.0)]
    q, k, v = _norm_proj(x, g, w_main, segs)
    (o,) = _banded_attention(q.reshape(Bn, S, QW), k.reshape(Bn, S, KW), v.reshape(Bn, S, KW),
                             _alibi_slopes(C_HEADS), sink.astype(F32), dilation=1, groups=C_GROUP,
                             radius=C_RADIUS, with_lse=False, out_dtype=BF16, q_block=C_Q_BLOCK, chains=C_CHAINS)
    return _out_c(x, o.reshape(T, QW), w_out)


def kernel(x, norm_g, ffn_w1, ffn_w3, ffn_w2, ab_w_in, ab_conv_w, ab_conv_b, ab_gate_b, ab_hnorm_g, ab_w_out,
           c_w_in, c_sink, c_w_out, final_g):
    Bn, S, D = x.shape
    depth = norm_g.shape[0]
    w1, w3, w2 = (w.astype(BF16) for w in (ffn_w1, ffn_w3, ffn_w2))
    x = x.reshape(Bn * S, D)
    for l in range(depth):
        j = l // 2
        x = _ffn(x, norm_g[l, 0], w1[l, 0], w3[l, 0], w2[l, 0])
        if l % 2 == 0:
            x = _mixer_ab(x, norm_g[l, 1], ab_w_in[j], ab_conv_w[j], ab_conv_b[j], ab_gate_b[j], ab_hnorm_g[j],
                          ab_w_out[j], Bn, S)
        else:
            x = _mixer_c(x, norm_g[l, 1], c_w_in[j], c_sink[j], c_w_out[j], Bn, S)
        x = _ffn(x, norm_g[l, 2], w1[l, 1], w3[l, 1], w2[l, 1], final_g if l == depth - 1 else None)
    return x.reshape(Bn, S, D)
```

```python
import functools

import jax
import jax.numpy as jnp
from jax import lax
from jax.experimental import pallas as pl
from jax.experimental.pallas import tpu as pltpu

F32 = jnp.float32
BF16 = jnp.bfloat16

EPS = 1e-6
M_HEADS = 4
M_HEAD_DIM = 128
M_WIDTH = M_HEADS * M_HEAD_DIM
M_CONV = 5
B_HEADS = 8
B_WIDTH = 512
DILATIONS = (1, 4, 16)
B_RADIUS = 64
C_HEADS = 16
C_KV_HEADS = 4
C_GROUP = C_HEADS // C_KV_HEADS
C_RADIUS = 128
HEAD_DIM = 64
LANES = 128
SUBLANES = 8

VMEM_LIMIT_BYTES = 56 * 1024 * 1024
ROW_TILE = 512
B_Q_BLOCK, B_CHAINS = 128, 32
C_Q_BLOCK, C_CHAINS = 128, 32
MLSTM_CHUNK = 256
CONV_COLS = 256
MLSTM_HEADS_PER_STEP = 2


def _cparams(*sem):
    return pltpu.CompilerParams(dimension_semantics=sem, vmem_limit_bytes=VMEM_LIMIT_BYTES)


def _resident(shape, index_map):
    return pl.BlockSpec(shape, index_map, pipeline_mode=pl.Buffered(1))


def _rms(x, g):
    return x * lax.rsqrt(jnp.mean(x * x, axis=-1, keepdims=True) + EPS) * g


def _mix_c(a_ref, w_ref):
    return jnp.dot(a_ref[...], w_ref[...], preferred_element_type=F32)


def _mix_ab(hn_ref, om_ref, o1_ref, l1_ref, o4_ref, l4_ref, o16_ref, l16_ref, wa_ref, wb_ref,
            so4_ref, sl4_ref, so16_ref, sl16_ref):
    tm = hn_ref.shape[0]
    n_col = B_WIDTH // LANES
    for d, pairs in ((DILATIONS[1], ((o4_ref, so4_ref), (l4_ref, sl4_ref))),
                     (DILATIONS[2], ((o16_ref, so16_ref), (l16_ref, sl16_ref)))):
        for src_ref, dst_ref in pairs:
            for r in range(d):
                for c in range(n_col):
                    lo = r * B_WIDTH + c * LANES
                    dst_ref[c, pl.ds(r, tm // d, stride=d), :] = src_ref[:, lo:lo + LANES].astype(F32)
    cols = []
    for c in range(n_col):
        lanes = slice(c * LANES, (c + 1) * LANES)
        l1, l2, l3 = l1_ref[:, lanes], sl4_ref[c], sl16_ref[c]
        lm = jnp.maximum(jnp.maximum(l1, l2), l3)
        e1, e2, e3 = jnp.exp(l1 - lm), jnp.exp(l2 - lm), jnp.exp(l3 - lm)
        ob = (e1 * o1_ref[:, lanes].astype(F32) + e2 * so4_ref[c] + e3 * so16_ref[c]) / (e1 + e2 + e3)
        cols.append(ob.astype(BF16))
    ob = jnp.concatenate(cols, axis=1)
    ma = (hn_ref[...].astype(F32) * jax.nn.sigmoid(om_ref[...].astype(F32))).astype(BF16)
    return (jnp.dot(ma, wa_ref[...], preferred_element_type=F32)
            + jnp.dot(ob, wb_ref[...], preferred_element_type=F32))


def _ffn_kernel(x_ref, g_ref, w1_ref, w3_ref, w2_ref, *rest, final_norm, mix, n_mix_in, n_scratch):
    mix_in = rest[:n_mix_in]
    rest = rest[n_mix_in:]
    scratch = rest[len(rest) - n_scratch:]
    o_ref = rest[len(rest) - n_scratch - 1]
    x = x_ref[...]
    if mix is not None:
        x = x + mix(*mix_in, *scratch)
    h = _rms(x, g_ref[...]).astype(BF16)
    a = jnp.dot(h, w1_ref[...], preferred_element_type=F32)
    b = jnp.dot(h, w3_ref[...], preferred_element_type=F32)
    act = (a * jax.nn.sigmoid(a) * b).astype(BF16)
    y = x + 0.5 * jnp.dot(act, w2_ref[...], preferred_element_type=F32)
    if final_norm:
        y = _rms(y, rest[0][...])
    o_ref[...] = y


def _ffn(x, g, w1, w3, w2, final_g=None, mixer=None):
    T, D = x.shape
    F = w1.shape[1]
    tm = ROW_TILE
    row = lambda w: pl.BlockSpec((tm, w), lambda i: (i, 0))
    res = lambda d: pl.BlockSpec((tm // d, d * B_WIDTH), lambda i: (i, 0))
    full = lambda a: _resident(a.shape, lambda i: (0, 0))
    g = g.reshape(1, D)
    in_specs = [row(D), full(g), full(w1), full(w3), full(w2)]
    args = [x, g, w1, w3, w2]
    mix, n_mix_in, scratch = None, 0, []
    if mixer is not None and mixer[0] == "c":
        _, att, w_out = mixer
        mix, mix_args, mix_specs = _mix_c, [att, w_out], [row(att.shape[1]), full(w_out)]
    elif mixer is not None:
        _, hn, om, outs, lses, wa, wb = mixer
        d4, d16 = DILATIONS[1:]
        mix = _mix_ab
        mix_args = [hn, om, outs[0], lses[0], outs[1], lses[1], outs[2], lses[2], wa, wb]
        mix_specs = [row(M_WIDTH)] * 4 + [res(d4)] * 2 + [res(d16)] * 2 + [full(wa), full(wb)]
        scratch = [pltpu.VMEM((B_WIDTH // LANES, tm, LANES), F32)] * 4
    if mix is not None:
        n_mix_in = len(mix_args)
        in_specs += mix_specs
        args += mix_args
    if final_g is not None:
        final_g = final_g.reshape(1, D)
        in_specs.append(full(final_g))
        args.append(final_g)
    return pl.pallas_call(
        functools.partial(_ffn_kernel, final_norm=final_g is not None, mix=mix, n_mix_in=n_mix_in,
                          n_scratch=len(scratch)),
        grid=(T // tm,), in_specs=in_specs, out_specs=row(D),
        out_shape=jax.ShapeDtypeStruct((T, D), F32), scratch_shapes=scratch,
        compiler_params=_cparams("parallel"), name="ffn")(*args)


def _proj_kernel(x_ref, g_ref, w_ref, *outs, segs):
    h = _rms(x_ref[...], g_ref[...]).astype(BF16)
    off = 0
    for (width, scale), o_ref in zip(segs, outs):
        y = jnp.dot(h, w_ref[:, off:off + width], preferred_element_type=F32)
        if scale != 1.0:
            y = y * scale
        o_ref[...] = y.astype(o_ref.dtype)
        off += width


def _norm_proj(x, g, w, segs):
    T, D = x.shape
    N = w.shape[1]
    tm = ROW_TILE
    return pl.pallas_call(
        functools.partial(_proj_kernel, segs=tuple((wd, sc) for wd, _, sc in segs)),
        grid=(T // tm,),
        in_specs=[pl.BlockSpec((tm, D), lambda i: (i, 0)), _resident((1, D), lambda i: (0, 0)),
                  _resident((D, N), lambda i: (0, 0))],
        out_specs=[pl.BlockSpec((tm, wd), lambda i: (i, 0)) for wd, _, _ in segs],
        out_shape=[jax.ShapeDtypeStruct((T, wd), dt) for wd, dt, _ in segs],
        compiler_params=_cparams("parallel"), name="norm_proj")(x, g.reshape(1, D), w)


def _proj_ab_kernel(x_ref, xp_ref, xn_ref, g_ref, wqk_ref, wr_ref, wg_ref, cw_ref, cb_ref,
                    qa_ref, ka_ref, vm_ref, om_ref, gr_ref, *rest, tiles_per_seq):
    tok_refs = rest[0:3]
    res_refs = {d: rest[3 * (n + 1):3 * (n + 2)] for n, d in enumerate(DILATIONS[1:])}
    stage_ref = rest[-1]
    tm = x_ref.shape[0]
    halo = SUBLANES
    g = g_ref[...]
    i = pl.program_id(0)
    first = i % tiles_per_seq == 0
    last = i % tiles_per_seq == tiles_per_seq - 1

    x = x_ref[...]
    h = _rms(x, g).astype(BF16)
    x_ext = jnp.concatenate([xp_ref[...], x, xn_ref[...]], axis=0)
    h_ext = _rms(x_ext, g).astype(BF16)
    n_ext = tm + 2 * halo
    row = lax.broadcasted_iota(jnp.int32, (n_ext, 1), 0)
    outside = ((row < halo) & first) | ((row >= halo + tm) & last)
    n_conv = 2 * M_WIDTH // CONV_COLS
    qk_blocks = [jnp.dot(h_ext, wqk_ref[:, blk * CONV_COLS:(blk + 1) * CONV_COLS], preferred_element_type=F32)
                 for blk in range(n_conv)]
    y_vm = jnp.dot(h, wr_ref[:, 0:M_WIDTH], preferred_element_type=F32)
    y_om = jnp.dot(h, wr_ref[:, M_WIDTH:2 * M_WIDTH], preferred_element_type=F32)
    y_g = lax.dot_general(wg_ref[...], h, (((1,), (1,)), ((), ())), preferred_element_type=F32)
    ys = [jnp.dot(h, wr_ref[:, 2 * M_WIDTH + s * B_WIDTH:2 * M_WIDTH + (s + 1) * B_WIDTH],
                  preferred_element_type=F32) for s in range(len(tok_refs))]

    for blk in range(n_conv):
        cols = slice(blk * CONV_COLS, (blk + 1) * CONV_COLS)
        qk = jnp.where(outside, 0.0, qk_blocks[blk])
        acc = cb_ref[:, cols] + qk[halo:halo + tm] * cw_ref[M_CONV // 2:M_CONV // 2 + 1, cols]
        for j in range(M_CONV):
            off = j - M_CONV // 2
            if off != 0:
                shifted = pltpu.roll(qk, (-off) % n_ext, axis=0)
                acc = acc + shifted[halo:halo + tm] * cw_ref[j:j + 1, cols]
        act = acc * jax.nn.sigmoid(acc)
        if blk * CONV_COLS < M_WIDTH:
            qa_ref[:, cols] = act.astype(BF16)
        else:
            kcols = slice(blk * CONV_COLS - M_WIDTH, (blk + 1) * CONV_COLS - M_WIDTH)
            ka_ref[:, kcols] = (act * (M_HEAD_DIM ** -0.5)).astype(BF16)

    vm_ref[...] = y_vm.astype(BF16)
    om_ref[...] = y_om.astype(BF16)
    gr_ref[...] = y_g

    n_col = B_WIDTH // LANES
    for s, tok_ref in enumerate(tok_refs):
        y = ys[s] * (HEAD_DIM ** -0.5) if s == 0 else ys[s]
        tok_ref[...] = y.astype(BF16)
        for c in range(n_col):
            stage_ref[s, c] = y[:, c * LANES:(c + 1) * LANES]
        for d, refs in res_refs.items():
            for r in range(d):
                for c in range(n_col):
                    lo = r * B_WIDTH + c * LANES
                    refs[s][:, lo:lo + LANES] = stage_ref[s, c, pl.ds(r, tm // d, stride=d), :].astype(BF16)


def _proj_ab(x, g, w_qk, w_rest, wg_t, conv_w, conv_b, seq_len):
    T, D = x.shape
    tm = ROW_TILE
    halo = SUBLANES
    hb = tm // halo
    row = lambda w: pl.BlockSpec((tm, w), lambda i: (i, 0))
    res = lambda d: pl.BlockSpec((tm // d, d * B_WIDTH), lambda i: (i, 0))
    full = lambda a: _resident(a.shape, lambda i: (0, 0))
    conv_b = conv_b.reshape(1, -1)
    g = g.reshape(1, D)
    out_specs = [row(M_WIDTH)] * 4 + [pl.BlockSpec((wg_t.shape[0], tm), lambda i: (0, i))] + [row(B_WIDTH)] * 3
    out_shape = ([jax.ShapeDtypeStruct((T, M_WIDTH), BF16)] * 4
                 + [jax.ShapeDtypeStruct((wg_t.shape[0], T), F32)]
                 + [jax.ShapeDtypeStruct((T, B_WIDTH), BF16)] * 3)
    for d in DILATIONS[1:]:
        out_specs += [res(d)] * 3
        out_shape += [jax.ShapeDtypeStruct((T // d, d * B_WIDTH), BF16)] * 3
    return pl.pallas_call(
        functools.partial(_proj_ab_kernel, tiles_per_seq=seq_len // tm),
        grid=(T // tm,),
        in_specs=[row(D),
                  pl.BlockSpec((halo, D), lambda i: (jnp.maximum(i * hb - 1, 0), 0)),
                  pl.BlockSpec((halo, D), lambda i: (jnp.minimum((i + 1) * hb, T // halo - 1), 0)),
                  full(g), full(w_qk), full(w_rest), full(wg_t), full(conv_w), full(conv_b)],
        out_specs=out_specs, out_shape=out_shape,
        scratch_shapes=[pltpu.VMEM((3, B_WIDTH // LANES, tm, LANES), F32)],
        compiler_params=_cparams("parallel"), name="proj_ab")(x, x, x, g, w_qk, w_rest, wg_t, conv_w, conv_b)


def _attn_kernel(*refs, groups, q_tile, q_block, k_window, radius, dist_unit, seq_len, has_sink, with_lse):
    slope_ref = refs[0]
    pos = 1
    sink_ref = None
    if has_sink:
        sink_ref = refs[pos]
        pos += 1
    q_ref, k_ref, v_ref, o_ref = refs[pos:pos + 4]
    pos += 4
    lse_ref = None
    if with_lse:
        lse_ref = refs[pos]
        pos += 1
    bias_ref = refs[pos]
    pair = pl.program_id(0)
    tile = pl.program_id(3)

    shift_step = min(q_block, radius)

    @pl.when((pl.program_id(1) == 0) & (pl.program_id(2) == 0) & (tile == 0))
    def _():
        sub = lax.broadcasted_iota(jnp.int32, (q_block, k_window), 0)
        ln = lax.broadcasted_iota(jnp.int32, (q_block, k_window), 1)
        for variant in range(2 * radius // shift_step + 1):
            adist = jnp.abs(ln - sub - variant * shift_step)
            negd = jnp.where(adist <= radius, -(dist_unit * adist).astype(F32), -jnp.inf)
            for g in range(groups):
                for j in range(2):
                    bias_ref[variant, 2 * g + j] = slope_ref[(2 * pair + j) * groups + g] * negd

    lane = lax.broadcasted_iota(jnp.int32, (q_block, LANES), 1)
    low = lane < HEAD_DIM
    ones = jnp.ones((k_window, LANES), BF16)
    chains = []
    for blk in range(q_tile // q_block):
        q0 = tile * q_tile + blk * q_block
        ks = pl.multiple_of(jnp.clip(q0 - radius, 0, seq_len - k_window), HEAD_DIM)
        variant = (q0 - ks) // shift_step
        rows = slice(blk * q_block, (blk + 1) * q_block)
        kt = k_ref[pl.ds(ks, k_window), :]
        vt = jnp.concatenate([v_ref[pl.ds(ks, k_window), :], ones], axis=1)
        for g in range(groups):
            qt = q_ref[rows, g * LANES:(g + 1) * LANES]
            for j in range(2):
                qm = jnp.where(low if j == 0 else jnp.logical_not(low), qt, jnp.zeros_like(qt))
                chains.append(dict(qm=qm, kt=kt, vt=vt, variant=variant, slot=2 * g + j,
                                   head=(2 * pair + j) * groups + g))
    s = [lax.dot_general(c["qm"], c["kt"], (((1,), (1,)), ((), ())), preferred_element_type=F32)
         + bias_ref[c["variant"], c["slot"]] for c in chains]
    m = [jnp.max(x, axis=-1, keepdims=True) for x in s]
    if has_sink:
        m = [jnp.maximum(x, sink_ref[c["head"]]) for c, x in zip(chains, m)]
    p = [jnp.exp(x - y).astype(BF16) for x, y in zip(s, m)]
    od = [jnp.dot(x, c["vt"], preferred_element_type=F32) for c, x in zip(chains, p)]
    den = [x[:, LANES:] for x in od]
    if has_sink:
        den = [x + jnp.exp(sink_ref[c["head"]] - y) for c, x, y in zip(chains, den, m)]
    o = [x[:, :LANES] / y for x, y in zip(od, den)]
    lse = [x + jnp.log(y) for x, y in zip(m, den)] if with_lse else None
    for n in range(0, len(chains), 2):
        blk, g = divmod(n // 2, groups)
        rows = slice(blk * q_block, (blk + 1) * q_block)
        cols = slice(g * LANES, (g + 1) * LANES)
        o_ref[rows, cols] = jnp.where(low, o[n], o[n + 1]).astype(o_ref.dtype)
        if with_lse:
            lse_ref[rows, cols] = jnp.where(low, lse[n], lse[n + 1])


def _banded_attention(qv, kv, vv, slopes, sink, *, dilation, groups, radius, with_lse, out_dtype, q_block, chains):
    Bn, Lv, _ = qv.shape
    d = dilation
    pairs = kv.shape[2] // (d * LANES)
    q_tile = min(chains // (2 * groups) * q_block, Lv)
    k_window = q_block + 2 * radius
    shift_step = min(q_block, radius)
    assert max(q_block, radius) % shift_step == 0 and Lv % q_tile == 0 and k_window <= Lv
    qw = groups * LANES
    smem = pl.BlockSpec(memory_space=pltpu.SMEM)
    qspec = pl.BlockSpec((None, q_tile, qw), lambda p, b, r, i: (b, i, r * pairs + p))
    kspec = pl.BlockSpec((None, Lv, LANES), lambda p, b, r, i: (b, 0, r * pairs + p))
    in_specs = [smem] + ([smem] if sink is not None else []) + [qspec, kspec, kspec]
    args = [slopes] + ([sink] if sink is not None else []) + [qv, kv, vv]
    out_specs = [qspec]
    out_shape = [jax.ShapeDtypeStruct(qv.shape, out_dtype)]
    if with_lse:
        out_specs.append(qspec)
        out_shape.append(jax.ShapeDtypeStruct(qv.shape, F32))
    return pl.pallas_call(
        functools.partial(_attn_kernel, groups=groups, q_tile=q_tile, q_block=q_block, k_window=k_window,
                          radius=radius, dist_unit=d, seq_len=Lv, has_sink=sink is not None, with_lse=with_lse),
        grid=(pairs, Bn, d, Lv // q_tile), in_specs=in_specs, out_specs=out_specs, out_shape=out_shape,
        scratch_shapes=[pltpu.VMEM((2 * radius // shift_step + 1, 2 * groups, q_block, k_window), F32)],
        compiler_params=_cparams("arbitrary", "arbitrary", "arbitrary", "arbitrary"),
        name=f"banded_attention_d{d}")(*args)


def _alibi_slopes(n):
    return jnp.exp2(-8.0 * jnp.arange(1, n + 1, dtype=F32) / n)


def _log_sigmoid(x):
    return jnp.minimum(x, 0.0) - jnp.log1p(jnp.exp(-jnp.abs(x)))


def _mlstm_chunks(chains, seen, diag):
    nt = (((1,), (1,)), ((), ()))
    E = chains[0]["q"].shape[1]
    mask = [seen[c["rev"]] for c in chains]
    b_c = [jnp.sum(jnp.where(mk, c["lf_r"], 0.0), axis=1, keepdims=True) for c, mk in zip(chains, mask)]
    qk = [lax.dot_general(c["q"], c["k"], nt, preferred_element_type=F32) for c in chains]
    inter = [jnp.dot(c["q"], c["ct"].astype(BF16), preferred_element_type=F32) for c in chains]
    k_t = [c["k"].astype(F32).T for c in chains]
    b_last = [jnp.sum(c["lf_r"], axis=1, keepdims=True) for c in chains]
    b_r = [jnp.sum(jnp.where(diag, x, 0.0), axis=0, keepdims=True) for x in b_c]
    u_r = [c["i_r"] - x for c, x in zip(chains, b_r)]
    um = [jnp.where(mk, x, -jnp.inf) for mk, x in zip(mask, u_r)]
    g_t = [jnp.maximum(c["m"], jnp.max(x, axis=1, keepdims=True)) for c, x in zip(chains, um)]
    a_r = [x + y for x, y in zip(b_last, u_r)]
    m_new = [jnp.maximum(x + c["m"], jnp.max(y, axis=1, keepdims=True)) for c, x, y in zip(chains, b_last, a_r)]
    kw_t = [(x * jnp.exp(y - z)).astype(BF16) for x, y, z in zip(k_t, a_r, m_new)]
    ct = [jnp.exp(x + c["m"] - z) * c["ct"] + jnp.dot(y, c["v1"], preferred_element_type=F32)
          for c, x, y, z in zip(chains, b_last, kw_t, m_new)]
    w = [(jnp.exp(x - y) * z).astype(BF16) for x, y, z in zip(um, g_t, qk)]
    tot = [jnp.dot(x, c["v1"], preferred_element_type=F32) + jnp.exp(c["m"] - y) * z
           for c, x, y, z in zip(chains, w, g_t, inter)]
    h = [x[:, :E] / jnp.maximum(jnp.abs(x[:, E:]), jnp.exp(-(y + z))) for x, y, z in zip(tot, b_c, g_t)]
    return list(zip(h, ct, m_new))


def _mlstm_kernel(q_ref, k_ref, v_ref, gr_ref, gb_ref, hg_ref, o_ref, acc_ref, *, seq_len, chunk, heads):
    S, L, E = seq_len, chunk, M_HEAD_DIM
    n_chunks = S // L
    row = lax.broadcasted_iota(jnp.int32, (L, L), 0)
    col = lax.broadcasted_iota(jnp.int32, (L, L), 1)
    seen = (col <= row, col >= row)
    diag = col == row
    ones = jnp.ones((L, E), BF16)

    def finish(tot, lanes):
        return tot * lax.rsqrt(jnp.mean(tot * tot, axis=-1, keepdims=True) + EPS) * hg_ref[:, lanes]

    def sweep(second):
        def body(it, states):
            chains, where = [], []
            for n, (ct, m) in enumerate(states):
                hh, rev = divmod(n, 2)
                c = it if rev == 0 else n_chunks - 1 - it
                rows = pl.ds(pl.multiple_of(c * L, L), L)
                lanes = slice(hh * E, (hh + 1) * E)
                g = gr_ref[4 * hh:4 * hh + 4, rows] + gb_ref[4 * hh:4 * hh + 4, :]
                where.append((rows, lanes, acc_ref[rows, lanes] if second else None))
                chains.append(dict(q=q_ref[rows, lanes], k=k_ref[rows, lanes],
                                   v1=jnp.concatenate([v_ref[rows, lanes], ones], axis=1),
                                   i_r=g[rev:rev + 1], lf_r=_log_sigmoid(g[2 + rev:3 + rev]), ct=ct, m=m, rev=rev))
            out = _mlstm_chunks(chains, seen, diag)
            for (rows, lanes, prev), (h, _, _) in zip(where, out):
                if second:
                    o_ref[rows, lanes] = finish(prev + h, lanes).astype(o_ref.dtype)
                else:
                    acc_ref[rows, lanes] = h
            return tuple((ct, m) for _, ct, m in out)
        return body

    zero = (jnp.zeros((E, 2 * E), F32), jnp.zeros((1, 1), F32))
    states = lax.fori_loop(0, n_chunks // 2, sweep(False), (zero,) * (2 * heads))
    lax.fori_loop(n_chunks // 2, n_chunks, sweep(True), states)


def _mlstm(q, k, v, g_rows, gate_b, hnorm_g, Bn, S):
    E = M_HEAD_DIM
    T = Bn * S
    hs = MLSTM_HEADS_PER_STEP
    groups = M_HEADS // hs
    seq = pl.BlockSpec((None, S, hs * E), lambda b, h: (b, 0, h))
    q3, k3, v3 = (t.reshape(Bn, S, M_WIDTH) for t in (q, k, v))
    out = pl.pallas_call(
        functools.partial(_mlstm_kernel, seq_len=S, chunk=MLSTM_CHUNK, heads=hs),
        grid=(Bn, groups),
        in_specs=[seq, seq, seq,
                  pl.BlockSpec((None, 4 * hs, S), lambda b, h: (h, 0, b)),
                  pl.BlockSpec((None, 4 * hs, 1), lambda b, h: (h, 0, 0)),
                  pl.BlockSpec((1, hs * E), lambda b, h: (0, h))],
        out_specs=seq,
        out_shape=jax.ShapeDtypeStruct((Bn, S, M_WIDTH), BF16),
        scratch_shapes=[pltpu.VMEM((S, hs * E), F32)],
        compiler_params=_cparams("parallel", "parallel"), name="mlstm")(
            q3, k3, v3, g_rows.reshape(groups, 4 * hs, T), gate_b.reshape(groups, 4 * hs, 1),
            hnorm_g.reshape(1, M_WIDTH))
    return out.reshape(T, M_WIDTH)


def _mixer_ab(x, g, w_in, conv_w, conv_b, gate_b, hnorm_g, w_out, Bn, S):
    T = Bn * S
    MW, BW = M_WIDTH, B_WIDTH
    g0 = 4 * MW
    n_gate = 4 * M_HEADS
    w_qk = w_in[:, :2 * MW].astype(BF16)
    w_rest = jnp.concatenate([w_in[:, 2 * MW:g0], w_in[:, g0 + n_gate:]], axis=1).astype(BF16)
    wg_t = w_in[:, g0:g0 + n_gate].reshape(-1, 4, M_HEADS).transpose(2, 1, 0).reshape(n_gate, -1).astype(BF16)
    gate_b = gate_b.reshape(4, M_HEADS).T.astype(F32)
    (q_m, k_m, v_m, o_m, g_rows, q1, k1, v1, q4, k4, v4, q16, k16, v16) = _proj_ab(
        x, g, w_qk, w_rest, wg_t, conv_w, conv_b, S)

    hn = _mlstm(q_m, k_m, v_m, g_rows, gate_b, hnorm_g, Bn, S)

    slopes = _alibi_slopes(B_HEADS)
    outs, lses = [], []
    for d, qkv in zip(DILATIONS, ((q1, k1, v1), (q4, k4, v4), (q16, k16, v16))):
        qv, kv, vv = (t.reshape(Bn, S // d, d * BW) for t in qkv)
        o, lse = _banded_attention(qv, kv, vv, slopes, None, dilation=d, groups=1, radius=B_RADIUS,
                                   with_lse=True, out_dtype=BF16, q_block=B_Q_BLOCK, chains=B_CHAINS)
        outs.append(o.reshape(T // d, d * BW))
        lses.append(lse.reshape(T // d, d * BW))
    w_out = w_out.astype(BF16)
    return ("ab", hn, o_m, outs, lses, w_out[:MW], w_out[MW:])


def _c_head_order():
    order = []
    for pair in range(C_KV_HEADS // 2):
        for slot in range(C_GROUP):
            for j in range(2):
                order.append((2 * pair + j) * C_GROUP + slot)
    return jnp.array(order, dtype=jnp.int32)


def _mixer_c(x, g, w_in, sink, w_out, Bn, S):
    T, D = x.shape
    QW = C_HEADS * HEAD_DIM
    KW = C_KV_HEADS * HEAD_DIM
    order = _c_head_order()
    wq = w_in[:, :QW].reshape(D, C_HEADS, HEAD_DIM)[:, order].reshape(D, QW)
    w_main = jnp.concatenate([wq, w_in[:, QW:]], axis=1).astype(BF16)
    w_out = w_out.reshape(C_HEADS, HEAD_DIM, D)[order].reshape(QW, D).astype(BF16)
    segs = [(QW, BF16, HEAD_DIM ** -0.5), (KW, BF16, 1.0), (KW, BF16, 1.0)]
    q, k, v = _norm_proj(x, g, w_main, segs)
    (o,) = _banded_attention(q.reshape(Bn, S, QW), k.reshape(Bn, S, KW), v.reshape(Bn, S, KW),
                             _alibi_slopes(C_HEADS), sink.astype(F32), dilation=1, groups=C_GROUP,
                             radius=C_RADIUS, with_lse=False, out_dtype=BF16, q_block=C_Q_BLOCK, chains=C_CHAINS)
    return ("c", o.reshape(T, QW), w_out)


def kernel(x, norm_g, ffn_w1, ffn_w3, ffn_w2, ab_w_in, ab_conv_w, ab_conv_b, ab_gate_b, ab_hnorm_g, ab_w_out,
           c_w_in, c_sink, c_w_out, final_g):
    Bn, S, D = x.shape
    depth = norm_g.shape[0]
    w1, w3, w2 = (w.astype(BF16) for w in (ffn_w1, ffn_w3, ffn_w2))
    x = x.reshape(Bn * S, D)
    for l in range(depth):
        j = l // 2
        x = _ffn(x, norm_g[l, 0], w1[l, 0], w3[l, 0], w2[l, 0])
        if l % 2 == 0:
            mixed = _mixer_ab(x, norm_g[l, 1], ab_w_in[j], ab_conv_w[j], ab_conv_b[j], ab_gate_b[j], ab_hnorm_g[j],
                              ab_w_out[j], Bn, S)
        else:
            mixed = _mixer_c(x, norm_g[l, 1], c_w_in[j], c_sink[j], c_w_out[j], Bn, S)
        x = _ffn(x, norm_g[l, 2], w1[l, 1], w3[l, 1], w2[l, 1], final_g if l == depth - 1 else None, mixed)
    return x.reshape(Bn, S, D)
```

```python
import functools

import jax
import jax.numpy as jnp
from jax import lax
from jax.experimental import pallas as pl
from jax.experimental.pallas import tpu as pltpu

F32 = jnp.float32
BF16 = jnp.bfloat16

EPS = 1e-6
M_HEADS = 4
M_HEAD_DIM = 128
M_WIDTH = M_HEADS * M_HEAD_DIM
M_CONV = 5
B_HEADS = 8
B_WIDTH = 512
DILATIONS = (1, 4, 16)
B_RADIUS = 64
C_HEADS = 16
C_KV_HEADS = 4
C_GROUP = C_HEADS // C_KV_HEADS
C_RADIUS = 128
HEAD_DIM = 64
LANES = 128
SUBLANES = 8

VMEM_LIMIT_BYTES = 56 * 1024 * 1024
ROW_TILE = 512
B_Q_BLOCK, B_CHAINS = 128, 32
C_Q_BLOCK, C_CHAINS = 128, 32
MLSTM_CHUNK = 256
CONV_COLS = 256
MLSTM_HEADS_PER_STEP = 2


def _cparams(*sem):
    return pltpu.CompilerParams(dimension_semantics=sem, vmem_limit_bytes=VMEM_LIMIT_BYTES)


def _resident(shape, index_map):
    return pl.BlockSpec(shape, index_map, pipeline_mode=pl.Buffered(1))


def _rms(x, g):
    return x * lax.rsqrt(jnp.mean(x * x, axis=-1, keepdims=True) + EPS) * g


def _mix_c(a_ref, w_ref):
    return jnp.dot(a_ref[...], w_ref[...], preferred_element_type=F32)


def _mix_ab(hn_ref, om_ref, o1_ref, l1_ref, o4_ref, l4_ref, o16_ref, l16_ref, wa_ref, wb_ref,
            so4_ref, sl4_ref, so16_ref, sl16_ref):
    tm = hn_ref.shape[0]
    n_col = B_WIDTH // LANES
    for d, pairs in ((DILATIONS[1], ((o4_ref, so4_ref), (l4_ref, sl4_ref))),
                     (DILATIONS[2], ((o16_ref, so16_ref), (l16_ref, sl16_ref)))):
        for src_ref, dst_ref in pairs:
            for r in range(d):
                for c in range(n_col):
                    lo = (c * d + r) * LANES
                    dst_ref[c, pl.ds(r, tm // d, stride=d), :] = src_ref[:, lo:lo + LANES].astype(F32)
    cols = []
    for c in range(n_col):
        lanes = slice(c * LANES, (c + 1) * LANES)
        l1, l2, l3 = l1_ref[:, lanes], sl4_ref[c], sl16_ref[c]
        lm = jnp.maximum(jnp.maximum(l1, l2), l3)
        e1, e2, e3 = jnp.exp(l1 - lm), jnp.exp(l2 - lm), jnp.exp(l3 - lm)
        ob = (e1 * o1_ref[:, lanes].astype(F32) + e2 * so4_ref[c] + e3 * so16_ref[c]) / (e1 + e2 + e3)
        cols.append(ob.astype(BF16))
    ob = jnp.concatenate(cols, axis=1)
    ma = (hn_ref[...].astype(F32) * jax.nn.sigmoid(om_ref[...].astype(F32))).astype(BF16)
    return (jnp.dot(ma, wa_ref[...], preferred_element_type=F32)
            + jnp.dot(ob, wb_ref[...], preferred_element_type=F32))


def _ffn_kernel(x_ref, g_ref, w1_ref, w3_ref, w2_ref, *rest, final_norm, mix, n_mix_in, n_scratch):
    mix_in = rest[:n_mix_in]
    rest = rest[n_mix_in:]
    scratch = rest[len(rest) - n_scratch:]
    o_ref = rest[len(rest) - n_scratch - 1]
    x = x_ref[...]
    if mix is not None:
        x = x + mix(*mix_in, *scratch)
    h = _rms(x, g_ref[...]).astype(BF16)
    a = jnp.dot(h, w1_ref[...], preferred_element_type=F32)
    b = jnp.dot(h, w3_ref[...], preferred_element_type=F32)
    act = (a * jax.nn.sigmoid(a) * b).astype(BF16)
    y = x + 0.5 * jnp.dot(act, w2_ref[...], preferred_element_type=F32)
    if final_norm:
        y = _rms(y, rest[0][...])
    o_ref[...] = y


def _ffn(x, g, w1, w3, w2, which, final_g=None, mixer=None):
    T, D = x.shape
    tm = ROW_TILE
    layer, half = which
    stacked = lambda w: pl.BlockSpec((None, None) + w.shape[2:], lambda i: (layer, half, 0, 0),
                                     pipeline_mode=pl.Buffered(1))
    row = lambda w: pl.BlockSpec((tm, w), lambda i: (i, 0))
    res = lambda d: pl.BlockSpec((tm // d, d * B_WIDTH), lambda i: (i, 0))
    full = lambda a: _resident(a.shape, lambda i: (0, 0))
    g = g.reshape(1, D)
    in_specs = [row(D), full(g), stacked(w1), stacked(w3), stacked(w2)]
    args = [x, g, w1, w3, w2]
    mix, n_mix_in, scratch = None, 0, []
    if mixer is not None and mixer[0] == "c":
        _, att, w_out = mixer
        mix, mix_args, mix_specs = _mix_c, [att, w_out], [row(att.shape[1]), full(w_out)]
    elif mixer is not None:
        _, hn, om, outs, lses, wa, wb = mixer
        d4, d16 = DILATIONS[1:]
        mix = _mix_ab
        mix_args = [hn, om, outs[0], lses[0], outs[1], lses[1], outs[2], lses[2], wa, wb]
        mix_specs = [row(M_WIDTH)] * 4 + [res(d4)] * 2 + [res(d16)] * 2 + [full(wa), full(wb)]
        scratch = [pltpu.VMEM((B_WIDTH // LANES, tm, LANES), F32)] * 4
    if mix is not None:
        n_mix_in = len(mix_args)
        in_specs += mix_specs
        args += mix_args
    if final_g is not None:
        final_g = final_g.reshape(1, D)
        in_specs.append(full(final_g))
        args.append(final_g)
    return pl.pallas_call(
        functools.partial(_ffn_kernel, final_norm=final_g is not None, mix=mix, n_mix_in=n_mix_in,
                          n_scratch=len(scratch)),
        grid=(T // tm,), in_specs=in_specs, out_specs=row(D),
        out_shape=jax.ShapeDtypeStruct((T, D), F32), scratch_shapes=scratch,
        compiler_params=_cparams("parallel"), name="ffn")(*args)


def _proj_kernel(x_ref, g_ref, w_ref, *outs, segs):
    h = _rms(x_ref[...], g_ref[...]).astype(BF16)
    off = 0
    for (width, scale), o_ref in zip(segs, outs):
        y = jnp.dot(h, w_ref[:, off:off + width], preferred_element_type=F32)
        if scale != 1.0:
            y = y * scale
        o_ref[...] = y.astype(o_ref.dtype)
        off += width


def _norm_proj(x, g, w, segs):
    T, D = x.shape
    N = w.shape[1]
    tm = ROW_TILE
    return pl.pallas_call(
        functools.partial(_proj_kernel, segs=tuple((wd, sc) for wd, _, sc in segs)),
        grid=(T // tm,),
        in_specs=[pl.BlockSpec((tm, D), lambda i: (i, 0)), _resident((1, D), lambda i: (0, 0)),
                  _resident((D, N), lambda i: (0, 0))],
        out_specs=[pl.BlockSpec((tm, wd), lambda i: (i, 0)) for wd, _, _ in segs],
        out_shape=[jax.ShapeDtypeStruct((T, wd), dt) for wd, dt, _ in segs],
        compiler_params=_cparams("parallel"), name="norm_proj")(x, g.reshape(1, D), w)


def _proj_ab_kernel(x_ref, xp_ref, xn_ref, g_ref, wqk_ref, wr_ref, wg_ref, cw_ref, cb_ref,
                    qa_ref, ka_ref, vm_ref, om_ref, gr_ref, *rest, tiles_per_seq):
    tok_refs = rest[0:3]
    res_refs = {d: rest[3 * (n + 1):3 * (n + 2)] for n, d in enumerate(DILATIONS[1:])}
    stage_ref = rest[-1]
    tm = x_ref.shape[0]
    halo = SUBLANES
    g = g_ref[...]
    i = pl.program_id(0)
    first = i % tiles_per_seq == 0
    last = i % tiles_per_seq == tiles_per_seq - 1

    x = x_ref[...]
    h = _rms(x, g).astype(BF16)
    x_ext = jnp.concatenate([xp_ref[...], x, xn_ref[...]], axis=0)
    h_ext = _rms(x_ext, g).astype(BF16)
    n_ext = tm + 2 * halo
    row = lax.broadcasted_iota(jnp.int32, (n_ext, 1), 0)
    outside = ((row < halo) & first) | ((row >= halo + tm) & last)
    n_conv = 2 * M_WIDTH // CONV_COLS
    qk_blocks = [jnp.dot(h_ext, wqk_ref[:, blk * CONV_COLS:(blk + 1) * CONV_COLS], preferred_element_type=F32)
                 for blk in range(n_conv)]
    y_vm = jnp.dot(h, wr_ref[:, 0:M_WIDTH], preferred_element_type=F32)
    y_om = jnp.dot(h, wr_ref[:, M_WIDTH:2 * M_WIDTH], preferred_element_type=F32)
    y_g = lax.dot_general(wg_ref[...], h, (((1,), (1,)), ((), ())), preferred_element_type=F32)
    ys = [jnp.dot(h, wr_ref[:, 2 * M_WIDTH + s * B_WIDTH:2 * M_WIDTH + (s + 1) * B_WIDTH],
                  preferred_element_type=F32) for s in range(len(tok_refs))]

    for blk in range(n_conv):
        cols = slice(blk * CONV_COLS, (blk + 1) * CONV_COLS)
        qk = jnp.where(outside, 0.0, qk_blocks[blk])
        acc = cb_ref[:, cols] + qk[halo:halo + tm] * cw_ref[M_CONV // 2:M_CONV // 2 + 1, cols]
        for j in range(M_CONV):
            off = j - M_CONV // 2
            if off != 0:
                shifted = pltpu.roll(qk, (-off) % n_ext, axis=0)
                acc = acc + shifted[halo:halo + tm] * cw_ref[j:j + 1, cols]
        act = acc * jax.nn.sigmoid(acc)
        if blk * CONV_COLS < M_WIDTH:
            qa_ref[:, cols] = act.astype(BF16)
        else:
            kcols = slice(blk * CONV_COLS - M_WIDTH, (blk + 1) * CONV_COLS - M_WIDTH)
            ka_ref[:, kcols] = (act * (M_HEAD_DIM ** -0.5)).astype(BF16)

    vm_ref[...] = y_vm.astype(BF16)
    om_ref[...] = y_om.astype(BF16)
    gr_ref[...] = y_g

    n_col = B_WIDTH // LANES
    for s, tok_ref in enumerate(tok_refs):
        y = ys[s] * (HEAD_DIM ** -0.5) if s == 0 else ys[s]
        tok_ref[...] = y.astype(BF16)
        for c in range(n_col):
            stage_ref[s, c] = y[:, c * LANES:(c + 1) * LANES]
        for d, refs in res_refs.items():
            for r in range(d):
                for c in range(n_col):
                    lo = (c * d + r) * LANES
                    refs[s][:, lo:lo + LANES] = stage_ref[s, c, pl.ds(r, tm // d, stride=d), :].astype(BF16)


def _proj_ab(x, g, w_qk, w_rest, wg_t, conv_w, conv_b, seq_len):
    T, D = x.shape
    tm = ROW_TILE
    halo = SUBLANES
    hb = tm // halo
    row = lambda w: pl.BlockSpec((tm, w), lambda i: (i, 0))
    res = lambda d: pl.BlockSpec((tm // d, d * B_WIDTH), lambda i: (i, 0))
    full = lambda a: _resident(a.shape, lambda i: (0, 0))
    conv_b = conv_b.reshape(1, -1)
    g = g.reshape(1, D)
    out_specs = [row(M_WIDTH)] * 4 + [pl.BlockSpec((wg_t.shape[0], tm), lambda i: (0, i))] + [row(B_WIDTH)] * 3
    out_shape = ([jax.ShapeDtypeStruct((T, M_WIDTH), BF16)] * 4
                 + [jax.ShapeDtypeStruct((wg_t.shape[0], T), F32)]
                 + [jax.ShapeDtypeStruct((T, B_WIDTH), BF16)] * 3)
    for d in DILATIONS[1:]:
        out_specs += [res(d)] * 3
        out_shape += [jax.ShapeDtypeStruct((T // d, d * B_WIDTH), BF16)] * 3
    return pl.pallas_call(
        functools.partial(_proj_ab_kernel, tiles_per_seq=seq_len // tm),
        grid=(T // tm,),
        in_specs=[row(D),
                  pl.BlockSpec((halo, D), lambda i: (jnp.maximum(i * hb - 1, 0), 0)),
                  pl.BlockSpec((halo, D), lambda i: (jnp.minimum((i + 1) * hb, T // halo - 1), 0)),
                  full(g), full(w_qk), full(w_rest), full(wg_t), full(conv_w), full(conv_b)],
        out_specs=out_specs, out_shape=out_shape,
        scratch_shapes=[pltpu.VMEM((3, B_WIDTH // LANES, tm, LANES), F32)],
        compiler_params=_cparams("parallel"), name="proj_ab")(x, x, x, g, w_qk, w_rest, wg_t, conv_w, conv_b)


def _attn_kernel(*refs, groups, kv_tiles, q_tile, q_block, k_window, radius, dist_unit, seq_len, has_sink,
                 with_lse):
    slope_ref = refs[0]
    pos = 1
    sink_ref = None
    if has_sink:
        sink_ref = refs[pos]
        pos += 1
    q_ref, k_ref, v_ref, o_ref = refs[pos:pos + 4]
    pos += 4
    lse_ref = None
    if with_lse:
        lse_ref = refs[pos]
        pos += 1
    bias_ref = refs[pos]
    pair = pl.program_id(0)
    tile = pl.program_id(3)

    shift_step = min(q_block, radius)

    @pl.when((pl.program_id(1) == 0) & (pl.program_id(2) == 0) & (tile == 0))
    def _():
        sub = lax.broadcasted_iota(jnp.int32, (q_block, k_window), 0)
        ln = lax.broadcasted_iota(jnp.int32, (q_block, k_window), 1)
        for variant in range(2 * radius // shift_step + 1):
            adist = jnp.abs(ln - sub - variant * shift_step)
            negd = jnp.where(adist <= radius, -(dist_unit * adist).astype(F32), -jnp.inf)
            for g in range(groups):
                for j in range(2):
                    bias_ref[variant, 2 * g + j] = slope_ref[(2 * pair + j) * groups + g] * negd

    lane = lax.broadcasted_iota(jnp.int32, (q_block, LANES), 1)
    low = lane < HEAD_DIM
    ones = jnp.ones((k_window, LANES), BF16)
    chains = []
    for blk in range(q_tile // q_block):
        q0 = tile * q_tile + blk * q_block
        ks = pl.multiple_of(jnp.clip(q0 - radius, 0, seq_len - k_window), HEAD_DIM)
        variant = (q0 - ks) // shift_step
        rows = slice(blk * q_block, (blk + 1) * q_block)
        for t in range(kv_tiles):
            kt = k_ref[pl.ds(ks, k_window), t * LANES:(t + 1) * LANES]
            vt = jnp.concatenate([v_ref[pl.ds(ks, k_window), t * LANES:(t + 1) * LANES], ones], axis=1)
            for g in range(groups):
                qt = q_ref[rows, (t * groups + g) * LANES:(t * groups + g + 1) * LANES]
                for j in range(2):
                    qm = jnp.where(low if j == 0 else jnp.logical_not(low), qt, jnp.zeros_like(qt))
                    chains.append(dict(qm=qm, kt=kt, vt=vt, variant=variant, slot=2 * g + j,
                                       head=(2 * pair + j) * groups + g))
    s = [lax.dot_general(c["qm"], c["kt"], (((1,), (1,)), ((), ())), preferred_element_type=F32)
         + bias_ref[c["variant"], c["slot"]] for c in chains]
    m = [jnp.max(x, axis=-1, keepdims=True) for x in s]
    if has_sink:
        m = [jnp.maximum(x, sink_ref[c["head"]]) for c, x in zip(chains, m)]
    p = [jnp.exp(x - y).astype(BF16) for x, y in zip(s, m)]
    od = [jnp.dot(x, c["vt"], preferred_element_type=F32) for c, x in zip(chains, p)]
    den = [x[:, LANES:] for x in od]
    if has_sink:
        den = [x + jnp.exp(sink_ref[c["head"]] - y) for c, x, y in zip(chains, den, m)]
    o = [x[:, :LANES] / y for x, y in zip(od, den)]
    lse = [x + jnp.log(y) for x, y in zip(m, den)] if with_lse else None
    for n in range(0, len(chains), 2):
        blk, tile_col = divmod(n // 2, kv_tiles * groups)
        rows = slice(blk * q_block, (blk + 1) * q_block)
        cols = slice(tile_col * LANES, (tile_col + 1) * LANES)
        o_ref[rows, cols] = jnp.where(low, o[n], o[n + 1]).astype(o_ref.dtype)
        if with_lse:
            lse_ref[rows, cols] = jnp.where(low, lse[n], lse[n + 1])


def _banded_attention(qv, kv, vv, slopes, sink, *, dilation, groups, radius, with_lse, out_dtype, q_block, chains):
    Bn, Lv, _ = qv.shape
    d = dilation
    pairs = kv.shape[2] // (d * LANES)
    q_tile = min(chains // (2 * groups) * q_block, Lv)
    kv_tiles = min(d, max(1, chains // (2 * groups * (q_tile // q_block))))
    k_window = q_block + 2 * radius
    shift_step = min(q_block, radius)
    assert max(q_block, radius) % shift_step == 0 and Lv % q_tile == 0 and k_window <= Lv
    assert d % kv_tiles == 0
    rsteps = d // kv_tiles
    smem = pl.BlockSpec(memory_space=pltpu.SMEM)
    qspec = pl.BlockSpec((None, q_tile, kv_tiles * groups * LANES), lambda p, b, r, i: (b, i, p * rsteps + r))
    kspec = pl.BlockSpec((None, Lv, kv_tiles * LANES), lambda p, b, r, i: (b, 0, p * rsteps + r))
    in_specs = [smem] + ([smem] if sink is not None else []) + [qspec, kspec, kspec]
    args = [slopes] + ([sink] if sink is not None else []) + [qv, kv, vv]
    out_specs = [qspec]
    out_shape = [jax.ShapeDtypeStruct(qv.shape, out_dtype)]
    if with_lse:
        out_specs.append(qspec)
        out_shape.append(jax.ShapeDtypeStruct(qv.shape, F32))
    return pl.pallas_call(
        functools.partial(_attn_kernel, groups=groups, kv_tiles=kv_tiles, q_tile=q_tile, q_block=q_block,
                          k_window=k_window,
                          radius=radius, dist_unit=d, seq_len=Lv, has_sink=sink is not None, with_lse=with_lse),
        grid=(pairs, Bn, rsteps, Lv // q_tile), in_specs=in_specs, out_specs=out_specs, out_shape=out_shape,
        scratch_shapes=[pltpu.VMEM((2 * radius // shift_step + 1, 2 * groups, q_block, k_window), F32)],
        compiler_params=_cparams("arbitrary", "arbitrary", "arbitrary", "arbitrary"),
        name=f"banded_attention_d{d}")(*args)


def _alibi_slopes(n):
    return jnp.exp2(-8.0 * jnp.arange(1, n + 1, dtype=F32) / n)


def _log_sigmoid(x):
    return jnp.minimum(x, 0.0) - jnp.log1p(jnp.exp(-jnp.abs(x)))


def _mlstm_chunks(chains, seen, diag):
    nt = (((1,), (1,)), ((), ()))
    E = chains[0]["q"].shape[1]
    mask = [seen[c["rev"]] for c in chains]
    b_c = [jnp.sum(jnp.where(mk, c["lf_r"], 0.0), axis=1, keepdims=True) for c, mk in zip(chains, mask)]
    qk = [lax.dot_general(c["q"], c["k"], nt, preferred_element_type=F32) for c in chains]
    inter = [jnp.dot(c["q"], c["ct"].astype(BF16), preferred_element_type=F32) for c in chains]
    k_t = [c["k"].astype(F32).T for c in chains]
    b_last = [jnp.sum(c["lf_r"], axis=1, keepdims=True) for c in chains]
    b_r = [jnp.sum(jnp.where(diag, x, 0.0), axis=0, keepdims=True) for x in b_c]
    u_r = [c["i_r"] - x for c, x in zip(chains, b_r)]
    um = [jnp.where(mk, x, -jnp.inf) for mk, x in zip(mask, u_r)]
    g_t = [jnp.maximum(c["m"], jnp.max(x, axis=1, keepdims=True)) for c, x in zip(chains, um)]
    a_r = [x + y for x, y in zip(b_last, u_r)]
    m_new = [jnp.maximum(x + c["m"], jnp.max(y, axis=1, keepdims=True)) for c, x, y in zip(chains, b_last, a_r)]
    kw_t = [(x * jnp.exp(y - z)).astype(BF16) for x, y, z in zip(k_t, a_r, m_new)]
    ct = [jnp.exp(x + c["m"] - z) * c["ct"] + jnp.dot(y, c["v1"], preferred_element_type=F32)
          for c, x, y, z in zip(chains, b_last, kw_t, m_new)]
    w = [(jnp.exp(x - y) * z).astype(BF16) for x, y, z in zip(um, g_t, qk)]
    tot = [jnp.dot(x, c["v1"], preferred_element_type=F32) + jnp.exp(c["m"] - y) * z
           for c, x, y, z in zip(chains, w, g_t, inter)]
    h = [x[:, :E] / jnp.maximum(jnp.abs(x[:, E:]), jnp.exp(-(y + z))) for x, y, z in zip(tot, b_c, g_t)]
    return list(zip(h, ct, m_new))


def _mlstm_kernel(q_ref, k_ref, v_ref, gr_ref, gb_ref, hg_ref, o_ref, acc_ref, *, seq_len, chunk, heads):
    S, L, E = seq_len, chunk, M_HEAD_DIM
    n_chunks = S // L
    row = lax.broadcasted_iota(jnp.int32, (L, L), 0)
    col = lax.broadcasted_iota(jnp.int32, (L, L), 1)
    seen = (col <= row, col >= row)
    diag = col == row
    ones = jnp.ones((L, E), BF16)

    def finish(tot, lanes):
        return tot * lax.rsqrt(jnp.mean(tot * tot, axis=-1, keepdims=True) + EPS) * hg_ref[:, lanes]

    def sweep(second):
        def body(it, states):
            chains, where = [], []
            for n, (ct, m) in enumerate(states):
                hh, rev = divmod(n, 2)
                c = it if rev == 0 else n_chunks - 1 - it
                rows = pl.ds(pl.multiple_of(c * L, L), L)
                lanes = slice(hh * E, (hh + 1) * E)
                g = gr_ref[4 * hh:4 * hh + 4, rows] + gb_ref[4 * hh:4 * hh + 4, :]
                where.append((rows, lanes, acc_ref[rows, lanes] if second else None))
                chains.append(dict(q=q_ref[rows, lanes], k=k_ref[rows, lanes],
                                   v1=jnp.concatenate([v_ref[rows, lanes], ones], axis=1),
                                   i_r=g[rev:rev + 1], lf_r=_log_sigmoid(g[2 + rev:3 + rev]), ct=ct, m=m, rev=rev))
            out = _mlstm_chunks(chains, seen, diag)
            for (rows, lanes, prev), (h, _, _) in zip(where, out):
                if second:
                    o_ref[rows, lanes] = finish(prev + h, lanes).astype(o_ref.dtype)
                else:
                    acc_ref[rows, lanes] = h
            return tuple((ct, m) for _, ct, m in out)
        return body

    zero = (jnp.zeros((E, 2 * E), F32), jnp.zeros((1, 1), F32))
    states = lax.fori_loop(0, n_chunks // 2, sweep(False), (zero,) * (2 * heads))
    lax.fori_loop(n_chunks // 2, n_chunks, sweep(True), states)


def _mlstm(q, k, v, g_rows, gate_b, hnorm_g, Bn, S):
    E = M_HEAD_DIM
    T = Bn * S
    hs = MLSTM_HEADS_PER_STEP
    groups = M_HEADS // hs
    seq = pl.BlockSpec((None, S, hs * E), lambda b, h: (b, 0, h))
    q3, k3, v3 = (t.reshape(Bn, S, M_WIDTH) for t in (q, k, v))
    out = pl.pallas_call(
        functools.partial(_mlstm_kernel, seq_len=S, chunk=MLSTM_CHUNK, heads=hs),
        grid=(Bn, groups),
        in_specs=[seq, seq, seq,
                  pl.BlockSpec((None, 4 * hs, S), lambda b, h: (h, 0, b)),
                  pl.BlockSpec((None, 4 * hs, 1), lambda b, h: (h, 0, 0)),
                  pl.BlockSpec((1, hs * E), lambda b, h: (0, h))],
        out_specs=seq,
        out_shape=jax.ShapeDtypeStruct((Bn, S, M_WIDTH), BF16),
        scratch_shapes=[pltpu.VMEM((S, hs * E), F32)],
        compiler_params=_cparams("parallel", "parallel"), name="mlstm")(
            q3, k3, v3, g_rows.reshape(groups, 4 * hs, T), gate_b.reshape(groups, 4 * hs, 1),
            hnorm_g.reshape(1, M_WIDTH))
    return out.reshape(T, M_WIDTH)


def _mixer_ab(x, g, w_in, conv_w, conv_b, gate_b, hnorm_g, w_out, Bn, S):
    T = Bn * S
    MW, BW = M_WIDTH, B_WIDTH
    g0 = 4 * MW
    n_gate = 4 * M_HEADS
    w_qk = w_in[:, :2 * MW].astype(BF16)
    w_rest = jnp.concatenate([w_in[:, 2 * MW:g0], w_in[:, g0 + n_gate:]], axis=1).astype(BF16)
    wg_t = w_in[:, g0:g0 + n_gate].reshape(-1, 4, M_HEADS).transpose(2, 1, 0).reshape(n_gate, -1).astype(BF16)
    gate_b = gate_b.reshape(4, M_HEADS).T.astype(F32)
    (q_m, k_m, v_m, o_m, g_rows, q1, k1, v1, q4, k4, v4, q16, k16, v16) = _proj_ab(
        x, g, w_qk, w_rest, wg_t, conv_w, conv_b, S)

    hn = _mlstm(q_m, k_m, v_m, g_rows, gate_b, hnorm_g, Bn, S)

    slopes = _alibi_slopes(B_HEADS)
    outs, lses = [], []
    for d, qkv in zip(DILATIONS, ((q1, k1, v1), (q4, k4, v4), (q16, k16, v16))):
        qv, kv, vv = (t.reshape(Bn, S // d, d * BW) for t in qkv)
        o, lse = _banded_attention(qv, kv, vv, slopes, None, dilation=d, groups=1, radius=B_RADIUS,
                                   with_lse=True, out_dtype=BF16, q_block=B_Q_BLOCK, chains=B_CHAINS)
        outs.append(o.reshape(T // d, d * BW))
        lses.append(lse.reshape(T // d, d * BW))
    w_out = w_out.astype(BF16)
    return ("ab", hn, o_m, outs, lses, w_out[:MW], w_out[MW:])


def _c_head_order():
    order = []
    for pair in range(C_KV_HEADS // 2):
        for slot in range(C_GROUP):
            for j in range(2):
                order.append((2 * pair + j) * C_GROUP + slot)
    return jnp.array(order, dtype=jnp.int32)


def _mixer_c(x, g, w_in, sink, w_out, Bn, S):
    T, D = x.shape
    QW = C_HEADS * HEAD_DIM
    KW = C_KV_HEADS * HEAD_DIM
    order = _c_head_order()
    wq = w_in[:, :QW].reshape(D, C_HEADS, HEAD_DIM)[:, order].reshape(D, QW)
    w_main = jnp.concatenate([wq, w_in[:, QW:]], axis=1).astype(BF16)
    w_out = w_out.reshape(C_HEADS, HEAD_DIM, D)[order].reshape(QW, D).astype(BF16)
    segs = [(QW, BF16, HEAD_DIM ** -0.5), (KW, BF16, 1.0), (KW, BF16, 1.0)]
    q, k, v = _norm_proj(x, g, w_main, segs)
    (o,) = _banded_attention(q.reshape(Bn, S, QW), k.reshape(Bn, S, KW), v.reshape(Bn, S, KW),
                             _alibi_slopes(C_HEADS), sink.astype(F32), dilation=1, groups=C_GROUP,
                             radius=C_RADIUS, with_lse=False, out_dtype=BF16, q_block=C_Q_BLOCK, chains=C_CHAINS)
    return ("c", o.reshape(T, QW), w_out)


def kernel(x, norm_g, ffn_w1, ffn_w3, ffn_w2, ab_w_in, ab_conv_w, ab_conv_b, ab_gate_b, ab_hnorm_g, ab_w_out,
           c_w_in, c_sink, c_w_out, final_g):
    Bn, S, D = x.shape
    depth = norm_g.shape[0]
    w1, w3, w2 = (w.astype(BF16) for w in (ffn_w1, ffn_w3, ffn_w2))
    x = x.reshape(Bn * S, D)
    for l in range(depth):
        j = l // 2
        x = _ffn(x, norm_g[l, 0], w1, w3, w2, (l, 0))
        if l % 2 == 0:
            mixed = _mixer_ab(x, norm_g[l, 1], ab_w_in[j], ab_conv_w[j], ab_conv_b[j], ab_gate_b[j], ab_hnorm_g[j],
                              ab_w_out[j], Bn, S)
        else:
            mixed = _mixer_c(x, norm_g[l, 1], c_w_in[j], c_sink[j], c_w_out[j], Bn, S)
        x = _ffn(x, norm_g[l, 2], w1, w3, w2, (l, 1), final_g if l == depth - 1 else None, mixed)
    return x.reshape(Bn, S, D)
```

```python
import functools

import jax
import jax.numpy as jnp
from jax import lax
from jax.experimental import pallas as pl
from jax.experimental.pallas import tpu as pltpu

F32 = jnp.float32
BF16 = jnp.bfloat16

EPS = 1e-6
M_HEADS = 4
M_HEAD_DIM = 128
M_WIDTH = M_HEADS * M_HEAD_DIM
M_CONV = 5
B_HEADS = 8
B_WIDTH = 512
DILATIONS = (1, 4, 16)
B_RADIUS = 64
C_HEADS = 16
C_KV_HEADS = 4
C_GROUP = C_HEADS // C_KV_HEADS
C_RADIUS = 128
HEAD_DIM = 64
LANES = 128
SUBLANES = 8
LOG2E = 1.4426950408889634
LN2 = 0.6931471805599453
ATTN_Q_SCALE = HEAD_DIM ** -0.5 * LOG2E

VMEM_LIMIT_BYTES = 56 * 1024 * 1024
ROW_TILE = 512
B_Q_BLOCK, B_CHAINS = 128, 64
C_Q_BLOCK, C_CHAINS = 128, 64
MLSTM_CHUNK = 256
CONV_COLS = 256
MLSTM_HEADS_PER_STEP = 2


def _cparams(*sem):
    return pltpu.CompilerParams(dimension_semantics=sem, vmem_limit_bytes=VMEM_LIMIT_BYTES)


def _resident(shape, index_map):
    return pl.BlockSpec(shape, index_map, pipeline_mode=pl.Buffered(1))


def _rms(x, g):
    return x * lax.rsqrt(jnp.mean(x * x, axis=-1, keepdims=True) + EPS) * g


def _mix_c(a_ref, w_ref):
    return jnp.dot(a_ref[...], w_ref[...], preferred_element_type=F32)


def _mix_ab(hn_ref, om_ref, o1_ref, l1_ref, o4_ref, l4_ref, o16_ref, l16_ref, wa_ref, wb_ref,
            so4_ref, sl4_ref, so16_ref, sl16_ref):
    tm = hn_ref.shape[0]
    n_col = B_WIDTH // LANES
    for d, pairs in ((DILATIONS[1], ((o4_ref, so4_ref), (l4_ref, sl4_ref))),
                     (DILATIONS[2], ((o16_ref, so16_ref), (l16_ref, sl16_ref)))):
        for src_ref, dst_ref in pairs:
            for r in range(d):
                for c in range(n_col):
                    lo = (c * d + r) * LANES
                    dst_ref[c, pl.ds(r, tm // d, stride=d), :] = src_ref[:, lo:lo + LANES].astype(F32)
    cols = []
    for c in range(n_col):
        lanes = slice(c * LANES, (c + 1) * LANES)
        l1, l2, l3 = l1_ref[:, lanes], sl4_ref[c], sl16_ref[c]
        lm = jnp.maximum(jnp.maximum(l1, l2), l3)
        e1, e2, e3 = jnp.exp(l1 - lm), jnp.exp(l2 - lm), jnp.exp(l3 - lm)
        ob = (e1 * o1_ref[:, lanes].astype(F32) + e2 * so4_ref[c] + e3 * so16_ref[c]) / (e1 + e2 + e3)
        cols.append(ob.astype(BF16))
    ob = jnp.concatenate(cols, axis=1)
    ma = (hn_ref[...].astype(F32) * jax.nn.sigmoid(om_ref[...].astype(F32))).astype(BF16)
    return (jnp.dot(ma, wa_ref[...], preferred_element_type=F32)
            + jnp.dot(ob, wb_ref[...], preferred_element_type=F32))


def _ffn_kernel(x_ref, g_ref, w1_ref, w3_ref, w2_ref, *rest, final_norm, mix, n_mix_in, n_scratch):
    mix_in = rest[:n_mix_in]
    rest = rest[n_mix_in:]
    scratch = rest[len(rest) - n_scratch:]
    o_ref = rest[len(rest) - n_scratch - 1]
    x = x_ref[...]
    if mix is not None:
        x = x + mix(*mix_in, *scratch)
    h = _rms(x, g_ref[...]).astype(BF16)
    a = jnp.dot(h, w1_ref[...], preferred_element_type=F32)
    b = jnp.dot(h, w3_ref[...], preferred_element_type=F32)
    act = (a * jax.nn.sigmoid(a) * b).astype(BF16)
    y = x + 0.5 * jnp.dot(act, w2_ref[...], preferred_element_type=F32)
    if final_norm:
        y = _rms(y, rest[0][...])
    o_ref[...] = y


def _ffn(x, g, w1, w3, w2, which, final_g=None, mixer=None):
    T, D = x.shape
    tm = ROW_TILE
    layer, half = which
    stacked = lambda w: pl.BlockSpec((None, None) + w.shape[2:], lambda i: (layer, half, 0, 0),
                                     pipeline_mode=pl.Buffered(1))
    row = lambda w: pl.BlockSpec((tm, w), lambda i: (i, 0))
    res = lambda d: pl.BlockSpec((tm // d, d * B_WIDTH), lambda i: (i, 0))
    full = lambda a: _resident(a.shape, lambda i: (0, 0))
    g = g.reshape(1, D)
    in_specs = [row(D), full(g), stacked(w1), stacked(w3), stacked(w2)]
    args = [x, g, w1, w3, w2]
    mix, n_mix_in, scratch = None, 0, []
    if mixer is not None and mixer[0] == "c":
        _, att, w_out = mixer
        mix, mix_args, mix_specs = _mix_c, [att, w_out], [row(att.shape[1]), full(w_out)]
    elif mixer is not None:
        _, hn, om, outs, lses, wa, wb = mixer
        d4, d16 = DILATIONS[1:]
        mix = _mix_ab
        mix_args = [hn, om, outs[0], lses[0], outs[1], lses[1], outs[2], lses[2], wa, wb]
        mix_specs = [row(M_WIDTH)] * 4 + [res(d4)] * 2 + [res(d16)] * 2 + [full(wa), full(wb)]
        scratch = [pltpu.VMEM((B_WIDTH // LANES, tm, LANES), F32)] * 4
    if mix is not None:
        n_mix_in = len(mix_args)
        in_specs += mix_specs
        args += mix_args
    if final_g is not None:
        final_g = final_g.reshape(1, D)
        in_specs.append(full(final_g))
        args.append(final_g)
    return pl.pallas_call(
        functools.partial(_ffn_kernel, final_norm=final_g is not None, mix=mix, n_mix_in=n_mix_in,
                          n_scratch=len(scratch)),
        grid=(T // tm,), in_specs=in_specs, out_specs=row(D),
        out_shape=jax.ShapeDtypeStruct((T, D), F32), scratch_shapes=scratch,
        compiler_params=_cparams("parallel"), name="ffn")(*args)


def _proj_kernel(x_ref, g_ref, w_ref, *outs, segs):
    h = _rms(x_ref[...], g_ref[...]).astype(BF16)
    off = 0
    for (width, scale), o_ref in zip(segs, outs):
        y = jnp.dot(h, w_ref[:, off:off + width], preferred_element_type=F32)
        if scale != 1.0:
            y = y * scale
        o_ref[...] = y.astype(o_ref.dtype)
        off += width


def _norm_proj(x, g, w, segs):
    T, D = x.shape
    N = w.shape[1]
    tm = ROW_TILE
    return pl.pallas_call(
        functools.partial(_proj_kernel, segs=tuple((wd, sc) for wd, _, sc in segs)),
        grid=(T // tm,),
        in_specs=[pl.BlockSpec((tm, D), lambda i: (i, 0)), _resident((1, D), lambda i: (0, 0)),
                  _resident((D, N), lambda i: (0, 0))],
        out_specs=[pl.BlockSpec((tm, wd), lambda i: (i, 0)) for wd, _, _ in segs],
        out_shape=[jax.ShapeDtypeStruct((T, wd), dt) for wd, dt, _ in segs],
        compiler_params=_cparams("parallel"), name="norm_proj")(x, g.reshape(1, D), w)


def _proj_ab_kernel(x_ref, xp_ref, xn_ref, g_ref, wqk_ref, wr_ref, wg_ref, cw_ref, cb_ref,
                    qa_ref, ka_ref, vm_ref, om_ref, gr_ref, *rest, tiles_per_seq):
    tok_refs = rest[0:3]
    res_refs = {d: rest[3 * (n + 1):3 * (n + 2)] for n, d in enumerate(DILATIONS[1:])}
    stage_ref = rest[-1]
    tm = x_ref.shape[0]
    halo = SUBLANES
    g = g_ref[...]
    i = pl.program_id(0)
    first = i % tiles_per_seq == 0
    last = i % tiles_per_seq == tiles_per_seq - 1

    x = x_ref[...]
    h = _rms(x, g).astype(BF16)
    x_ext = jnp.concatenate([xp_ref[...], x, xn_ref[...]], axis=0)
    h_ext = _rms(x_ext, g).astype(BF16)
    n_ext = tm + 2 * halo
    row = lax.broadcasted_iota(jnp.int32, (n_ext, 1), 0)
    outside = ((row < halo) & first) | ((row >= halo + tm) & last)
    n_conv = 2 * M_WIDTH // CONV_COLS
    qk_blocks = [jnp.dot(h_ext, wqk_ref[:, blk * CONV_COLS:(blk + 1) * CONV_COLS], preferred_element_type=F32)
                 for blk in range(n_conv)]
    y_vm = jnp.dot(h, wr_ref[:, 0:M_WIDTH], preferred_element_type=F32)
    y_om = jnp.dot(h, wr_ref[:, M_WIDTH:2 * M_WIDTH], preferred_element_type=F32)
    y_g = lax.dot_general(wg_ref[...], h, (((1,), (1,)), ((), ())), preferred_element_type=F32)
    ys = [jnp.dot(h, wr_ref[:, 2 * M_WIDTH + s * B_WIDTH:2 * M_WIDTH + (s + 1) * B_WIDTH],
                  preferred_element_type=F32) for s in range(len(tok_refs))]

    for blk in range(n_conv):
        cols = slice(blk * CONV_COLS, (blk + 1) * CONV_COLS)
        qk = jnp.where(outside, 0.0, qk_blocks[blk])
        acc = cb_ref[:, cols] + qk[halo:halo + tm] * cw_ref[M_CONV // 2:M_CONV // 2 + 1, cols]
        for j in range(M_CONV):
            off = j - M_CONV // 2
            if off != 0:
                shifted = pltpu.roll(qk, (-off) % n_ext, axis=0)
                acc = acc + shifted[halo:halo + tm] * cw_ref[j:j + 1, cols]
        act = acc * jax.nn.sigmoid(acc)
        if blk * CONV_COLS < M_WIDTH:
            qa_ref[:, cols] = act.astype(BF16)
        else:
            kcols = slice(blk * CONV_COLS - M_WIDTH, (blk + 1) * CONV_COLS - M_WIDTH)
            ka_ref[:, kcols] = (act * (M_HEAD_DIM ** -0.5)).astype(BF16)

    vm_ref[...] = y_vm.astype(BF16)
    om_ref[...] = y_om.astype(BF16)
    gr_ref[...] = y_g

    n_col = B_WIDTH // LANES
    for s, tok_ref in enumerate(tok_refs):
        y = ys[s] * ATTN_Q_SCALE if s == 0 else ys[s]
        tok_ref[...] = y.astype(BF16)
        for c in range(n_col):
            stage_ref[s, c] = y[:, c * LANES:(c + 1) * LANES]
        for d, refs in res_refs.items():
            for r in range(d):
                for c in range(n_col):
                    lo = (c * d + r) * LANES
                    refs[s][:, lo:lo + LANES] = stage_ref[s, c, pl.ds(r, tm // d, stride=d), :].astype(BF16)


def _proj_ab(x, g, w_qk, w_rest, wg_t, conv_w, conv_b, seq_len):
    T, D = x.shape
    tm = ROW_TILE
    halo = SUBLANES
    hb = tm // halo
    row = lambda w: pl.BlockSpec((tm, w), lambda i: (i, 0))
    res = lambda d: pl.BlockSpec((tm // d, d * B_WIDTH), lambda i: (i, 0))
    full = lambda a: _resident(a.shape, lambda i: (0, 0))
    conv_b = conv_b.reshape(1, -1)
    g = g.reshape(1, D)
    out_specs = [row(M_WIDTH)] * 4 + [pl.BlockSpec((wg_t.shape[0], tm), lambda i: (0, i))] + [row(B_WIDTH)] * 3
    out_shape = ([jax.ShapeDtypeStruct((T, M_WIDTH), BF16)] * 4
                 + [jax.ShapeDtypeStruct((wg_t.shape[0], T), F32)]
                 + [jax.ShapeDtypeStruct((T, B_WIDTH), BF16)] * 3)
    for d in DILATIONS[1:]:
        out_specs += [res(d)] * 3
        out_shape += [jax.ShapeDtypeStruct((T // d, d * B_WIDTH), BF16)] * 3
    return pl.pallas_call(
        functools.partial(_proj_ab_kernel, tiles_per_seq=seq_len // tm),
        grid=(T // tm,),
        in_specs=[row(D),
                  pl.BlockSpec((halo, D), lambda i: (jnp.maximum(i * hb - 1, 0), 0)),
                  pl.BlockSpec((halo, D), lambda i: (jnp.minimum((i + 1) * hb, T // halo - 1), 0)),
                  full(g), full(w_qk), full(w_rest), full(wg_t), full(conv_w), full(conv_b)],
        out_specs=out_specs, out_shape=out_shape,
        scratch_shapes=[pltpu.VMEM((3, B_WIDTH // LANES, tm, LANES), F32)],
        compiler_params=_cparams("parallel"), name="proj_ab")(x, x, x, g, w_qk, w_rest, wg_t, conv_w, conv_b)


def _attn_kernel(*refs, groups, kv_tiles, q_tile, q_block, k_window, radius, dist_unit, seq_len, has_sink,
                 with_lse):
    slope_ref = refs[0]
    pos = 1
    sink_ref = None
    if has_sink:
        sink_ref = refs[pos]
        pos += 1
    q_ref, k_ref, v_ref, o_ref = refs[pos:pos + 4]
    pos += 4
    lse_ref = None
    if with_lse:
        lse_ref = refs[pos]
        pos += 1
    bias_ref = refs[pos]
    pair = pl.program_id(0)
    tile = pl.program_id(3)

    shift_step = min(q_block, radius)

    @pl.when((pl.program_id(1) == 0) & (pl.program_id(2) == 0) & (tile == 0))
    def _():
        sub = lax.broadcasted_iota(jnp.int32, (q_block, k_window), 0)
        ln = lax.broadcasted_iota(jnp.int32, (q_block, k_window), 1)
        for variant in range(2 * radius // shift_step + 1):
            adist = jnp.abs(ln - sub - variant * shift_step)
            negd = jnp.where(adist <= radius, -(dist_unit * adist).astype(F32), -jnp.inf)
            for g in range(groups):
                for j in range(2):
                    bias_ref[variant, 2 * g + j] = (slope_ref[(2 * pair + j) * groups + g] * LOG2E) * negd

    lane = lax.broadcasted_iota(jnp.int32, (q_block, LANES), 1)
    low = lane < HEAD_DIM
    ones = jnp.ones((k_window, LANES), BF16)
    chains = []
    for blk in range(q_tile // q_block):
        q0 = tile * q_tile + blk * q_block
        ks = pl.multiple_of(jnp.clip(q0 - radius, 0, seq_len - k_window), HEAD_DIM)
        variant = (q0 - ks) // shift_step
        rows = slice(blk * q_block, (blk + 1) * q_block)
        for t in range(kv_tiles):
            kt = k_ref[pl.ds(ks, k_window), t * LANES:(t + 1) * LANES]
            vt = jnp.concatenate([v_ref[pl.ds(ks, k_window), t * LANES:(t + 1) * LANES], ones], axis=1)
            for g in range(groups):
                qt = q_ref[rows, (t * groups + g) * LANES:(t * groups + g + 1) * LANES]
                for j in range(2):
                    qm = jnp.where(low if j == 0 else jnp.logical_not(low), qt, jnp.zeros_like(qt))
                    chains.append(dict(qm=qm, kt=kt, vt=vt, variant=variant, slot=2 * g + j,
                                       head=(2 * pair + j) * groups + g))
    s = [lax.dot_general(c["qm"], c["kt"], (((1,), (1,)), ((), ())), preferred_element_type=F32)
         + bias_ref[c["variant"], c["slot"]] for c in chains]
    m = [jnp.max(x, axis=-1, keepdims=True) for x in s]
    if has_sink:
        m = [jnp.maximum(x, sink_ref[c["head"]] * LOG2E) for c, x in zip(chains, m)]
    p = [jnp.exp2(x - y).astype(BF16) for x, y in zip(s, m)]
    od = [jnp.dot(x, c["vt"], preferred_element_type=F32) for c, x in zip(chains, p)]
    den = [x[:, LANES:] for x in od]
    if has_sink:
        den = [x + jnp.exp2(sink_ref[c["head"]] * LOG2E - y) for c, x, y in zip(chains, den, m)]
    o = [x[:, :LANES] / y for x, y in zip(od, den)]
    lse = [(x + jnp.log2(y)) * LN2 for x, y in zip(m, den)] if with_lse else None
    for n in range(0, len(chains), 2):
        blk, tile_col = divmod(n // 2, kv_tiles * groups)
        rows = slice(blk * q_block, (blk + 1) * q_block)
        cols = slice(tile_col * LANES, (tile_col + 1) * LANES)
        o_ref[rows, cols] = jnp.where(low, o[n], o[n + 1]).astype(o_ref.dtype)
        if with_lse:
            lse_ref[rows, cols] = jnp.where(low, lse[n], lse[n + 1])


def _banded_attention(qv, kv, vv, slopes, sink, *, dilation, groups, radius, with_lse, out_dtype, q_block, chains):
    Bn, Lv, _ = qv.shape
    d = dilation
    pairs = kv.shape[2] // (d * LANES)
    q_tile = min(chains // (2 * groups) * q_block, Lv)
    kv_tiles = min(d, max(1, chains // (2 * groups * (q_tile // q_block))))
    k_window = q_block + 2 * radius
    shift_step = min(q_block, radius)
    assert max(q_block, radius) % shift_step == 0 and Lv % q_tile == 0 and k_window <= Lv
    assert d % kv_tiles == 0
    rsteps = d // kv_tiles
    smem = pl.BlockSpec(memory_space=pltpu.SMEM)
    qspec = pl.BlockSpec((None, q_tile, kv_tiles * groups * LANES), lambda p, b, r, i: (b, i, p * rsteps + r))
    kspec = pl.BlockSpec((None, Lv, kv_tiles * LANES), lambda p, b, r, i: (b, 0, p * rsteps + r))
    in_specs = [smem] + ([smem] if sink is not None else []) + [qspec, kspec, kspec]
    args = [slopes] + ([sink] if sink is not None else []) + [qv, kv, vv]
    out_specs = [qspec]
    out_shape = [jax.ShapeDtypeStruct(qv.shape, out_dtype)]
    if with_lse:
        out_specs.append(qspec)
        out_shape.append(jax.ShapeDtypeStruct(qv.shape, F32))
    return pl.pallas_call(
        functools.partial(_attn_kernel, groups=groups, kv_tiles=kv_tiles, q_tile=q_tile, q_block=q_block,
                          k_window=k_window,
                          radius=radius, dist_unit=d, seq_len=Lv, has_sink=sink is not None, with_lse=with_lse),
        grid=(pairs, Bn, rsteps, Lv // q_tile), in_specs=in_specs, out_specs=out_specs, out_shape=out_shape,
        scratch_shapes=[pltpu.VMEM((2 * radius // shift_step + 1, 2 * groups, q_block, k_window), F32)],
        compiler_params=_cparams("arbitrary", "arbitrary", "arbitrary", "arbitrary"),
        name=f"banded_attention_d{d}")(*args)


def _alibi_slopes(n):
    return jnp.exp2(-8.0 * jnp.arange(1, n + 1, dtype=F32) / n)


def _log_sigmoid(x):
    return jnp.minimum(x, 0.0) - jnp.log1p(jnp.exp(-jnp.abs(x)))


def _mlstm_chunks(chains, seen, diag):
    nt = (((1,), (1,)), ((), ()))
    E = chains[0]["q"].shape[1]
    mask = [seen[c["rev"]] for c in chains]
    b_c = [jnp.sum(jnp.where(mk, c["lf_r"], 0.0), axis=1, keepdims=True) for c, mk in zip(chains, mask)]
    qk = [lax.dot_general(c["q"], c["k"], nt, preferred_element_type=F32) for c in chains]
    inter = [jnp.dot(c["q"], c["ct"].astype(BF16), preferred_element_type=F32) for c in chains]
    k_t = [c["k"].astype(F32).T for c in chains]
    b_last = [jnp.sum(c["lf_r"], axis=1, keepdims=True) for c in chains]
    b_r = [jnp.sum(jnp.where(diag, x, 0.0), axis=0, keepdims=True) for x in b_c]
    u_r = [c["i_r"] - x for c, x in zip(chains, b_r)]
    um = [jnp.where(mk, x, -jnp.inf) for mk, x in zip(mask, u_r)]
    g_t = [jnp.maximum(c["m"], jnp.max(x, axis=1, keepdims=True)) for c, x in zip(chains, um)]
    a_r = [x + y for x, y in zip(b_last, u_r)]
    m_new = [jnp.maximum(x + c["m"], jnp.max(y, axis=1, keepdims=True)) for c, x, y in zip(chains, b_last, a_r)]
    kw_t = [(x * jnp.exp(y - z)).astype(BF16) for x, y, z in zip(k_t, a_r, m_new)]
    ct = [jnp.exp(x + c["m"] - z) * c["ct"] + jnp.dot(y, c["v1"], preferred_element_type=F32)
          for c, x, y, z in zip(chains, b_last, kw_t, m_new)]
    w = [(jnp.exp(x - y) * z).astype(BF16) for x, y, z in zip(um, g_t, qk)]
    tot = [jnp.dot(x, c["v1"], preferred_element_type=F32) + jnp.exp(c["m"] - y) * z
           for c, x, y, z in zip(chains, w, g_t, inter)]
    h = [x[:, :E] / jnp.maximum(jnp.abs(x[:, E:]), jnp.exp(-(y + z))) for x, y, z in zip(tot, b_c, g_t)]
    return list(zip(h, ct, m_new))


def _mlstm_kernel(q_ref, k_ref, v_ref, gr_ref, gb_ref, hg_ref, o_ref, acc_ref, *, seq_len, chunk, heads):
    S, L, E = seq_len, chunk, M_HEAD_DIM
    n_chunks = S // L
    row = lax.broadcasted_iota(jnp.int32, (L, L), 0)
    col = lax.broadcasted_iota(jnp.int32, (L, L), 1)
    seen = (col <= row, col >= row)
    diag = col == row
    ones = jnp.ones((L, E), BF16)

    def finish(tot, lanes):
        return tot * lax.rsqrt(jnp.mean(tot * tot, axis=-1, keepdims=True) + EPS) * hg_ref[:, lanes]

    def sweep(second):
        def body(it, states):
            chains, where = [], []
            for n, (ct, m) in enumerate(states):
                hh, rev = divmod(n, 2)
                c = it if rev == 0 else n_chunks - 1 - it
                rows = pl.ds(pl.multiple_of(c * L, L), L)
                lanes = slice(hh * E, (hh + 1) * E)
                g = gr_ref[4 * hh:4 * hh + 4, rows] + gb_ref[4 * hh:4 * hh + 4, :]
                where.append((rows, lanes, acc_ref[rows, lanes] if second else None))
                chains.append(dict(q=q_ref[rows, lanes], k=k_ref[rows, lanes],
                                   v1=jnp.concatenate([v_ref[rows, lanes], ones], axis=1),
                                   i_r=g[rev:rev + 1], lf_r=_log_sigmoid(g[2 + rev:3 + rev]), ct=ct, m=m, rev=rev))
            out = _mlstm_chunks(chains, seen, diag)
            for (rows, lanes, prev), (h, _, _) in zip(where, out):
                if second:
                    o_ref[rows, lanes] = finish(prev + h, lanes).astype(o_ref.dtype)
                else:
                    acc_ref[rows, lanes] = h
            return tuple((ct, m) for _, ct, m in out)
        return body

    zero = (jnp.zeros((E, 2 * E), F32), jnp.zeros((1, 1), F32))
    states = lax.fori_loop(0, n_chunks // 2, sweep(False), (zero,) * (2 * heads))
    lax.fori_loop(n_chunks // 2, n_chunks, sweep(True), states)


def _mlstm(q, k, v, g_rows, gate_b, hnorm_g, Bn, S):
    E = M_HEAD_DIM
    T = Bn * S
    hs = MLSTM_HEADS_PER_STEP
    groups = M_HEADS // hs
    seq = pl.BlockSpec((None, S, hs * E), lambda b, h: (b, 0, h))
    q3, k3, v3 = (t.reshape(Bn, S, M_WIDTH) for t in (q, k, v))
    out = pl.pallas_call(
        functools.partial(_mlstm_kernel, seq_len=S, chunk=MLSTM_CHUNK, heads=hs),
        grid=(Bn, groups),
        in_specs=[seq, seq, seq,
                  pl.BlockSpec((None, 4 * hs, S), lambda b, h: (h, 0, b)),
                  pl.BlockSpec((None, 4 * hs, 1), lambda b, h: (h, 0, 0)),
                  pl.BlockSpec((1, hs * E), lambda b, h: (0, h))],
        out_specs=seq,
        out_shape=jax.ShapeDtypeStruct((Bn, S, M_WIDTH), BF16),
        scratch_shapes=[pltpu.VMEM((S, hs * E), F32)],
        compiler_params=_cparams("parallel", "parallel"), name="mlstm")(
            q3, k3, v3, g_rows.reshape(groups, 4 * hs, T), gate_b.reshape(groups, 4 * hs, 1),
            hnorm_g.reshape(1, M_WIDTH))
    return out.reshape(T, M_WIDTH)


def _mixer_ab(x, g, w_in, conv_w, conv_b, gate_b, hnorm_g, w_out, Bn, S):
    T = Bn * S
    MW, BW = M_WIDTH, B_WIDTH
    g0 = 4 * MW
    n_gate = 4 * M_HEADS
    w_qk = w_in[:, :2 * MW].astype(BF16)
    w_rest = jnp.concatenate([w_in[:, 2 * MW:g0], w_in[:, g0 + n_gate:]], axis=1).astype(BF16)
    wg_t = w_in[:, g0:g0 + n_gate].reshape(-1, 4, M_HEADS).transpose(2, 1, 0).reshape(n_gate, -1).astype(BF16)
    gate_b = gate_b.reshape(4, M_HEADS).T.astype(F32)
    (q_m, k_m, v_m, o_m, g_rows, q1, k1, v1, q4, k4, v4, q16, k16, v16) = _proj_ab(
        x, g, w_qk, w_rest, wg_t, conv_w, conv_b, S)

    hn = _mlstm(q_m, k_m, v_m, g_rows, gate_b, hnorm_g, Bn, S)

    slopes = _alibi_slopes(B_HEADS)
    outs, lses = [], []
    for d, qkv in zip(DILATIONS, ((q1, k1, v1), (q4, k4, v4), (q16, k16, v16))):
        qv, kv, vv = (t.reshape(Bn, S // d, d * BW) for t in qkv)
        o, lse = _banded_attention(qv, kv, vv, slopes, None, dilation=d, groups=1, radius=B_RADIUS,
                                   with_lse=True, out_dtype=BF16, q_block=B_Q_BLOCK, chains=B_CHAINS)
        outs.append(o.reshape(T // d, d * BW))
        lses.append(lse.reshape(T // d, d * BW))
    w_out = w_out.astype(BF16)
    return ("ab", hn, o_m, outs, lses, w_out[:MW], w_out[MW:])


def _c_head_order():
    order = []
    for pair in range(C_KV_HEADS // 2):
        for slot in range(C_GROUP):
            for j in range(2):
                order.append((2 * pair + j) * C_GROUP + slot)
    return jnp.array(order, dtype=jnp.int32)


def _mixer_c(x, g, w_in, sink, w_out, Bn, S):
    T, D = x.shape
    QW = C_HEADS * HEAD_DIM
    KW = C_KV_HEADS * HEAD_DIM
    order = _c_head_order()
    wq = w_in[:, :QW].reshape(D, C_HEADS, HEAD_DIM)[:, order].reshape(D, QW)
    w_main = jnp.concatenate([wq, w_in[:, QW:]], axis=1).astype(BF16)
    w_out = w_out.reshape(C_HEADS, HEAD_DIM, D)[order].reshape(QW, D).astype(BF16)
    segs = [(QW, BF16, ATTN_Q_SCALE), (KW, BF16, 1.0), (KW, BF16, 1.0)]
    q, k, v = _norm_proj(x, g, w_main, segs)
    (o,) = _banded_attention(q.reshape(Bn, S, QW), k.reshape(Bn, S, KW), v.reshape(Bn, S, KW),
                             _alibi_slopes(C_HEADS), sink.astype(F32), dilation=1, groups=C_GROUP,
                             radius=C_RADIUS, with_lse=False, out_dtype=BF16, q_block=C_Q_BLOCK, chains=C_CHAINS)
    return ("c", o.reshape(T, QW), w_out)


def kernel(x, norm_g, ffn_w1, ffn_w3, ffn_w2, ab_w_in, ab_conv_w, ab_conv_b, ab_gate_b, ab_hnorm_g, ab_w_out,
           c_w_in, c_sink, c_w_out, final_g):
    Bn, S, D = x.shape
    depth = norm_g.shape[0]
    w1, w3, w2 = (w.astype(BF16) for w in (ffn_w1, ffn_w3, ffn_w2))
    x = x.reshape(Bn * S, D)
    for l in range(depth):
        j = l // 2
        x = _ffn(x, norm_g[l, 0], w1, w3, w2, (l, 0))
        if l % 2 == 0:
            mixed = _mixer_ab(x, norm_g[l, 1], ab_w_in[j], ab_conv_w[j], ab_conv_b[j], ab_gate_b[j], ab_hnorm_g[j],
                              ab_w_out[j], Bn, S)
        else:
            mixed = _mixer_c(x, norm_g[l, 1], c_w_in[j], c_sink[j], c_w_out[j], Bn, S)
        x = _ffn(x, norm_g[l, 2], w1, w3, w2, (l, 1), final_g if l == depth - 1 else None, mixed)
    return x.reshape(Bn, S, D)
```

```python
import functools

import jax
import jax.numpy as jnp
from jax import lax
from jax.experimental import pallas as pl
from jax.experimental.pallas import tpu as pltpu

F32 = jnp.float32
BF16 = jnp.bfloat16

EPS = 1e-6
M_HEADS = 4
M_HEAD_DIM = 128
M_WIDTH = M_HEADS * M_HEAD_DIM
M_CONV = 5
B_HEADS = 8
B_WIDTH = 512
DILATIONS = (1, 4, 16)
B_RADIUS = 64
C_HEADS = 16
C_KV_HEADS = 4
C_GROUP = C_HEADS // C_KV_HEADS
C_RADIUS = 128
HEAD_DIM = 64
LANES = 128
SUBLANES = 8
LOG2E = 1.4426950408889634
ATTN_Q_SCALE = HEAD_DIM ** -0.5 * LOG2E

VMEM_LIMIT_BYTES = 56 * 1024 * 1024
ROW_TILE = 512
B_Q_BLOCK, B_CHAINS = 128, 64
C_Q_BLOCK, C_CHAINS = 128, 64
MLSTM_CHUNK = 256
CONV_COLS = 256
MLSTM_HEADS_PER_STEP = 2


def _cparams(*sem):
    return pltpu.CompilerParams(dimension_semantics=sem, vmem_limit_bytes=VMEM_LIMIT_BYTES)


def _resident(shape, index_map):
    return pl.BlockSpec(shape, index_map, pipeline_mode=pl.Buffered(1))


def _rms(x, g):
    return x * lax.rsqrt(jnp.mean(x * x, axis=-1, keepdims=True) + EPS) * g


def _mix_c(a_ref, w_ref):
    return jnp.dot(a_ref[...], w_ref[...], preferred_element_type=F32)


def _mix_ab(hn_ref, om_ref, o1_ref, l1_ref, o4_ref, l4_ref, o16_ref, l16_ref, wa_ref, wb_ref,
            so4_ref, sl4_ref, so16_ref, sl16_ref):
    tm = hn_ref.shape[0]
    n_col = B_WIDTH // LANES
    for d, pairs in ((DILATIONS[1], ((o4_ref, so4_ref), (l4_ref, sl4_ref))),
                     (DILATIONS[2], ((o16_ref, so16_ref), (l16_ref, sl16_ref)))):
        for src_ref, dst_ref in pairs:
            for r in range(d):
                for c in range(n_col):
                    lo = (c * d + r) * LANES
                    dst_ref[c, pl.ds(r, tm // d, stride=d), :] = src_ref[:, lo:lo + LANES].astype(F32)
    cols = []
    for c in range(n_col):
        lanes = slice(c * LANES, (c + 1) * LANES)
        l1, l2, l3 = l1_ref[:, lanes], sl4_ref[c], sl16_ref[c]
        lm = jnp.maximum(jnp.maximum(l1, l2), l3)
        e1, e2, e3 = jnp.exp2(l1 - lm), jnp.exp2(l2 - lm), jnp.exp2(l3 - lm)
        ob = (e1 * o1_ref[:, lanes].astype(F32) + e2 * so4_ref[c] + e3 * so16_ref[c]) / (e1 + e2 + e3)
        cols.append(ob.astype(BF16))
    ob = jnp.concatenate(cols, axis=1)
    ma = (hn_ref[...].astype(F32) * jax.nn.sigmoid(om_ref[...].astype(F32))).astype(BF16)
    return (jnp.dot(ma, wa_ref[...], preferred_element_type=F32)
            + jnp.dot(ob, wb_ref[...], preferred_element_type=F32))


def _ffn_kernel(x_ref, g_ref, w1_ref, w3_ref, w2_ref, *rest, final_norm, mix, n_mix_in, n_scratch, proj_segs):
    mix_in = rest[:n_mix_in]
    rest = rest[n_mix_in:]
    scratch = rest[len(rest) - n_scratch:]
    proj_outs = rest[len(rest) - n_scratch - len(proj_segs):len(rest) - n_scratch]
    o_ref = rest[len(rest) - n_scratch - len(proj_segs) - 1]
    x = x_ref[...]
    if mix is not None:
        x = x + mix(*mix_in, *scratch)
    h = _rms(x, g_ref[...]).astype(BF16)
    a = jnp.dot(h, w1_ref[...], preferred_element_type=F32)
    b = jnp.dot(h, w3_ref[...], preferred_element_type=F32)
    act = (a * jax.nn.sigmoid(a) * b).astype(BF16)
    y = x + 0.5 * jnp.dot(act, w2_ref[...], preferred_element_type=F32)
    if final_norm:
        y = _rms(y, rest[0][...])
    o_ref[...] = y
    if proj_segs:
        pg_ref, pw_ref = rest[final_norm:final_norm + 2]
        h2 = _rms(y, pg_ref[...]).astype(BF16)
        off = 0
        for (width, scale), p_ref in zip(proj_segs, proj_outs):
            z = jnp.dot(h2, pw_ref[:, off:off + width], preferred_element_type=F32)
            p_ref[...] = (z * scale if scale != 1.0 else z).astype(p_ref.dtype)
            off += width


def _ffn(x, g, w1, w3, w2, which, final_g=None, mixer=None, proj=None):
    T, D = x.shape
    tm = ROW_TILE
    layer, half = which
    stacked = lambda w: pl.BlockSpec((None, None) + w.shape[2:], lambda i: (layer, half, 0, 0),
                                     pipeline_mode=pl.Buffered(1))
    row = lambda w: pl.BlockSpec((tm, w), lambda i: (i, 0))
    res = lambda d: pl.BlockSpec((tm // d, d * B_WIDTH), lambda i: (i, 0))
    full = lambda a: _resident(a.shape, lambda i: (0, 0))
    g = g.reshape(1, D)
    in_specs = [row(D), full(g), stacked(w1), stacked(w3), stacked(w2)]
    args = [x, g, w1, w3, w2]
    mix, n_mix_in, scratch = None, 0, []
    if mixer is not None and mixer[0] == "c":
        _, att, w_out = mixer
        mix, mix_args, mix_specs = _mix_c, [att, w_out], [row(att.shape[1]), full(w_out)]
    elif mixer is not None:
        _, hn, om, outs, lses, wa, wb = mixer
        d4, d16 = DILATIONS[1:]
        mix = _mix_ab
        mix_args = [hn, om, outs[0], lses[0], outs[1], lses[1], outs[2], lses[2], wa, wb]
        mix_specs = [row(M_WIDTH)] * 4 + [res(d4)] * 2 + [res(d16)] * 2 + [full(wa), full(wb)]
        scratch = [pltpu.VMEM((B_WIDTH // LANES, tm, LANES), F32)] * 4
    if mix is not None:
        n_mix_in = len(mix_args)
        in_specs += mix_specs
        args += mix_args
    if final_g is not None:
        final_g = final_g.reshape(1, D)
        in_specs.append(full(final_g))
        args.append(final_g)
    out_specs, out_shape, proj_segs = [row(D)], [jax.ShapeDtypeStruct((T, D), F32)], ()
    if proj is not None:
        pg, pw, segs = proj
        pg = pg.reshape(1, D)
        in_specs += [full(pg), full(pw)]
        args += [pg, pw]
        proj_segs = tuple((wd, sc) for wd, _, sc in segs)
        out_specs += [row(wd) for wd, _, _ in segs]
        out_shape += [jax.ShapeDtypeStruct((T, wd), dt) for wd, dt, _ in segs]
    res_out = pl.pallas_call(
        functools.partial(_ffn_kernel, final_norm=final_g is not None, mix=mix, n_mix_in=n_mix_in,
                          n_scratch=len(scratch), proj_segs=proj_segs),
        grid=(T // tm,), in_specs=in_specs, out_specs=out_specs, out_shape=out_shape, scratch_shapes=scratch,
        compiler_params=_cparams("parallel"), name="ffn")(*args)
    return res_out if proj is not None else res_out[0]


def _proj_ab_kernel(x_ref, xp_ref, xn_ref, g_ref, wqk_ref, wr_ref, wg_ref, cw_ref, cb_ref,
                    qa_ref, ka_ref, vm_ref, om_ref, gr_ref, *rest, tiles_per_seq):
    tok_refs = rest[0:3]
    res_refs = {d: rest[3 * (n + 1):3 * (n + 2)] for n, d in enumerate(DILATIONS[1:])}
    stage_ref = rest[-1]
    tm = x_ref.shape[0]
    halo = SUBLANES
    g = g_ref[...]
    i = pl.program_id(0)
    first = i % tiles_per_seq == 0
    last = i % tiles_per_seq == tiles_per_seq - 1

    x = x_ref[...]
    h = _rms(x, g).astype(BF16)
    x_ext = jnp.concatenate([xp_ref[...], x, xn_ref[...]], axis=0)
    h_ext = _rms(x_ext, g).astype(BF16)
    n_ext = tm + 2 * halo
    row = lax.broadcasted_iota(jnp.int32, (n_ext, 1), 0)
    outside = ((row < halo) & first) | ((row >= halo + tm) & last)
    n_conv = 2 * M_WIDTH // CONV_COLS
    qk_blocks = [jnp.dot(h_ext, wqk_ref[:, blk * CONV_COLS:(blk + 1) * CONV_COLS], preferred_element_type=F32)
                 for blk in range(n_conv)]
    y_vm = jnp.dot(h, wr_ref[:, 0:M_WIDTH], preferred_element_type=F32)
    y_om = jnp.dot(h, wr_ref[:, M_WIDTH:2 * M_WIDTH], preferred_element_type=F32)
    y_g = lax.dot_general(wg_ref[...], h, (((1,), (1,)), ((), ())), preferred_element_type=F32)
    ys = [jnp.dot(h, wr_ref[:, 2 * M_WIDTH + s * B_WIDTH:2 * M_WIDTH + (s + 1) * B_WIDTH],
                  preferred_element_type=F32) for s in range(len(tok_refs))]

    for blk in range(n_conv):
        cols = slice(blk * CONV_COLS, (blk + 1) * CONV_COLS)
        qk = jnp.where(outside, 0.0, qk_blocks[blk])
        acc = cb_ref[:, cols] + qk[halo:halo + tm] * cw_ref[M_CONV // 2:M_CONV // 2 + 1, cols]
        for j in range(M_CONV):
            off = j - M_CONV // 2
            if off != 0:
                shifted = pltpu.roll(qk, (-off) % n_ext, axis=0)
                acc = acc + shifted[halo:halo + tm] * cw_ref[j:j + 1, cols]
        act = acc * jax.nn.sigmoid(acc)
        if blk * CONV_COLS < M_WIDTH:
            qa_ref[:, cols] = act.astype(BF16)
        else:
            kcols = slice(blk * CONV_COLS - M_WIDTH, (blk + 1) * CONV_COLS - M_WIDTH)
            ka_ref[:, kcols] = (act * (M_HEAD_DIM ** -0.5)).astype(BF16)

    vm_ref[...] = y_vm.astype(BF16)
    om_ref[...] = y_om.astype(BF16)
    gr_ref[...] = y_g

    n_col = B_WIDTH // LANES
    for s, tok_ref in enumerate(tok_refs):
        y = ys[s] * ATTN_Q_SCALE if s == 0 else ys[s]
        tok_ref[...] = y.astype(BF16)
        for c in range(n_col):
            stage_ref[s, c] = y[:, c * LANES:(c + 1) * LANES]
        for d, refs in res_refs.items():
            for r in range(d):
                for c in range(n_col):
                    lo = (c * d + r) * LANES
                    refs[s][:, lo:lo + LANES] = stage_ref[s, c, pl.ds(r, tm // d, stride=d), :].astype(BF16)


def _proj_ab(x, g, w_qk, w_rest, wg_t, conv_w, conv_b, seq_len):
    T, D = x.shape
    tm = ROW_TILE
    halo = SUBLANES
    hb = tm // halo
    row = lambda w: pl.BlockSpec((tm, w), lambda i: (i, 0))
    res = lambda d: pl.BlockSpec((tm // d, d * B_WIDTH), lambda i: (i, 0))
    full = lambda a: _resident(a.shape, lambda i: (0, 0))
    conv_b = conv_b.reshape(1, -1)
    g = g.reshape(1, D)
    out_specs = [row(M_WIDTH)] * 4 + [pl.BlockSpec((wg_t.shape[0], tm), lambda i: (0, i))] + [row(B_WIDTH)] * 3
    out_shape = ([jax.ShapeDtypeStruct((T, M_WIDTH), BF16)] * 4
                 + [jax.ShapeDtypeStruct((wg_t.shape[0], T), F32)]
                 + [jax.ShapeDtypeStruct((T, B_WIDTH), BF16)] * 3)
    for d in DILATIONS[1:]:
        out_specs += [res(d)] * 3
        out_shape += [jax.ShapeDtypeStruct((T // d, d * B_WIDTH), BF16)] * 3
    return pl.pallas_call(
        functools.partial(_proj_ab_kernel, tiles_per_seq=seq_len // tm),
        grid=(T // tm,),
        in_specs=[row(D),
                  pl.BlockSpec((halo, D), lambda i: (jnp.maximum(i * hb - 1, 0), 0)),
                  pl.BlockSpec((halo, D), lambda i: (jnp.minimum((i + 1) * hb, T // halo - 1), 0)),
                  full(g), full(w_qk), full(w_rest), full(wg_t), full(conv_w), full(conv_b)],
        out_specs=out_specs, out_shape=out_shape,
        scratch_shapes=[pltpu.VMEM((3, B_WIDTH // LANES, tm, LANES), F32)],
        compiler_params=_cparams("parallel"), name="proj_ab")(x, x, x, g, w_qk, w_rest, wg_t, conv_w, conv_b)


def _attn_kernel(*refs, groups, kv_tiles, q_tile, q_block, k_window, radius, dist_unit, seq_len, has_sink,
                 with_lse):
    slope_ref = refs[0]
    pos = 1
    sink_ref = None
    if has_sink:
        sink_ref = refs[pos]
        pos += 1
    q_ref, k_ref, v_ref, o_ref = refs[pos:pos + 4]
    pos += 4
    lse_ref = None
    if with_lse:
        lse_ref = refs[pos]
        pos += 1
    bias_ref = refs[pos]
    pair = pl.program_id(0)
    tile = pl.program_id(3)

    shift_step = min(q_block, radius)

    @pl.when((pl.program_id(1) == 0) & (pl.program_id(2) == 0) & (tile == 0))
    def _():
        sub = lax.broadcasted_iota(jnp.int32, (q_block, k_window), 0)
        ln = lax.broadcasted_iota(jnp.int32, (q_block, k_window), 1)
        for variant in range(2 * radius // shift_step + 1):
            adist = jnp.abs(ln - sub - variant * shift_step)
            negd = jnp.where(adist <= radius, -(dist_unit * adist).astype(F32), -jnp.inf)
            for g in range(groups):
                for j in range(2):
                    bias_ref[variant, 2 * g + j] = (slope_ref[(2 * pair + j) * groups + g] * LOG2E) * negd

    lane = lax.broadcasted_iota(jnp.int32, (q_block, LANES), 1)
    low = lane < HEAD_DIM
    ones = jnp.ones((k_window, LANES), BF16)
    chains = []
    for blk in range(q_tile // q_block):
        q0 = tile * q_tile + blk * q_block
        ks = pl.multiple_of(jnp.clip(q0 - radius, 0, seq_len - k_window), HEAD_DIM)
        variant = (q0 - ks) // shift_step
        rows = slice(blk * q_block, (blk + 1) * q_block)
        for t in range(kv_tiles):
            kt = k_ref[pl.ds(ks, k_window), t * LANES:(t + 1) * LANES]
            vt = jnp.concatenate([v_ref[pl.ds(ks, k_window), t * LANES:(t + 1) * LANES], ones], axis=1)
            for g in range(groups):
                qt = q_ref[rows, (t * groups + g) * LANES:(t * groups + g + 1) * LANES]
                for j in range(2):
                    qm = jnp.where(low if j == 0 else jnp.logical_not(low), qt, jnp.zeros_like(qt))
                    chains.append(dict(qm=qm, kt=kt, vt=vt, variant=variant, slot=2 * g + j,
                                       head=(2 * pair + j) * groups + g))
    s = [lax.dot_general(c["qm"], c["kt"], (((1,), (1,)), ((), ())), preferred_element_type=F32)
         + bias_ref[c["variant"], c["slot"]] for c in chains]
    m = [jnp.max(x, axis=-1, keepdims=True) for x in s]
    if has_sink:
        m = [jnp.maximum(x, sink_ref[c["head"]] * LOG2E) for c, x in zip(chains, m)]
    p = [jnp.exp2(x - y).astype(BF16) for x, y in zip(s, m)]
    od = [jnp.dot(x, c["vt"], preferred_element_type=F32) for c, x in zip(chains, p)]
    den = [x[:, LANES:] for x in od]
    if has_sink:
        den = [x + jnp.exp2(sink_ref[c["head"]] * LOG2E - y) for c, x, y in zip(chains, den, m)]
    o = [x[:, :LANES] / y for x, y in zip(od, den)]
    lse = [x + jnp.log2(y) for x, y in zip(m, den)] if with_lse else None
    for n in range(0, len(chains), 2):
        blk, tile_col = divmod(n // 2, kv_tiles * groups)
        rows = slice(blk * q_block, (blk + 1) * q_block)
        cols = slice(tile_col * LANES, (tile_col + 1) * LANES)
        o_ref[rows, cols] = jnp.where(low, o[n], o[n + 1]).astype(o_ref.dtype)
        if with_lse:
            lse_ref[rows, cols] = jnp.where(low, lse[n], lse[n + 1])


def _banded_attention(qv, kv, vv, slopes, sink, *, dilation, groups, radius, with_lse, out_dtype, q_block, chains):
    Bn, Lv, _ = qv.shape
    d = dilation
    pairs = kv.shape[2] // (d * LANES)
    q_tile = min(chains // (2 * groups) * q_block, Lv)
    kv_tiles = min(d, max(1, chains // (2 * groups * (q_tile // q_block))))
    k_window = q_block + 2 * radius
    shift_step = min(q_block, radius)
    assert max(q_block, radius) % shift_step == 0 and Lv % q_tile == 0 and k_window <= Lv
    assert d % kv_tiles == 0
    rsteps = d // kv_tiles
    smem = pl.BlockSpec(memory_space=pltpu.SMEM)
    qspec = pl.BlockSpec((None, q_tile, kv_tiles * groups * LANES), lambda p, b, r, i: (b, i, p * rsteps + r))
    kspec = pl.BlockSpec((None, Lv, kv_tiles * LANES), lambda p, b, r, i: (b, 0, p * rsteps + r))
    in_specs = [smem] + ([smem] if sink is not None else []) + [qspec, kspec, kspec]
    args = [slopes] + ([sink] if sink is not None else []) + [qv, kv, vv]
    out_specs = [qspec]
    out_shape = [jax.ShapeDtypeStruct(qv.shape, out_dtype)]
    if with_lse:
        out_specs.append(qspec)
        out_shape.append(jax.ShapeDtypeStruct(qv.shape, F32))
    return pl.pallas_call(
        functools.partial(_attn_kernel, groups=groups, kv_tiles=kv_tiles, q_tile=q_tile, q_block=q_block,
                          k_window=k_window,
                          radius=radius, dist_unit=d, seq_len=Lv, has_sink=sink is not None, with_lse=with_lse),
        grid=(pairs, Bn, rsteps, Lv // q_tile), in_specs=in_specs, out_specs=out_specs, out_shape=out_shape,
        scratch_shapes=[pltpu.VMEM((2 * radius // shift_step + 1, 2 * groups, q_block, k_window), F32)],
        compiler_params=_cparams("arbitrary", "arbitrary", "arbitrary", "arbitrary"),
        name=f"banded_attention_d{d}")(*args)


def _alibi_slopes(n):
    return jnp.exp2(-8.0 * jnp.arange(1, n + 1, dtype=F32) / n)


def _log_sigmoid(x):
    return jnp.minimum(x, 0.0) - jnp.log1p(jnp.exp(-jnp.abs(x)))


def _mlstm_chunks(chains, seen, diag):
    nt = (((1,), (1,)), ((), ()))
    E = chains[0]["q"].shape[1]
    mask = [seen[c["rev"]] for c in chains]
    b_c = [jnp.sum(jnp.where(mk, c["lf_r"], 0.0), axis=1, keepdims=True) for c, mk in zip(chains, mask)]
    qk = [lax.dot_general(c["q"], c["k"], nt, preferred_element_type=F32) for c in chains]
    inter = [jnp.dot(c["q"], c["ct"].astype(BF16), preferred_element_type=F32) for c in chains]
    k_t = [c["k"].astype(F32).T for c in chains]
    b_last = [jnp.sum(c["lf_r"], axis=1, keepdims=True) for c in chains]
    b_r = [jnp.sum(jnp.where(diag, x, 0.0), axis=0, keepdims=True) for x in b_c]
    u_r = [c["i_r"] - x for c, x in zip(chains, b_r)]
    um = [jnp.where(mk, x, -jnp.inf) for mk, x in zip(mask, u_r)]
    g_t = [jnp.maximum(c["m"], jnp.max(x, axis=1, keepdims=True)) for c, x in zip(chains, um)]
    a_r = [x + y for x, y in zip(b_last, u_r)]
    m_new = [jnp.maximum(x + c["m"], jnp.max(y, axis=1, keepdims=True)) for c, x, y in zip(chains, b_last, a_r)]
    kw_t = [(x * jnp.exp(y - z)).astype(BF16) for x, y, z in zip(k_t, a_r, m_new)]
    ct = [jnp.exp(x + c["m"] - z) * c["ct"] + jnp.dot(y, c["v1"], preferred_element_type=F32)
          for c, x, y, z in zip(chains, b_last, kw_t, m_new)]
    w = [(jnp.exp(x - y) * z).astype(BF16) for x, y, z in zip(um, g_t, qk)]
    tot = [jnp.dot(x, c["v1"], preferred_element_type=F32) + jnp.exp(c["m"] - y) * z
           for c, x, y, z in zip(chains, w, g_t, inter)]
    h = [x[:, :E] / jnp.maximum(jnp.abs(x[:, E:]), jnp.exp(-(y + z))) for x, y, z in zip(tot, b_c, g_t)]
    return list(zip(h, ct, m_new))


def _mlstm_kernel(q_ref, k_ref, v_ref, gr_ref, gb_ref, hg_ref, o_ref, acc_ref, *, seq_len, chunk, heads):
    S, L, E = seq_len, chunk, M_HEAD_DIM
    n_chunks = S // L
    row = lax.broadcasted_iota(jnp.int32, (L, L), 0)
    col = lax.broadcasted_iota(jnp.int32, (L, L), 1)
    seen = (col <= row, col >= row)
    diag = col == row
    ones = jnp.ones((L, E), BF16)

    def finish(tot, lanes):
        return tot * lax.rsqrt(jnp.mean(tot * tot, axis=-1, keepdims=True) + EPS) * hg_ref[:, lanes]

    def sweep(second):
        def body(it, states):
            chains, where = [], []
            for n, (ct, m) in enumerate(states):
                hh, rev = divmod(n, 2)
                c = it if rev == 0 else n_chunks - 1 - it
                rows = pl.ds(pl.multiple_of(c * L, L), L)
                lanes = slice(hh * E, (hh + 1) * E)
                g = gr_ref[4 * hh:4 * hh + 4, rows] + gb_ref[4 * hh:4 * hh + 4, :]
                where.append((rows, lanes, acc_ref[rows, lanes] if second else None))
                chains.append(dict(q=q_ref[rows, lanes], k=k_ref[rows, lanes],
                                   v1=jnp.concatenate([v_ref[rows, lanes], ones], axis=1),
                                   i_r=g[rev:rev + 1], lf_r=_log_sigmoid(g[2 + rev:3 + rev]), ct=ct, m=m, rev=rev))
            out = _mlstm_chunks(chains, seen, diag)
            for (rows, lanes, prev), (h, _, _) in zip(where, out):
                if second:
                    o_ref[rows, lanes] = finish(prev + h, lanes).astype(o_ref.dtype)
                else:
                    acc_ref[rows, lanes] = h
            return tuple((ct, m) for _, ct, m in out)
        return body

    zero = (jnp.zeros((E, 2 * E), F32), jnp.zeros((1, 1), F32))
    states = lax.fori_loop(0, n_chunks // 2, sweep(False), (zero,) * (2 * heads))
    lax.fori_loop(n_chunks // 2, n_chunks, sweep(True), states)


def _mlstm(q, k, v, g_rows, gate_b, hnorm_g, Bn, S):
    E = M_HEAD_DIM
    T = Bn * S
    hs = MLSTM_HEADS_PER_STEP
    groups = M_HEADS // hs
    seq = pl.BlockSpec((None, S, hs * E), lambda b, h: (b, 0, h))
    q3, k3, v3 = (t.reshape(Bn, S, M_WIDTH) for t in (q, k, v))
    out = pl.pallas_call(
        functools.partial(_mlstm_kernel, seq_len=S, chunk=MLSTM_CHUNK, heads=hs),
        grid=(Bn, groups),
        in_specs=[seq, seq, seq,
                  pl.BlockSpec((None, 4 * hs, S), lambda b, h: (h, 0, b)),
                  pl.BlockSpec((None, 4 * hs, 1), lambda b, h: (h, 0, 0)),
                  pl.BlockSpec((1, hs * E), lambda b, h: (0, h))],
        out_specs=seq,
        out_shape=jax.ShapeDtypeStruct((Bn, S, M_WIDTH), BF16),
        scratch_shapes=[pltpu.VMEM((S, hs * E), F32)],
        compiler_params=_cparams("parallel", "parallel"), name="mlstm")(
            q3, k3, v3, g_rows.reshape(groups, 4 * hs, T), gate_b.reshape(groups, 4 * hs, 1),
            hnorm_g.reshape(1, M_WIDTH))
    return out.reshape(T, M_WIDTH)


def _mixer_ab(x, g, w_in, conv_w, conv_b, gate_b, hnorm_g, w_out, Bn, S):
    T = Bn * S
    MW, BW = M_WIDTH, B_WIDTH
    g0 = 4 * MW
    n_gate = 4 * M_HEADS
    w_qk = w_in[:, :2 * MW].astype(BF16)
    w_rest = jnp.concatenate([w_in[:, 2 * MW:g0], w_in[:, g0 + n_gate:]], axis=1).astype(BF16)
    wg_t = w_in[:, g0:g0 + n_gate].reshape(-1, 4, M_HEADS).transpose(2, 1, 0).reshape(n_gate, -1).astype(BF16)
    gate_b = gate_b.reshape(4, M_HEADS).T.astype(F32)
    (q_m, k_m, v_m, o_m, g_rows, q1, k1, v1, q4, k4, v4, q16, k16, v16) = _proj_ab(
        x, g, w_qk, w_rest, wg_t, conv_w, conv_b, S)

    hn = _mlstm(q_m, k_m, v_m, g_rows, gate_b, hnorm_g, Bn, S)

    slopes = _alibi_slopes(B_HEADS)
    outs, lses = [], []
    for d, qkv in zip(DILATIONS, ((q1, k1, v1), (q4, k4, v4), (q16, k16, v16))):
        qv, kv, vv = (t.reshape(Bn, S // d, d * BW) for t in qkv)
        o, lse = _banded_attention(qv, kv, vv, slopes, None, dilation=d, groups=1, radius=B_RADIUS,
                                   with_lse=True, out_dtype=BF16, q_block=B_Q_BLOCK, chains=B_CHAINS)
        outs.append(o.reshape(T // d, d * BW))
        lses.append(lse.reshape(T // d, d * BW))
    w_out = w_out.astype(BF16)
    return ("ab", hn, o_m, outs, lses, w_out[:MW], w_out[MW:])


def _c_head_order():
    order = []
    for pair in range(C_KV_HEADS // 2):
        for slot in range(C_GROUP):
            for j in range(2):
                order.append((2 * pair + j) * C_GROUP + slot)
    return jnp.array(order, dtype=jnp.int32)


def _c_in_proj(g, w_in):
    D = w_in.shape[0]
    QW = C_HEADS * HEAD_DIM
    KW = C_KV_HEADS * HEAD_DIM
    wq = w_in[:, :QW].reshape(D, C_HEADS, HEAD_DIM)[:, _c_head_order()].reshape(D, QW)
    w_main = jnp.concatenate([wq, w_in[:, QW:]], axis=1).astype(BF16)
    return g, w_main, [(QW, BF16, ATTN_Q_SCALE), (KW, BF16, 1.0), (KW, BF16, 1.0)]


def _mixer_c(q, k, v, sink, w_out, Bn, S):
    T, QW = q.shape
    KW = k.shape[1]
    D = w_out.shape[1]
    w_out = w_out.reshape(C_HEADS, HEAD_DIM, D)[_c_head_order()].reshape(QW, D).astype(BF16)
    (o,) = _banded_attention(q.reshape(Bn, S, QW), k.reshape(Bn, S, KW), v.reshape(Bn, S, KW),
                             _alibi_slopes(C_HEADS), sink.astype(F32), dilation=1, groups=C_GROUP,
                             radius=C_RADIUS, with_lse=False, out_dtype=BF16, q_block=C_Q_BLOCK, chains=C_CHAINS)
    return ("c", o.reshape(T, QW), w_out)


def kernel(x, norm_g, ffn_w1, ffn_w3, ffn_w2, ab_w_in, ab_conv_w, ab_conv_b, ab_gate_b, ab_hnorm_g, ab_w_out,
           c_w_in, c_sink, c_w_out, final_g):
    Bn, S, D = x.shape
    depth = norm_g.shape[0]
    w1, w3, w2 = (w.astype(BF16) for w in (ffn_w1, ffn_w3, ffn_w2))
    x = x.reshape(Bn * S, D)
    for l in range(depth):
        j = l // 2
        if l % 2 == 0:
            x = _ffn(x, norm_g[l, 0], w1, w3, w2, (l, 0))
            mixed = _mixer_ab(x, norm_g[l, 1], ab_w_in[j], ab_conv_w[j], ab_conv_b[j], ab_gate_b[j], ab_hnorm_g[j],
                              ab_w_out[j], Bn, S)
        else:
            x, q, k, v = _ffn(x, norm_g[l, 0], w1, w3, w2, (l, 0), proj=_c_in_proj(norm_g[l, 1], c_w_in[j]))
            mixed = _mixer_c(q, k, v, c_sink[j], c_w_out[j], Bn, S)
        x = _ffn(x, norm_g[l, 2], w1, w3, w2, (l, 1), final_g if l == depth - 1 else None, mixed)
    return x.reshape(Bn, S, D)
```

```python
import functools

import jax
import jax.numpy as jnp
from jax import lax
from jax.experimental import pallas as pl
from jax.experimental.pallas import tpu as pltpu

F32 = jnp.float32
BF16 = jnp.bfloat16

EPS = 1e-6
M_HEADS = 4
M_HEAD_DIM = 128
M_WIDTH = M_HEADS * M_HEAD_DIM
M_CONV = 5
B_HEADS = 8
B_WIDTH = 512
DILATIONS = (1, 4, 16)
B_RADIUS = 64
C_HEADS = 16
C_KV_HEADS = 4
C_GROUP = C_HEADS // C_KV_HEADS
C_RADIUS = 128
HEAD_DIM = 64
LANES = 128
SUBLANES = 8
LOG2E = 1.4426950408889634
ATTN_Q_SCALE = HEAD_DIM ** -0.5 * LOG2E

VMEM_LIMIT_BYTES = 56 * 1024 * 1024
ROW_TILE = 512
B_Q_BLOCK, B_CHAINS = 128, 128
C_Q_BLOCK, C_CHAINS = 128, 128
MLSTM_CHUNK = 256
CONV_COLS = 256
MLSTM_HEADS_PER_STEP = 2


def _cparams(*sem):
    return pltpu.CompilerParams(dimension_semantics=sem, vmem_limit_bytes=VMEM_LIMIT_BYTES)


def _resident(shape, index_map):
    return pl.BlockSpec(shape, index_map, pipeline_mode=pl.Buffered(1))


def _rms(x, g):
    return x * lax.rsqrt(jnp.mean(x * x, axis=-1, keepdims=True) + EPS) * g


def _mix_c(a_ref, w_ref):
    return jnp.dot(a_ref[...], w_ref[...], preferred_element_type=F32)


def _mix_ab(hn_ref, om_ref, o1_ref, l1_ref, o4_ref, l4_ref, o16_ref, l16_ref, wa_ref, wb_ref,
            so4_ref, sl4_ref, so16_ref, sl16_ref):
    tm = hn_ref.shape[0]
    n_col = B_WIDTH // LANES
    for d, pairs in ((DILATIONS[1], ((o4_ref, so4_ref), (l4_ref, sl4_ref))),
                     (DILATIONS[2], ((o16_ref, so16_ref), (l16_ref, sl16_ref)))):
        for src_ref, dst_ref in pairs:
            for r in range(d):
                for c in range(n_col):
                    lo = (c * d + r) * LANES
                    dst_ref[c, pl.ds(r, tm // d, stride=d), :] = src_ref[:, lo:lo + LANES].astype(F32)
    cols = []
    for c in range(n_col):
        lanes = slice(c * LANES, (c + 1) * LANES)
        l1, l2, l3 = l1_ref[:, lanes], sl4_ref[c], sl16_ref[c]
        lm = jnp.maximum(jnp.maximum(l1, l2), l3)
        e1, e2, e3 = jnp.exp2(l1 - lm), jnp.exp2(l2 - lm), jnp.exp2(l3 - lm)
        ob = (e1 * o1_ref[:, lanes].astype(F32) + e2 * so4_ref[c] + e3 * so16_ref[c]) / (e1 + e2 + e3)
        cols.append(ob.astype(BF16))
    ob = jnp.concatenate(cols, axis=1)
    ma = (hn_ref[...].astype(F32) * jax.nn.sigmoid(om_ref[...].astype(F32))).astype(BF16)
    return (jnp.dot(ma, wa_ref[...], preferred_element_type=F32)
            + jnp.dot(ob, wb_ref[...], preferred_element_type=F32))


def _ffn_kernel(x_ref, g_ref, w1_ref, w3_ref, w2_ref, *rest, final_norm, mix, n_mix_in, n_scratch, proj_segs):
    mix_in = rest[:n_mix_in]
    rest = rest[n_mix_in:]
    scratch = rest[len(rest) - n_scratch:]
    proj_outs = rest[len(rest) - n_scratch - len(proj_segs):len(rest) - n_scratch]
    o_ref = rest[len(rest) - n_scratch - len(proj_segs) - 1]
    x = x_ref[...]
    if mix is not None:
        x = x + mix(*mix_in, *scratch)
    h = _rms(x, g_ref[...]).astype(BF16)
    a = jnp.dot(h, w1_ref[...], preferred_element_type=F32)
    b = jnp.dot(h, w3_ref[...], preferred_element_type=F32)
    act = (a * jax.nn.sigmoid(a) * b).astype(BF16)
    y = x + 0.5 * jnp.dot(act, w2_ref[...], preferred_element_type=F32)
    if final_norm:
        y = _rms(y, rest[0][...])
    o_ref[...] = y
    if proj_segs:
        pg_ref, pw_ref = rest[final_norm:final_norm + 2]
        h2 = _rms(y, pg_ref[...]).astype(BF16)
        off = 0
        for (width, scale), p_ref in zip(proj_segs, proj_outs):
            z = jnp.dot(h2, pw_ref[:, off:off + width], preferred_element_type=F32)
            p_ref[...] = (z * scale if scale != 1.0 else z).astype(p_ref.dtype)
            off += width


def _ffn(x, g, w1, w3, w2, which, final_g=None, mixer=None, proj=None):
    T, D = x.shape
    tm = ROW_TILE
    layer, half = which
    stacked = lambda w: pl.BlockSpec((None, None) + w.shape[2:], lambda i: (layer, half, 0, 0),
                                     pipeline_mode=pl.Buffered(1))
    row = lambda w: pl.BlockSpec((tm, w), lambda i: (i, 0))
    res = lambda d: pl.BlockSpec((tm // d, d * B_WIDTH), lambda i: (i, 0))
    full = lambda a: _resident(a.shape, lambda i: (0, 0))
    g = g.reshape(1, D)
    in_specs = [row(D), full(g), stacked(w1), stacked(w3), stacked(w2)]
    args = [x, g, w1, w3, w2]
    mix, n_mix_in, scratch = None, 0, []
    if mixer is not None and mixer[0] == "c":
        _, att, w_out = mixer
        mix, mix_args, mix_specs = _mix_c, [att, w_out], [row(att.shape[1]), full(w_out)]
    elif mixer is not None:
        _, hn, om, outs, lses, wa, wb = mixer
        d4, d16 = DILATIONS[1:]
        mix = _mix_ab
        mix_args = [hn, om, outs[0], lses[0], outs[1], lses[1], outs[2], lses[2], wa, wb]
        mix_specs = [row(M_WIDTH)] * 4 + [res(d4)] * 2 + [res(d16)] * 2 + [full(wa), full(wb)]
        scratch = [pltpu.VMEM((B_WIDTH // LANES, tm, LANES), F32)] * 4
    if mix is not None:
        n_mix_in = len(mix_args)
        in_specs += mix_specs
        args += mix_args
    if final_g is not None:
        final_g = final_g.reshape(1, D)
        in_specs.append(full(final_g))
        args.append(final_g)
    out_specs, out_shape, proj_segs = [row(D)], [jax.ShapeDtypeStruct((T, D), F32)], ()
    if proj is not None:
        pg, pw, segs = proj
        pg = pg.reshape(1, D)
        in_specs += [full(pg), full(pw)]
        args += [pg, pw]
        proj_segs = tuple((wd, sc) for wd, _, sc in segs)
        out_specs += [row(wd) for wd, _, _ in segs]
        out_shape += [jax.ShapeDtypeStruct((T, wd), dt) for wd, dt, _ in segs]
    res_out = pl.pallas_call(
        functools.partial(_ffn_kernel, final_norm=final_g is not None, mix=mix, n_mix_in=n_mix_in,
                          n_scratch=len(scratch), proj_segs=proj_segs),
        grid=(T // tm,), in_specs=in_specs, out_specs=out_specs, out_shape=out_shape, scratch_shapes=scratch,
        compiler_params=_cparams("parallel"), name="ffn")(*args)
    return res_out if proj is not None else res_out[0]


def _proj_ab_kernel(x_ref, xp_ref, xn_ref, g_ref, wqk_ref, wr_ref, wg_ref, cw_ref, cb_ref,
                    qa_ref, ka_ref, vm_ref, om_ref, gr_ref, *rest, tiles_per_seq):
    tok_refs = rest[0:3]
    res_refs = {d: rest[3 * (n + 1):3 * (n + 2)] for n, d in enumerate(DILATIONS[1:])}
    stage_ref = rest[-1]
    tm = x_ref.shape[0]
    halo = SUBLANES
    g = g_ref[...]
    i = pl.program_id(0)
    first = i % tiles_per_seq == 0
    last = i % tiles_per_seq == tiles_per_seq - 1

    x = x_ref[...]
    h = _rms(x, g).astype(BF16)
    x_ext = jnp.concatenate([xp_ref[...], x, xn_ref[...]], axis=0)
    h_ext = _rms(x_ext, g).astype(BF16)
    n_ext = tm + 2 * halo
    row = lax.broadcasted_iota(jnp.int32, (n_ext, 1), 0)
    outside = ((row < halo) & first) | ((row >= halo + tm) & last)
    n_conv = 2 * M_WIDTH // CONV_COLS
    qk_blocks = [jnp.dot(h_ext, wqk_ref[:, blk * CONV_COLS:(blk + 1) * CONV_COLS], preferred_element_type=F32)
                 for blk in range(n_conv)]
    y_vm = jnp.dot(h, wr_ref[:, 0:M_WIDTH], preferred_element_type=F32)
    y_om = jnp.dot(h, wr_ref[:, M_WIDTH:2 * M_WIDTH], preferred_element_type=F32)
    y_g = lax.dot_general(wg_ref[...], h, (((1,), (1,)), ((), ())), preferred_element_type=F32)
    ys = [jnp.dot(h, wr_ref[:, 2 * M_WIDTH + s * B_WIDTH:2 * M_WIDTH + (s + 1) * B_WIDTH],
                  preferred_element_type=F32) for s in range(len(tok_refs))]

    for blk in range(n_conv):
        cols = slice(blk * CONV_COLS, (blk + 1) * CONV_COLS)
        qk = jnp.where(outside, 0.0, qk_blocks[blk])
        acc = cb_ref[:, cols] + qk[halo:halo + tm] * cw_ref[M_CONV // 2:M_CONV // 2 + 1, cols]
        for j in range(M_CONV):
            off = j - M_CONV // 2
            if off != 0:
                shifted = pltpu.roll(qk, (-off) % n_ext, axis=0)
                acc = acc + shifted[halo:halo + tm] * cw_ref[j:j + 1, cols]
        act = acc * jax.nn.sigmoid(acc)
        if blk * CONV_COLS < M_WIDTH:
            qa_ref[:, cols] = act.astype(BF16)
        else:
            kcols = slice(blk * CONV_COLS - M_WIDTH, (blk + 1) * CONV_COLS - M_WIDTH)
            ka_ref[:, kcols] = (act * (M_HEAD_DIM ** -0.5)).astype(BF16)

    vm_ref[...] = y_vm.astype(BF16)
    om_ref[...] = y_om.astype(BF16)
    gr_ref[...] = y_g

    n_col = B_WIDTH // LANES
    for s, tok_ref in enumerate(tok_refs):
        y = ys[s] * ATTN_Q_SCALE if s == 0 else ys[s]
        tok_ref[...] = y.astype(BF16)
        for c in range(n_col):
            stage_ref[s, c] = y[:, c * LANES:(c + 1) * LANES]
        for d, refs in res_refs.items():
            for r in range(d):
                for c in range(n_col):
                    lo = (c * d + r) * LANES
                    refs[s][:, lo:lo + LANES] = stage_ref[s, c, pl.ds(r, tm // d, stride=d), :].astype(BF16)


def _proj_ab(x, g, w_qk, w_rest, wg_t, conv_w, conv_b, seq_len):
    T, D = x.shape
    tm = ROW_TILE
    halo = SUBLANES
    hb = tm // halo
    row = lambda w: pl.BlockSpec((tm, w), lambda i: (i, 0))
    res = lambda d: pl.BlockSpec((tm // d, d * B_WIDTH), lambda i: (i, 0))
    full = lambda a: _resident(a.shape, lambda i: (0, 0))
    conv_b = conv_b.reshape(1, -1)
    g = g.reshape(1, D)
    out_specs = [row(M_WIDTH)] * 4 + [pl.BlockSpec((wg_t.shape[0], tm), lambda i: (0, i))] + [row(B_WIDTH)] * 3
    out_shape = ([jax.ShapeDtypeStruct((T, M_WIDTH), BF16)] * 4
                 + [jax.ShapeDtypeStruct((wg_t.shape[0], T), F32)]
                 + [jax.ShapeDtypeStruct((T, B_WIDTH), BF16)] * 3)
    for d in DILATIONS[1:]:
        out_specs += [res(d)] * 3
        out_shape += [jax.ShapeDtypeStruct((T // d, d * B_WIDTH), BF16)] * 3
    return pl.pallas_call(
        functools.partial(_proj_ab_kernel, tiles_per_seq=seq_len // tm),
        grid=(T // tm,),
        in_specs=[row(D),
                  pl.BlockSpec((halo, D), lambda i: (jnp.maximum(i * hb - 1, 0), 0)),
                  pl.BlockSpec((halo, D), lambda i: (jnp.minimum((i + 1) * hb, T // halo - 1), 0)),
                  full(g), full(w_qk), full(w_rest), full(wg_t), full(conv_w), full(conv_b)],
        out_specs=out_specs, out_shape=out_shape,
        scratch_shapes=[pltpu.VMEM((3, B_WIDTH // LANES, tm, LANES), F32)],
        compiler_params=_cparams("parallel"), name="proj_ab")(x, x, x, g, w_qk, w_rest, wg_t, conv_w, conv_b)


def _attn_kernel(*refs, groups, kv_tiles, q_tile, q_block, k_window, radius, dist_unit, seq_len, has_sink,
                 with_lse):
    slope_ref = refs[0]
    pos = 1
    sink_ref = None
    if has_sink:
        sink_ref = refs[pos]
        pos += 1
    q_ref, k_ref, v_ref, o_ref = refs[pos:pos + 4]
    pos += 4
    lse_ref = None
    if with_lse:
        lse_ref = refs[pos]
        pos += 1
    bias_ref = refs[pos]
    pair = pl.program_id(0)
    tile = pl.program_id(3)

    shift_step = min(q_block, radius)

    @pl.when((pl.program_id(1) == 0) & (pl.program_id(2) == 0) & (tile == 0))
    def _():
        sub = lax.broadcasted_iota(jnp.int32, (q_block, k_window), 0)
        ln = lax.broadcasted_iota(jnp.int32, (q_block, k_window), 1)
        for variant in range(2 * radius // shift_step + 1):
            adist = jnp.abs(ln - sub - variant * shift_step)
            negd = jnp.where(adist <= radius, -(dist_unit * adist).astype(F32), -jnp.inf)
            for g in range(groups):
                for j in range(2):
                    bias_ref[variant, 2 * g + j] = (slope_ref[(2 * pair + j) * groups + g] * LOG2E) * negd

    lane = lax.broadcasted_iota(jnp.int32, (q_block, LANES), 1)
    low = lane < HEAD_DIM
    ones = jnp.ones((k_window, LANES), BF16)
    chains = []
    for blk in range(q_tile // q_block):
        q0 = tile * q_tile + blk * q_block
        ks = pl.multiple_of(jnp.clip(q0 - radius, 0, seq_len - k_window), HEAD_DIM)
        variant = (q0 - ks) // shift_step
        rows = slice(blk * q_block, (blk + 1) * q_block)
        for t in range(kv_tiles):
            kt = k_ref[pl.ds(ks, k_window), t * LANES:(t + 1) * LANES]
            vt = jnp.concatenate([v_ref[pl.ds(ks, k_window), t * LANES:(t + 1) * LANES], ones], axis=1)
            for g in range(groups):
                qt = q_ref[rows, (t * groups + g) * LANES:(t * groups + g + 1) * LANES]
                for j in range(2):
                    qm = jnp.where(low if j == 0 else jnp.logical_not(low), qt, jnp.zeros_like(qt))
                    chains.append(dict(qm=qm, kt=kt, vt=vt, variant=variant, slot=2 * g + j,
                                       head=(2 * pair + j) * groups + g))
    s = [lax.dot_general(c["qm"], c["kt"], (((1,), (1,)), ((), ())), preferred_element_type=F32)
         + bias_ref[c["variant"], c["slot"]] for c in chains]
    m = [jnp.max(x, axis=-1, keepdims=True) for x in s]
    if has_sink:
        m = [jnp.maximum(x, sink_ref[c["head"]] * LOG2E) for c, x in zip(chains, m)]
    p = [jnp.exp2(x - y).astype(BF16) for x, y in zip(s, m)]
    od = [jnp.dot(x, c["vt"], preferred_element_type=F32) for c, x in zip(chains, p)]
    den = [x[:, LANES:] for x in od]
    if has_sink:
        den = [x + jnp.exp2(sink_ref[c["head"]] * LOG2E - y) for c, x, y in zip(chains, den, m)]
    o = [x[:, :LANES] / y for x, y in zip(od, den)]
    lse = [x + jnp.log2(y) for x, y in zip(m, den)] if with_lse else None
    for n in range(0, len(chains), 2):
        blk, tile_col = divmod(n // 2, kv_tiles * groups)
        rows = slice(blk * q_block, (blk + 1) * q_block)
        cols = slice(tile_col * LANES, (tile_col + 1) * LANES)
        o_ref[rows, cols] = jnp.where(low, o[n], o[n + 1]).astype(o_ref.dtype)
        if with_lse:
            lse_ref[rows, cols] = jnp.where(low, lse[n], lse[n + 1])


def _banded_attention(qv, kv, vv, slopes, sink, *, dilation, groups, radius, with_lse, out_dtype, q_block, chains):
    Bn, Lv, _ = qv.shape
    d = dilation
    pairs = kv.shape[2] // (d * LANES)
    q_tile = min(chains // (2 * groups) * q_block, Lv)
    kv_tiles = min(d, max(1, chains // (2 * groups * (q_tile // q_block))))
    k_window = q_block + 2 * radius
    shift_step = min(q_block, radius)
    assert max(q_block, radius) % shift_step == 0 and Lv % q_tile == 0 and k_window <= Lv
    assert d % kv_tiles == 0
    rsteps = d // kv_tiles
    smem = pl.BlockSpec(memory_space=pltpu.SMEM)
    qspec = pl.BlockSpec((None, q_tile, kv_tiles * groups * LANES), lambda p, b, r, i: (b, i, p * rsteps + r))
    kspec = pl.BlockSpec((None, Lv, kv_tiles * LANES), lambda p, b, r, i: (b, 0, p * rsteps + r))
    in_specs = [smem] + ([smem] if sink is not None else []) + [qspec, kspec, kspec]
    args = [slopes] + ([sink] if sink is not None else []) + [qv, kv, vv]
    out_specs = [qspec]
    out_shape = [jax.ShapeDtypeStruct(qv.shape, out_dtype)]
    if with_lse:
        out_specs.append(qspec)
        out_shape.append(jax.ShapeDtypeStruct(qv.shape, F32))
    return pl.pallas_call(
        functools.partial(_attn_kernel, groups=groups, kv_tiles=kv_tiles, q_tile=q_tile, q_block=q_block,
                          k_window=k_window,
                          radius=radius, dist_unit=d, seq_len=Lv, has_sink=sink is not None, with_lse=with_lse),
        grid=(pairs, Bn, rsteps, Lv // q_tile), in_specs=in_specs, out_specs=out_specs, out_shape=out_shape,
        scratch_shapes=[pltpu.VMEM((2 * radius // shift_step + 1, 2 * groups, q_block, k_window), F32)],
        compiler_params=_cparams("arbitrary", "arbitrary", "arbitrary", "arbitrary"),
        name=f"banded_attention_d{d}")(*args)


def _alibi_slopes(n):
    return jnp.exp2(-8.0 * jnp.arange(1, n + 1, dtype=F32) / n)


def _log_sigmoid(x):
    return jnp.minimum(x, 0.0) - jnp.log1p(jnp.exp(-jnp.abs(x)))


def _mlstm_chunks(chains, seen, diag):
    nt = (((1,), (1,)), ((), ()))
    E = chains[0]["q"].shape[1]
    mask = [seen[c["rev"]] for c in chains]
    b_c = [jnp.sum(jnp.where(mk, c["lf_r"], 0.0), axis=1, keepdims=True) for c, mk in zip(chains, mask)]
    qk = [lax.dot_general(c["q"], c["k"], nt, preferred_element_type=F32) for c in chains]
    inter = [jnp.dot(c["q"], c["ct"].astype(BF16), preferred_element_type=F32) for c in chains]
    k_t = [c["k"].astype(F32).T for c in chains]
    b_last = [jnp.sum(c["lf_r"], axis=1, keepdims=True) for c in chains]
    b_r = [jnp.sum(jnp.where(diag, x, 0.0), axis=0, keepdims=True) for x in b_c]
    u_r = [c["i_r"] - x for c, x in zip(chains, b_r)]
    um = [jnp.where(mk, x, -jnp.inf) for mk, x in zip(mask, u_r)]
    g_t = [jnp.maximum(c["m"], jnp.max(x, axis=1, keepdims=True)) for c, x in zip(chains, um)]
    a_r = [x + y for x, y in zip(b_last, u_r)]
    m_new = [jnp.maximum(x + c["m"], jnp.max(y, axis=1, keepdims=True)) for c, x, y in zip(chains, b_last, a_r)]
    kw_t = [(x * jnp.exp2(y - z)).astype(BF16) for x, y, z in zip(k_t, a_r, m_new)]
    ct = [jnp.exp2(x + c["m"] - z) * c["ct"] + jnp.dot(y, c["v1"], preferred_element_type=F32)
          for c, x, y, z in zip(chains, b_last, kw_t, m_new)]
    w = [(jnp.exp2(x - y) * z).astype(BF16) for x, y, z in zip(um, g_t, qk)]
    tot = [jnp.dot(x, c["v1"], preferred_element_type=F32) + jnp.exp2(c["m"] - y) * z
           for c, x, y, z in zip(chains, w, g_t, inter)]
    h = [x[:, :E] / jnp.maximum(jnp.abs(x[:, E:]), jnp.exp2(-(y + z))) for x, y, z in zip(tot, b_c, g_t)]
    return list(zip(h, ct, m_new))


def _mlstm_kernel(q_ref, k_ref, v_ref, gr_ref, gb_ref, hg_ref, o_ref, acc_ref, *, seq_len, chunk, heads):
    S, L, E = seq_len, chunk, M_HEAD_DIM
    n_chunks = S // L
    row = lax.broadcasted_iota(jnp.int32, (L, L), 0)
    col = lax.broadcasted_iota(jnp.int32, (L, L), 1)
    seen = (col <= row, col >= row)
    diag = col == row
    ones = jnp.ones((L, E), BF16)

    def finish(tot, lanes):
        return tot * lax.rsqrt(jnp.mean(tot * tot, axis=-1, keepdims=True) + EPS) * hg_ref[:, lanes]

    def sweep(second):
        def body(it, states):
            chains, where = [], []
            for n, (ct, m) in enumerate(states):
                hh, rev = divmod(n, 2)
                c = it if rev == 0 else n_chunks - 1 - it
                rows = pl.ds(pl.multiple_of(c * L, L), L)
                lanes = slice(hh * E, (hh + 1) * E)
                g = gr_ref[4 * hh:4 * hh + 4, rows] + gb_ref[4 * hh:4 * hh + 4, :]
                where.append((rows, lanes, acc_ref[rows, lanes] if second else None))
                chains.append(dict(q=q_ref[rows, lanes], k=k_ref[rows, lanes],
                                   v1=jnp.concatenate([v_ref[rows, lanes], ones], axis=1),
                                   i_r=g[rev:rev + 1] * LOG2E, lf_r=_log_sigmoid(g[2 + rev:3 + rev]) * LOG2E,
                                   ct=ct, m=m, rev=rev))
            out = _mlstm_chunks(chains, seen, diag)
            for (rows, lanes, prev), (h, _, _) in zip(where, out):
                if second:
                    o_ref[rows, lanes] = finish(prev + h, lanes).astype(o_ref.dtype)
                else:
                    acc_ref[rows, lanes] = h
            return tuple((ct, m) for _, ct, m in out)
        return body

    zero = (jnp.zeros((E, 2 * E), F32), jnp.zeros((1, 1), F32))
    states = lax.fori_loop(0, n_chunks // 2, sweep(False), (zero,) * (2 * heads))
    lax.fori_loop(n_chunks // 2, n_chunks, sweep(True), states)


def _mlstm(q, k, v, g_rows, gate_b, hnorm_g, Bn, S):
    E = M_HEAD_DIM
    T = Bn * S
    hs = MLSTM_HEADS_PER_STEP
    groups = M_HEADS // hs
    seq = pl.BlockSpec((None, S, hs * E), lambda b, h: (b, 0, h))
    q3, k3, v3 = (t.reshape(Bn, S, M_WIDTH) for t in (q, k, v))
    out = pl.pallas_call(
        functools.partial(_mlstm_kernel, seq_len=S, chunk=MLSTM_CHUNK, heads=hs),
        grid=(Bn, groups),
        in_specs=[seq, seq, seq,
                  pl.BlockSpec((None, 4 * hs, S), lambda b, h: (h, 0, b)),
                  pl.BlockSpec((None, 4 * hs, 1), lambda b, h: (h, 0, 0)),
                  pl.BlockSpec((1, hs * E), lambda b, h: (0, h))],
        out_specs=seq,
        out_shape=jax.ShapeDtypeStruct((Bn, S, M_WIDTH), BF16),
        scratch_shapes=[pltpu.VMEM((S, hs * E), F32)],
        compiler_params=_cparams("parallel", "parallel"), name="mlstm")(
            q3, k3, v3, g_rows.reshape(groups, 4 * hs, T), gate_b.reshape(groups, 4 * hs, 1),
            hnorm_g.reshape(1, M_WIDTH))
    return out.reshape(T, M_WIDTH)


def _mixer_ab(x, g, w_in, conv_w, conv_b, gate_b, hnorm_g, w_out, Bn, S):
    T = Bn * S
    MW, BW = M_WIDTH, B_WIDTH
    g0 = 4 * MW
    n_gate = 4 * M_HEADS
    w_qk = w_in[:, :2 * MW].astype(BF16)
    w_rest = jnp.concatenate([w_in[:, 2 * MW:g0], w_in[:, g0 + n_gate:]], axis=1).astype(BF16)
    wg_t = w_in[:, g0:g0 + n_gate].reshape(-1, 4, M_HEADS).transpose(2, 1, 0).reshape(n_gate, -1).astype(BF16)
    gate_b = gate_b.reshape(4, M_HEADS).T.astype(F32)
    (q_m, k_m, v_m, o_m, g_rows, q1, k1, v1, q4, k4, v4, q16, k16, v16) = _proj_ab(
        x, g, w_qk, w_rest, wg_t, conv_w, conv_b, S)

    hn = _mlstm(q_m, k_m, v_m, g_rows, gate_b, hnorm_g, Bn, S)

    slopes = _alibi_slopes(B_HEADS)
    outs, lses = [], []
    for d, qkv in zip(DILATIONS, ((q1, k1, v1), (q4, k4, v4), (q16, k16, v16))):
        qv, kv, vv = (t.reshape(Bn, S // d, d * BW) for t in qkv)
        o, lse = _banded_attention(qv, kv, vv, slopes, None, dilation=d, groups=1, radius=B_RADIUS,
                                   with_lse=True, out_dtype=BF16, q_block=B_Q_BLOCK, chains=B_CHAINS)
        outs.append(o.reshape(T // d, d * BW))
        lses.append(lse.reshape(T // d, d * BW))
    w_out = w_out.astype(BF16)
    return ("ab", hn, o_m, outs, lses, w_out[:MW], w_out[MW:])


def _c_head_order():
    order = []
    for pair in range(C_KV_HEADS // 2):
        for slot in range(C_GROUP):
            for j in range(2):
                order.append((2 * pair + j) * C_GROUP + slot)
    return jnp.array(order, dtype=jnp.int32)


def _c_in_proj(g, w_in):
    D = w_in.shape[0]
    QW = C_HEADS * HEAD_DIM
    KW = C_KV_HEADS * HEAD_DIM
    wq = w_in[:, :QW].reshape(D, C_HEADS, HEAD_DIM)[:, _c_head_order()].reshape(D, QW)
    w_main = jnp.concatenate([wq, w_in[:, QW:]], axis=1).astype(BF16)
    return g, w_main, [(QW, BF16, ATTN_Q_SCALE), (KW, BF16, 1.0), (KW, BF16, 1.0)]


def _mixer_c(q, k, v, sink, w_out, Bn, S):
    T, QW = q.shape
    KW = k.shape[1]
    D = w_out.shape[1]
    w_out = w_out.reshape(C_HEADS, HEAD_DIM, D)[_c_head_order()].reshape(QW, D).astype(BF16)
    (o,) = _banded_attention(q.reshape(Bn, S, QW), k.reshape(Bn, S, KW), v.reshape(Bn, S, KW),
                             _alibi_slopes(C_HEADS), sink.astype(F32), dilation=1, groups=C_GROUP,
                             radius=C_RADIUS, with_lse=False, out_dtype=BF16, q_block=C_Q_BLOCK, chains=C_CHAINS)
    return ("c", o.reshape(T, QW), w_out)


def kernel(x, norm_g, ffn_w1, ffn_w3, ffn_w2, ab_w_in, ab_conv_w, ab_conv_b, ab_gate_b, ab_hnorm_g, ab_w_out,
           c_w_in, c_sink, c_w_out, final_g):
    Bn, S, D = x.shape
    depth = norm_g.shape[0]
    w1, w3, w2 = (w.astype(BF16) for w in (ffn_w1, ffn_w3, ffn_w2))
    x = x.reshape(Bn * S, D)
    for l in range(depth):
        j = l // 2
        if l % 2 == 0:
            x = _ffn(x, norm_g[l, 0], w1, w3, w2, (l, 0))
            mixed = _mixer_ab(x, norm_g[l, 1], ab_w_in[j], ab_conv_w[j], ab_conv_b[j], ab_gate_b[j], ab_hnorm_g[j],
                              ab_w_out[j], Bn, S)
        else:
            x, q, k, v = _ffn(x, norm_g[l, 0], w1, w3, w2, (l, 0), proj=_c_in_proj(norm_g[l, 1], c_w_in[j]))
            mixed = _mixer_c(q, k, v, c_sink[j], c_w_out[j], Bn, S)
        x = _ffn(x, norm_g[l, 2], w1, w3, w2, (l, 1), final_g if l == depth - 1 else None, mixed)
    return x.reshape(Bn, S, D)
```

```python
import functools

import jax
import jax.numpy as jnp
from jax import lax
from jax.experimental import pallas as pl
from jax.experimental.pallas import tpu as pltpu

F32 = jnp.float32
BF16 = jnp.bfloat16

EPS = 1e-6
M_HEADS = 4
M_HEAD_DIM = 128
M_WIDTH = M_HEADS * M_HEAD_DIM
M_CONV = 5
B_HEADS = 8
B_WIDTH = 512
DILATIONS = (1, 4, 16)
B_RADIUS = 64
C_HEADS = 16
C_KV_HEADS = 4
C_GROUP = C_HEADS // C_KV_HEADS
C_RADIUS = 128
HEAD_DIM = 64
LANES = 128
SUBLANES = 8
LOG2E = 1.4426950408889634
ATTN_Q_SCALE = HEAD_DIM ** -0.5 * LOG2E

VMEM_LIMIT_BYTES = 56 * 1024 * 1024
ROW_TILE = 512
B_Q_BLOCK, B_CHAINS = 128, 128
C_Q_BLOCK, C_CHAINS = 128, 128
MLSTM_CHUNK = 256
CONV_COLS = 256
MLSTM_HEADS_PER_STEP = 2


def _cparams(*sem):
    return pltpu.CompilerParams(dimension_semantics=sem, vmem_limit_bytes=VMEM_LIMIT_BYTES)


def _resident(shape, index_map):
    return pl.BlockSpec(shape, index_map, pipeline_mode=pl.Buffered(1))


def _rms(x, g):
    return x * lax.rsqrt(jnp.mean(x * x, axis=-1, keepdims=True) + EPS) * g


def _mix_c(a_ref, w_ref):
    return jnp.dot(a_ref[...], w_ref[...], preferred_element_type=F32)


def _mix_ab(hn_ref, om_ref, o1_ref, l1_ref, o4_ref, l4_ref, o16_ref, l16_ref, wa_ref, wb_ref,
            so4_ref, sl4_ref, so16_ref, sl16_ref):
    tm = hn_ref.shape[0]
    n_col = B_WIDTH // LANES
    for d, pairs in ((DILATIONS[1], ((o4_ref, so4_ref), (l4_ref, sl4_ref))),
                     (DILATIONS[2], ((o16_ref, so16_ref), (l16_ref, sl16_ref)))):
        for src_ref, dst_ref in pairs:
            for r in range(d):
                for c in range(n_col):
                    lo = (c * d + r) * LANES
                    dst_ref[c, pl.ds(r, tm // d, stride=d), :] = src_ref[:, lo:lo + LANES].astype(F32)
    cols = []
    for c in range(n_col):
        lanes = slice(c * LANES, (c + 1) * LANES)
        l1, l2, l3 = l1_ref[:, lanes], sl4_ref[c], sl16_ref[c]
        lm = jnp.maximum(jnp.maximum(l1, l2), l3)
        e1, e2, e3 = jnp.exp2(l1 - lm), jnp.exp2(l2 - lm), jnp.exp2(l3 - lm)
        ob = (e1 * o1_ref[:, lanes].astype(F32) + e2 * so4_ref[c] + e3 * so16_ref[c]) / (e1 + e2 + e3)
        cols.append(ob.astype(BF16))
    ob = jnp.concatenate(cols, axis=1)
    ma = (hn_ref[...].astype(F32) * jax.nn.sigmoid(om_ref[...].astype(F32))).astype(BF16)
    return (jnp.dot(ma, wa_ref[...], preferred_element_type=F32)
            + jnp.dot(ob, wb_ref[...], preferred_element_type=F32))


def _ffn_kernel(x_ref, g_ref, w1_ref, w3_ref, w2_ref, *rest, final_norm, mix, n_mix_in, n_scratch, proj_segs,
                n_cast):
    mix_in = rest[:n_mix_in]
    rest = rest[n_mix_in:]
    n_in = final_norm + (2 if proj_segs else 0)
    cast_in = rest[n_in:n_in + n_cast]
    scratch = rest[len(rest) - n_scratch:]
    cast_out = rest[len(rest) - n_scratch - n_cast:len(rest) - n_scratch]
    proj_outs = rest[len(rest) - n_scratch - n_cast - len(proj_segs):len(rest) - n_scratch - n_cast]
    o_ref = rest[len(rest) - n_scratch - n_cast - len(proj_segs) - 1]
    for src_ref, dst_ref in zip(cast_in, cast_out):
        dst_ref[...] = src_ref[...].astype(BF16)
    x = x_ref[...]
    if mix is not None:
        x = x + mix(*mix_in, *scratch)
    h = _rms(x, g_ref[...]).astype(BF16)
    a = jnp.dot(h, w1_ref[...], preferred_element_type=F32)
    b = jnp.dot(h, w3_ref[...], preferred_element_type=F32)
    act = (a * jax.nn.sigmoid(a) * b).astype(BF16)
    y = x + 0.5 * jnp.dot(act, w2_ref[...], preferred_element_type=F32)
    if final_norm:
        y = _rms(y, rest[0][...])
    o_ref[...] = y
    if proj_segs:
        pg_ref, pw_ref = rest[final_norm:final_norm + 2]
        h2 = _rms(y, pg_ref[...]).astype(BF16)
        off = 0
        for (width, scale), p_ref in zip(proj_segs, proj_outs):
            z = jnp.dot(h2, pw_ref[:, off:off + width], preferred_element_type=F32)
            p_ref[...] = (z * scale if scale != 1.0 else z).astype(p_ref.dtype)
            off += width


def _cast_blocks(n_rows, n_steps):
    n_blk = max(n for n in range(1, n_steps + 1) if n_steps % n == 0 and n_rows % (16 * n) == 0)
    return n_blk, n_rows // n_blk


def _ffn(x, g, weights, final_g=None, mixer=None, proj=None, cast_next=None):
    T, D = x.shape
    tm = ROW_TILE
    n_steps = T // tm
    w1, w3, w2 = weights
    row = lambda w: pl.BlockSpec((tm, w), lambda i: (i, 0))
    res = lambda d: pl.BlockSpec((tm // d, d * B_WIDTH), lambda i: (i, 0))
    full = lambda a: _resident(a.shape, lambda i: (0, 0))
    g = g.reshape(1, D)
    in_specs = [row(D), full(g), full(w1), full(w3), full(w2)]
    args = [x, g, w1, w3, w2]
    mix, n_mix_in, scratch = None, 0, []
    if mixer is not None and mixer[0] == "c":
        _, att, w_out = mixer
        mix, mix_args, mix_specs = _mix_c, [att, w_out], [row(att.shape[1]), full(w_out)]
    elif mixer is not None:
        _, hn, om, outs, lses, wa, wb = mixer
        d4, d16 = DILATIONS[1:]
        mix = _mix_ab
        mix_args = [hn, om, outs[0], lses[0], outs[1], lses[1], outs[2], lses[2], wa, wb]
        mix_specs = [row(M_WIDTH)] * 4 + [res(d4)] * 2 + [res(d16)] * 2 + [full(wa), full(wb)]
        scratch = [pltpu.VMEM((B_WIDTH // LANES, tm, LANES), F32)] * 4
    if mix is not None:
        n_mix_in = len(mix_args)
        in_specs += mix_specs
        args += mix_args
    if final_g is not None:
        final_g = final_g.reshape(1, D)
        in_specs.append(full(final_g))
        args.append(final_g)
    out_specs, out_shape, proj_segs = [row(D)], [jax.ShapeDtypeStruct((T, D), F32)], ()
    if proj is not None:
        pg, pw, segs = proj
        pg = pg.reshape(1, D)
        in_specs += [full(pg), full(pw)]
        args += [pg, pw]
        proj_segs = tuple((wd, sc) for wd, _, sc in segs)
        out_specs += [row(wd) for wd, _, _ in segs]
        out_shape += [jax.ShapeDtypeStruct((T, wd), dt) for wd, dt, _ in segs]
    n_cast = 0
    if cast_next is not None:
        *stacks, (layer, half) = cast_next
        n_cast = len(stacks)
        for w in stacks:
            n_rows, n_cols = w.shape[2:]
            n_blk, blk_rows = _cast_blocks(n_rows, n_steps)
            in_specs.append(pl.BlockSpec((None, None, blk_rows, n_cols),
                                         lambda i, n_blk=n_blk: (layer, half, jnp.minimum(i, n_blk - 1), 0)))
            args.append(w)
            out_specs.append(pl.BlockSpec((blk_rows, n_cols), lambda i, n_blk=n_blk: (jnp.minimum(i, n_blk - 1), 0)))
            out_shape.append(jax.ShapeDtypeStruct((n_rows, n_cols), BF16))
    res_out = pl.pallas_call(
        functools.partial(_ffn_kernel, final_norm=final_g is not None, mix=mix, n_mix_in=n_mix_in,
                          n_scratch=len(scratch), proj_segs=proj_segs, n_cast=n_cast),
        grid=(n_steps,), in_specs=in_specs, out_specs=out_specs, out_shape=out_shape, scratch_shapes=scratch,
        compiler_params=_cparams("arbitrary"), name="ffn")(*args)
    n_proj = len(proj_segs)
    return res_out[0], tuple(res_out[1:1 + n_proj]), tuple(res_out[1 + n_proj:])


def _proj_ab_kernel(x_ref, xp_ref, xn_ref, g_ref, wqk_ref, wr_ref, wg_ref, cw_ref, cb_ref,
                    qa_ref, ka_ref, vm_ref, om_ref, gr_ref, *rest, tiles_per_seq):
    tok_refs = rest[0:3]
    res_refs = {d: rest[3 * (n + 1):3 * (n + 2)] for n, d in enumerate(DILATIONS[1:])}
    stage_ref = rest[-1]
    tm = x_ref.shape[0]
    halo = SUBLANES
    g = g_ref[...]
    i = pl.program_id(0)
    first = i % tiles_per_seq == 0
    last = i % tiles_per_seq == tiles_per_seq - 1

    x = x_ref[...]
    h = _rms(x, g).astype(BF16)
    x_ext = jnp.concatenate([xp_ref[...], x, xn_ref[...]], axis=0)
    h_ext = _rms(x_ext, g).astype(BF16)
    n_ext = tm + 2 * halo
    row = lax.broadcasted_iota(jnp.int32, (n_ext, 1), 0)
    outside = ((row < halo) & first) | ((row >= halo + tm) & last)
    n_conv = 2 * M_WIDTH // CONV_COLS
    qk_blocks = [jnp.dot(h_ext, wqk_ref[:, blk * CONV_COLS:(blk + 1) * CONV_COLS], preferred_element_type=F32)
                 for blk in range(n_conv)]
    y_vm = jnp.dot(h, wr_ref[:, 0:M_WIDTH], preferred_element_type=F32)
    y_om = jnp.dot(h, wr_ref[:, M_WIDTH:2 * M_WIDTH], preferred_element_type=F32)
    y_g = lax.dot_general(wg_ref[...], h, (((1,), (1,)), ((), ())), preferred_element_type=F32)
    ys = [jnp.dot(h, wr_ref[:, 2 * M_WIDTH + s * B_WIDTH:2 * M_WIDTH + (s + 1) * B_WIDTH],
                  preferred_element_type=F32) for s in range(len(tok_refs))]

    for blk in range(n_conv):
        cols = slice(blk * CONV_COLS, (blk + 1) * CONV_COLS)
        qk = jnp.where(outside, 0.0, qk_blocks[blk])
        acc = cb_ref[:, cols] + qk[halo:halo + tm] * cw_ref[M_CONV // 2:M_CONV // 2 + 1, cols]
        for j in range(M_CONV):
            off = j - M_CONV // 2
            if off != 0:
                shifted = pltpu.roll(qk, (-off) % n_ext, axis=0)
                acc = acc + shifted[halo:halo + tm] * cw_ref[j:j + 1, cols]
        act = acc * jax.nn.sigmoid(acc)
        if blk * CONV_COLS < M_WIDTH:
            qa_ref[:, cols] = act.astype(BF16)
        else:
            kcols = slice(blk * CONV_COLS - M_WIDTH, (blk + 1) * CONV_COLS - M_WIDTH)
            ka_ref[:, kcols] = (act * (M_HEAD_DIM ** -0.5)).astype(BF16)

    vm_ref[...] = y_vm.astype(BF16)
    om_ref[...] = y_om.astype(BF16)
    gr_ref[...] = y_g

    n_col = B_WIDTH // LANES
    for s, tok_ref in enumerate(tok_refs):
        y = ys[s] * ATTN_Q_SCALE if s == 0 else ys[s]
        tok_ref[...] = y.astype(BF16)
        for c in range(n_col):
            stage_ref[s, c] = y[:, c * LANES:(c + 1) * LANES]
        for d, refs in res_refs.items():
            for r in range(d):
                for c in range(n_col):
                    lo = (c * d + r) * LANES
                    refs[s][:, lo:lo + LANES] = stage_ref[s, c, pl.ds(r, tm // d, stride=d), :].astype(BF16)


def _proj_ab(x, g, w_qk, w_rest, wg_t, conv_w, conv_b, seq_len):
    T, D = x.shape
    tm = ROW_TILE
    halo = SUBLANES
    hb = tm // halo
    row = lambda w: pl.BlockSpec((tm, w), lambda i: (i, 0))
    res = lambda d: pl.BlockSpec((tm // d, d * B_WIDTH), lambda i: (i, 0))
    full = lambda a: _resident(a.shape, lambda i: (0, 0))
    conv_b = conv_b.reshape(1, -1)
    g = g.reshape(1, D)
    out_specs = [row(M_WIDTH)] * 4 + [pl.BlockSpec((wg_t.shape[0], tm), lambda i: (0, i))] + [row(B_WIDTH)] * 3
    out_shape = ([jax.ShapeDtypeStruct((T, M_WIDTH), BF16)] * 4
                 + [jax.ShapeDtypeStruct((wg_t.shape[0], T), F32)]
                 + [jax.ShapeDtypeStruct((T, B_WIDTH), BF16)] * 3)
    for d in DILATIONS[1:]:
        out_specs += [res(d)] * 3
        out_shape += [jax.ShapeDtypeStruct((T // d, d * B_WIDTH), BF16)] * 3
    return pl.pallas_call(
        functools.partial(_proj_ab_kernel, tiles_per_seq=seq_len // tm),
        grid=(T // tm,),
        in_specs=[row(D),
                  pl.BlockSpec((halo, D), lambda i: (jnp.maximum(i * hb - 1, 0), 0)),
                  pl.BlockSpec((halo, D), lambda i: (jnp.minimum((i + 1) * hb, T // halo - 1), 0)),
                  full(g), full(w_qk), full(w_rest), full(wg_t), full(conv_w), full(conv_b)],
        out_specs=out_specs, out_shape=out_shape,
        scratch_shapes=[pltpu.VMEM((3, B_WIDTH // LANES, tm, LANES), F32)],
        compiler_params=_cparams("parallel"), name="proj_ab")(x, x, x, g, w_qk, w_rest, wg_t, conv_w, conv_b)


def _attn_kernel(*refs, groups, kv_tiles, q_tile, q_block, k_window, radius, dist_unit, seq_len, has_sink,
                 with_lse):
    slope_ref = refs[0]
    pos = 1
    sink_ref = None
    if has_sink:
        sink_ref = refs[pos]
        pos += 1
    q_ref, k_ref, v_ref, o_ref = refs[pos:pos + 4]
    pos += 4
    lse_ref = None
    if with_lse:
        lse_ref = refs[pos]
        pos += 1
    bias_ref = refs[pos]
    pair = pl.program_id(0)
    tile = pl.program_id(3)

    shift_step = min(q_block, radius)

    @pl.when((pl.program_id(1) == 0) & (pl.program_id(2) == 0) & (tile == 0))
    def _():
        sub = lax.broadcasted_iota(jnp.int32, (q_block, k_window), 0)
        ln = lax.broadcasted_iota(jnp.int32, (q_block, k_window), 1)
        for variant in range(2 * radius // shift_step + 1):
            adist = jnp.abs(ln - sub - variant * shift_step)
            negd = jnp.where(adist <= radius, -(dist_unit * adist).astype(F32), -jnp.inf)
            for g in range(groups):
                for j in range(2):
                    bias_ref[variant, 2 * g + j] = (slope_ref[(2 * pair + j) * groups + g] * LOG2E) * negd

    lane = lax.broadcasted_iota(jnp.int32, (q_block, LANES), 1)
    low = lane < HEAD_DIM
    ones = jnp.ones((k_window, LANES), BF16)
    chains = []
    for blk in range(q_tile // q_block):
        q0 = tile * q_tile + blk * q_block
        ks = pl.multiple_of(jnp.clip(q0 - radius, 0, seq_len - k_window), HEAD_DIM)
        variant = (q0 - ks) // shift_step
        rows = slice(blk * q_block, (blk + 1) * q_block)
        for t in range(kv_tiles):
            kt = k_ref[pl.ds(ks, k_window), t * LANES:(t + 1) * LANES]
            vt = jnp.concatenate([v_ref[pl.ds(ks, k_window), t * LANES:(t + 1) * LANES], ones], axis=1)
            for g in range(groups):
                qt = q_ref[rows, (t * groups + g) * LANES:(t * groups + g + 1) * LANES]
                for j in range(2):
                    qm = jnp.where(low if j == 0 else jnp.logical_not(low), qt, jnp.zeros_like(qt))
                    chains.append(dict(qm=qm, kt=kt, vt=vt, variant=variant, slot=2 * g + j,
                                       head=(2 * pair + j) * groups + g))
    s = [lax.dot_general(c["qm"], c["kt"], (((1,), (1,)), ((), ())), preferred_element_type=F32)
         + bias_ref[c["variant"], c["slot"]] for c in chains]
    m = [jnp.max(x, axis=-1, keepdims=True) for x in s]
    if has_sink:
        m = [jnp.maximum(x, sink_ref[c["head"]] * LOG2E) for c, x in zip(chains, m)]
    p = [jnp.exp2(x - y).astype(BF16) for x, y in zip(s, m)]
    od = [jnp.dot(x, c["vt"], preferred_element_type=F32) for c, x in zip(chains, p)]
    den = [x[:, LANES:] for x in od]
    if has_sink:
        den = [x + jnp.exp2(sink_ref[c["head"]] * LOG2E - y) for c, x, y in zip(chains, den, m)]
    o = [x[:, :LANES] / y for x, y in zip(od, den)]
    lse = [x + jnp.log2(y) for x, y in zip(m, den)] if with_lse else None
    for n in range(0, len(chains), 2):
        blk, tile_col = divmod(n // 2, kv_tiles * groups)
        rows = slice(blk * q_block, (blk + 1) * q_block)
        cols = slice(tile_col * LANES, (tile_col + 1) * LANES)
        o_ref[rows, cols] = jnp.where(low, o[n], o[n + 1]).astype(o_ref.dtype)
        if with_lse:
            lse_ref[rows, cols] = jnp.where(low, lse[n], lse[n + 1])


def _banded_attention(qv, kv, vv, slopes, sink, *, dilation, groups, radius, with_lse, out_dtype, q_block, chains):
    Bn, Lv, _ = qv.shape
    d = dilation
    pairs = kv.shape[2] // (d * LANES)
    q_tile = min(chains // (2 * groups) * q_block, Lv)
    kv_tiles = min(d, max(1, chains // (2 * groups * (q_tile // q_block))))
    k_window = q_block + 2 * radius
    shift_step = min(q_block, radius)
    assert max(q_block, radius) % shift_step == 0 and Lv % q_tile == 0 and k_window <= Lv
    assert d % kv_tiles == 0
    rsteps = d // kv_tiles
    smem = pl.BlockSpec(memory_space=pltpu.SMEM)
    qspec = pl.BlockSpec((None, q_tile, kv_tiles * groups * LANES), lambda p, b, r, i: (b, i, p * rsteps + r))
    kspec = pl.BlockSpec((None, Lv, kv_tiles * LANES), lambda p, b, r, i: (b, 0, p * rsteps + r))
    in_specs = [smem] + ([smem] if sink is not None else []) + [qspec, kspec, kspec]
    args = [slopes] + ([sink] if sink is not None else []) + [qv, kv, vv]
    out_specs = [qspec]
    out_shape = [jax.ShapeDtypeStruct(qv.shape, out_dtype)]
    if with_lse:
        out_specs.append(qspec)
        out_shape.append(jax.ShapeDtypeStruct(qv.shape, F32))
    return pl.pallas_call(
        functools.partial(_attn_kernel, groups=groups, kv_tiles=kv_tiles, q_tile=q_tile, q_block=q_block,
                          k_window=k_window,
                          radius=radius, dist_unit=d, seq_len=Lv, has_sink=sink is not None, with_lse=with_lse),
        grid=(pairs, Bn, rsteps, Lv // q_tile), in_specs=in_specs, out_specs=out_specs, out_shape=out_shape,
        scratch_shapes=[pltpu.VMEM((2 * radius // shift_step + 1, 2 * groups, q_block, k_window), F32)],
        compiler_params=_cparams("arbitrary", "arbitrary", "arbitrary", "arbitrary"),
        name=f"banded_attention_d{d}")(*args)


def _alibi_slopes(n):
    return jnp.exp2(-8.0 * jnp.arange(1, n + 1, dtype=F32) / n)


def _log_sigmoid(x):
    return jnp.minimum(x, 0.0) - jnp.log1p(jnp.exp(-jnp.abs(x)))


def _mlstm_chunks(chains, seen, diag):
    nt = (((1,), (1,)), ((), ()))
    E = chains[0]["q"].shape[1]
    mask = [seen[c["rev"]] for c in chains]
    b_c = [jnp.sum(jnp.where(mk, c["lf_r"], 0.0), axis=1, keepdims=True) for c, mk in zip(chains, mask)]
    qk = [lax.dot_general(c["q"], c["k"], nt, preferred_element_type=F32) for c in chains]
    inter = [jnp.dot(c["q"], c["ct"].astype(BF16), preferred_element_type=F32) for c in chains]
    k_t = [c["k"].astype(F32).T for c in chains]
    b_last = [jnp.sum(c["lf_r"], axis=1, keepdims=True) for c in chains]
    b_r = [jnp.sum(jnp.where(diag, x, 0.0), axis=0, keepdims=True) for x in b_c]
    u_r = [c["i_r"] - x for c, x in zip(chains, b_r)]
    um = [jnp.where(mk, x, -jnp.inf) for mk, x in zip(mask, u_r)]
    g_t = [jnp.maximum(c["m"], jnp.max(x, axis=1, keepdims=True)) for c, x in zip(chains, um)]
    a_r = [x + y for x, y in zip(b_last, u_r)]
    m_new = [jnp.maximum(x + c["m"], jnp.max(y, axis=1, keepdims=True)) for c, x, y in zip(chains, b_last, a_r)]
    kw_t = [(x * jnp.exp2(y - z)).astype(BF16) for x, y, z in zip(k_t, a_r, m_new)]
    ct = [jnp.exp2(x + c["m"] - z) * c["ct"] + jnp.dot(y, c["v1"], preferred_element_type=F32)
          for c, x, y, z in zip(chains, b_last, kw_t, m_new)]
    w = [(jnp.exp2(x - y) * z).astype(BF16) for x, y, z in zip(um, g_t, qk)]
    tot = [jnp.dot(x, c["v1"], preferred_element_type=F32) + jnp.exp2(c["m"] - y) * z
           for c, x, y, z in zip(chains, w, g_t, inter)]
    h = [x[:, :E] / jnp.maximum(jnp.abs(x[:, E:]), jnp.exp2(-(y + z))) for x, y, z in zip(tot, b_c, g_t)]
    return list(zip(h, ct, m_new))


def _mlstm_kernel(q_ref, k_ref, v_ref, gr_ref, gb_ref, hg_ref, o_ref, acc_ref, *, seq_len, chunk, heads):
    S, L, E = seq_len, chunk, M_HEAD_DIM
    n_chunks = S // L
    row = lax.broadcasted_iota(jnp.int32, (L, L), 0)
    col = lax.broadcasted_iota(jnp.int32, (L, L), 1)
    seen = (col <= row, col >= row)
    diag = col == row
    ones = jnp.ones((L, E), BF16)

    def finish(tot, lanes):
        return tot * lax.rsqrt(jnp.mean(tot * tot, axis=-1, keepdims=True) + EPS) * hg_ref[:, lanes]

    def sweep(second):
        def body(it, states):
            chains, where = [], []
            for n, (ct, m) in enumerate(states):
                hh, rev = divmod(n, 2)
                c = it if rev == 0 else n_chunks - 1 - it
                rows = pl.ds(pl.multiple_of(c * L, L), L)
                lanes = slice(hh * E, (hh + 1) * E)
                g = gr_ref[4 * hh:4 * hh + 4, rows] + gb_ref[4 * hh:4 * hh + 4, :]
                where.append((rows, lanes, acc_ref[rows, lanes] if second else None))
                chains.append(dict(q=q_ref[rows, lanes], k=k_ref[rows, lanes],
                                   v1=jnp.concatenate([v_ref[rows, lanes], ones], axis=1),
                                   i_r=g[rev:rev + 1] * LOG2E, lf_r=_log_sigmoid(g[2 + rev:3 + rev]) * LOG2E,
                                   ct=ct, m=m, rev=rev))
            out = _mlstm_chunks(chains, seen, diag)
            for (rows, lanes, prev), (h, _, _) in zip(where, out):
                if second:
                    o_ref[rows, lanes] = finish(prev + h, lanes).astype(o_ref.dtype)
                else:
                    acc_ref[rows, lanes] = h
            return tuple((ct, m) for _, ct, m in out)
        return body

    zero = (jnp.zeros((E, 2 * E), F32), jnp.zeros((1, 1), F32))
    states = lax.fori_loop(0, n_chunks // 2, sweep(False), (zero,) * (2 * heads))
    lax.fori_loop(n_chunks // 2, n_chunks, sweep(True), states)


def _mlstm(q, k, v, g_rows, gate_b, hnorm_g, Bn, S):
    E = M_HEAD_DIM
    T = Bn * S
    hs = MLSTM_HEADS_PER_STEP
    groups = M_HEADS // hs
    seq = pl.BlockSpec((None, S, hs * E), lambda b, h: (b, 0, h))
    q3, k3, v3 = (t.reshape(Bn, S, M_WIDTH) for t in (q, k, v))
    out = pl.pallas_call(
        functools.partial(_mlstm_kernel, seq_len=S, chunk=MLSTM_CHUNK, heads=hs),
        grid=(Bn, groups),
        in_specs=[seq, seq, seq,
                  pl.BlockSpec((None, 4 * hs, S), lambda b, h: (h, 0, b)),
                  pl.BlockSpec((None, 4 * hs, 1), lambda b, h: (h, 0, 0)),
                  pl.BlockSpec((1, hs * E), lambda b, h: (0, h))],
        out_specs=seq,
        out_shape=jax.ShapeDtypeStruct((Bn, S, M_WIDTH), BF16),
        scratch_shapes=[pltpu.VMEM((S, hs * E), F32)],
        compiler_params=_cparams("parallel", "parallel"), name="mlstm")(
            q3, k3, v3, g_rows.reshape(groups, 4 * hs, T), gate_b.reshape(groups, 4 * hs, 1),
            hnorm_g.reshape(1, M_WIDTH))
    return out.reshape(T, M_WIDTH)


def _mixer_ab(x, g, w_in, conv_w, conv_b, gate_b, hnorm_g, w_out, Bn, S):
    T = Bn * S
    MW, BW = M_WIDTH, B_WIDTH
    g0 = 4 * MW
    n_gate = 4 * M_HEADS
    w_qk = w_in[:, :2 * MW].astype(BF16)
    w_rest = jnp.concatenate([w_in[:, 2 * MW:g0], w_in[:, g0 + n_gate:]], axis=1).astype(BF16)
    wg_t = w_in[:, g0:g0 + n_gate].reshape(-1, 4, M_HEADS).transpose(2, 1, 0).reshape(n_gate, -1).astype(BF16)
    gate_b = gate_b.reshape(4, M_HEADS).T.astype(F32)
    (q_m, k_m, v_m, o_m, g_rows, q1, k1, v1, q4, k4, v4, q16, k16, v16) = _proj_ab(
        x, g, w_qk, w_rest, wg_t, conv_w, conv_b, S)

    hn = _mlstm(q_m, k_m, v_m, g_rows, gate_b, hnorm_g, Bn, S)

    slopes = _alibi_slopes(B_HEADS)
    outs, lses = [], []
    for d, qkv in zip(DILATIONS, ((q1, k1, v1), (q4, k4, v4), (q16, k16, v16))):
        qv, kv, vv = (t.reshape(Bn, S // d, d * BW) for t in qkv)
        o, lse = _banded_attention(qv, kv, vv, slopes, None, dilation=d, groups=1, radius=B_RADIUS,
                                   with_lse=True, out_dtype=BF16, q_block=B_Q_BLOCK, chains=B_CHAINS)
        outs.append(o.reshape(T // d, d * BW))
        lses.append(lse.reshape(T // d, d * BW))
    w_out = w_out.astype(BF16)
    return ("ab", hn, o_m, outs, lses, w_out[:MW], w_out[MW:])


def _c_head_order():
    order = []
    for pair in range(C_KV_HEADS // 2):
        for slot in range(C_GROUP):
            for j in range(2):
                order.append((2 * pair + j) * C_GROUP + slot)
    return jnp.array(order, dtype=jnp.int32)


def _c_in_proj(g, w_in):
    D = w_in.shape[0]
    QW = C_HEADS * HEAD_DIM
    KW = C_KV_HEADS * HEAD_DIM
    wq = w_in[:, :QW].reshape(D, C_HEADS, HEAD_DIM)[:, _c_head_order()].reshape(D, QW)
    w_main = jnp.concatenate([wq, w_in[:, QW:]], axis=1).astype(BF16)
    return g, w_main, [(QW, BF16, ATTN_Q_SCALE), (KW, BF16, 1.0), (KW, BF16, 1.0)]


def _mixer_c(q, k, v, sink, w_out, Bn, S):
    T, QW = q.shape
    KW = k.shape[1]
    D = w_out.shape[1]
    w_out = w_out.reshape(C_HEADS, HEAD_DIM, D)[_c_head_order()].reshape(QW, D).astype(BF16)
    (o,) = _banded_attention(q.reshape(Bn, S, QW), k.reshape(Bn, S, KW), v.reshape(Bn, S, KW),
                             _alibi_slopes(C_HEADS), sink.astype(F32), dilation=1, groups=C_GROUP,
                             radius=C_RADIUS, with_lse=False, out_dtype=BF16, q_block=C_Q_BLOCK, chains=C_CHAINS)
    return ("c", o.reshape(T, QW), w_out)


def kernel(x, norm_g, ffn_w1, ffn_w3, ffn_w2, ab_w_in, ab_conv_w, ab_conv_b, ab_gate_b, ab_hnorm_g, ab_w_out,
           c_w_in, c_sink, c_w_out, final_g):
    Bn, S, D = x.shape
    depth = norm_g.shape[0]
    stacks = (ffn_w1, ffn_w3, ffn_w2)
    weights = tuple(w[0, 0].astype(BF16) for w in stacks)
    x = x.reshape(Bn * S, D)
    for l in range(depth):
        j = l // 2
        proj = _c_in_proj(norm_g[l, 1], c_w_in[j]) if l % 2 else None
        x, qkv, weights = _ffn(x, norm_g[l, 0], weights, proj=proj, cast_next=stacks + ((l, 1),))
        if l % 2 == 0:
            mixed = _mixer_ab(x, norm_g[l, 1], ab_w_in[j], ab_conv_w[j], ab_conv_b[j], ab_gate_b[j], ab_hnorm_g[j],
                              ab_w_out[j], Bn, S)
        else:
            mixed = _mixer_c(*qkv, c_sink[j], c_w_out[j], Bn, S)
        last = l == depth - 1
        x, _, weights = _ffn(x, norm_g[l, 2], weights, final_g if last else None, mixed,
                             cast_next=None if last else stacks + ((l + 1, 0),))
    return x.reshape(Bn, S, D)
```

```python
import functools

import jax
import jax.numpy as jnp
from jax import lax
from jax.experimental import pallas as pl
from jax.experimental.pallas import tpu as pltpu

F32 = jnp.float32
BF16 = jnp.bfloat16

EPS = 1e-6
M_HEADS = 4
M_HEAD_DIM = 128
M_WIDTH = M_HEADS * M_HEAD_DIM
M_CONV = 5
B_HEADS = 8
B_WIDTH = 512
DILATIONS = (1, 4, 16)
B_RADIUS = 64
C_HEADS = 16
C_KV_HEADS = 4
C_GROUP = C_HEADS // C_KV_HEADS
C_RADIUS = 128
HEAD_DIM = 64
LANES = 128
SUBLANES = 8
LOG2E = 1.4426950408889634
ATTN_Q_SCALE = HEAD_DIM ** -0.5 * LOG2E

VMEM_LIMIT_BYTES = 56 * 1024 * 1024
ROW_TILE = 512
PROJ_AB_ROW_TILE = 256
B_Q_BLOCK, B_CHAINS = 128, 128
C_Q_BLOCK, C_CHAINS = 128, 128
MLSTM_CHUNK = 256
CONV_COLS = 256
MLSTM_HEADS_PER_STEP = 2


def _cparams(*sem):
    return pltpu.CompilerParams(dimension_semantics=sem, vmem_limit_bytes=VMEM_LIMIT_BYTES)


def _resident(shape, index_map):
    return pl.BlockSpec(shape, index_map, pipeline_mode=pl.Buffered(1))


def _rms(x, g):
    return x * lax.rsqrt(jnp.mean(x * x, axis=-1, keepdims=True) + EPS) * g


def _mix_c(a_ref, w_ref):
    return jnp.dot(a_ref[...], w_ref[...], preferred_element_type=F32)


def _mix_ab(hn_ref, om_ref, o1_ref, l1_ref, o4_ref, l4_ref, o16_ref, l16_ref, wa_ref, wb_ref,
            so4_ref, sl4_ref, so16_ref, sl16_ref):
    tm = hn_ref.shape[0]
    n_col = B_WIDTH // LANES
    for d, pairs in ((DILATIONS[1], ((o4_ref, so4_ref), (l4_ref, sl4_ref))),
                     (DILATIONS[2], ((o16_ref, so16_ref), (l16_ref, sl16_ref)))):
        for src_ref, dst_ref in pairs:
            for r in range(d):
                for c in range(n_col):
                    lo = (c * d + r) * LANES
                    dst_ref[c, pl.ds(r, tm // d, stride=d), :] = src_ref[:, lo:lo + LANES].astype(F32)
    cols = []
    for c in range(n_col):
        lanes = slice(c * LANES, (c + 1) * LANES)
        l1, l2, l3 = l1_ref[:, lanes], sl4_ref[c], sl16_ref[c]
        lm = jnp.maximum(jnp.maximum(l1, l2), l3)
        e1, e2, e3 = jnp.exp2(l1 - lm), jnp.exp2(l2 - lm), jnp.exp2(l3 - lm)
        ob = (e1 * o1_ref[:, lanes].astype(F32) + e2 * so4_ref[c] + e3 * so16_ref[c]) / (e1 + e2 + e3)
        cols.append(ob.astype(BF16))
    ob = jnp.concatenate(cols, axis=1)
    ma = (hn_ref[...].astype(F32) * jax.nn.sigmoid(om_ref[...].astype(F32))).astype(BF16)
    return (jnp.dot(ma, wa_ref[...], preferred_element_type=F32)
            + jnp.dot(ob, wb_ref[...], preferred_element_type=F32))


def _ffn_kernel(x_ref, g_ref, w1_ref, w3_ref, w2_ref, *rest, final_norm, mix, n_mix_in, n_scratch, proj_segs,
                n_cast):
    mix_in = rest[:n_mix_in]
    rest = rest[n_mix_in:]
    n_in = final_norm + (2 if proj_segs else 0)
    cast_in = rest[n_in:n_in + n_cast]
    scratch = rest[len(rest) - n_scratch:]
    cast_out = rest[len(rest) - n_scratch - n_cast:len(rest) - n_scratch]
    proj_outs = rest[len(rest) - n_scratch - n_cast - len(proj_segs):len(rest) - n_scratch - n_cast]
    o_ref = rest[len(rest) - n_scratch - n_cast - len(proj_segs) - 1]
    for src_ref, dst_ref in zip(cast_in, cast_out):
        dst_ref[...] = src_ref[...].astype(BF16)
    x = x_ref[...]
    if mix is not None:
        x = x + mix(*mix_in, *scratch)
    h = _rms(x, g_ref[...]).astype(BF16)
    a = jnp.dot(h, w1_ref[...], preferred_element_type=F32)
    b = jnp.dot(h, w3_ref[...], preferred_element_type=F32)
    act = (a * jax.nn.sigmoid(a) * b).astype(BF16)
    y = x + 0.5 * jnp.dot(act, w2_ref[...], preferred_element_type=F32)
    if final_norm:
        y = _rms(y, rest[0][...])
    o_ref[...] = y
    if proj_segs:
        pg_ref, pw_ref = rest[final_norm:final_norm + 2]
        h2 = _rms(y, pg_ref[...]).astype(BF16)
        off = 0
        for (width, scale), p_ref in zip(proj_segs, proj_outs):
            z = jnp.dot(h2, pw_ref[:, off:off + width], preferred_element_type=F32)
            p_ref[...] = (z * scale if scale != 1.0 else z).astype(p_ref.dtype)
            off += width


def _cast_blocks(n_rows, n_steps):
    n_blk = max(n for n in range(1, n_steps + 1) if n_steps % n == 0 and n_rows % (16 * n) == 0)
    return n_blk, n_rows // n_blk


def _ffn(x, g, weights, final_g=None, mixer=None, proj=None, cast_next=None):
    T, D = x.shape
    tm = ROW_TILE
    n_steps = T // tm
    w1, w3, w2 = weights
    row = lambda w: pl.BlockSpec((tm, w), lambda i: (i, 0))
    res = lambda d: pl.BlockSpec((tm // d, d * B_WIDTH), lambda i: (i, 0))
    full = lambda a: _resident(a.shape, lambda i: (0, 0))
    g = g.reshape(1, D)
    in_specs = [row(D), full(g), full(w1), full(w3), full(w2)]
    args = [x, g, w1, w3, w2]
    mix, n_mix_in, scratch = None, 0, []
    if mixer is not None and mixer[0] == "c":
        _, att, w_out = mixer
        mix, mix_args, mix_specs = _mix_c, [att, w_out], [row(att.shape[1]), full(w_out)]
    elif mixer is not None:
        _, hn, om, outs, lses, wa, wb = mixer
        d4, d16 = DILATIONS[1:]
        mix = _mix_ab
        mix_args = [hn, om, outs[0], lses[0], outs[1], lses[1], outs[2], lses[2], wa, wb]
        mix_specs = [row(M_WIDTH)] * 4 + [res(d4)] * 2 + [res(d16)] * 2 + [full(wa), full(wb)]
        scratch = [pltpu.VMEM((B_WIDTH // LANES, tm, LANES), F32)] * 4
    if mix is not None:
        n_mix_in = len(mix_args)
        in_specs += mix_specs
        args += mix_args
    if final_g is not None:
        final_g = final_g.reshape(1, D)
        in_specs.append(full(final_g))
        args.append(final_g)
    out_specs, out_shape, proj_segs = [row(D)], [jax.ShapeDtypeStruct((T, D), F32)], ()
    if proj is not None:
        pg, pw, segs = proj
        pg = pg.reshape(1, D)
        in_specs += [full(pg), full(pw)]
        args += [pg, pw]
        proj_segs = tuple((wd, sc) for wd, _, sc in segs)
        out_specs += [row(wd) for wd, _, _ in segs]
        out_shape += [jax.ShapeDtypeStruct((T, wd), dt) for wd, dt, _ in segs]
    n_cast = 0
    if cast_next is not None:
        *stacks, (layer, half) = cast_next
        n_cast = len(stacks)
        for w in stacks:
            n_rows, n_cols = w.shape[2:]
            n_blk, blk_rows = _cast_blocks(n_rows, n_steps)
            in_specs.append(pl.BlockSpec((None, None, blk_rows, n_cols),
                                         lambda i, n_blk=n_blk: (layer, half, jnp.minimum(i, n_blk - 1), 0)))
            args.append(w)
            out_specs.append(pl.BlockSpec((blk_rows, n_cols), lambda i, n_blk=n_blk: (jnp.minimum(i, n_blk - 1), 0)))
            out_shape.append(jax.ShapeDtypeStruct((n_rows, n_cols), BF16))
    res_out = pl.pallas_call(
        functools.partial(_ffn_kernel, final_norm=final_g is not None, mix=mix, n_mix_in=n_mix_in,
                          n_scratch=len(scratch), proj_segs=proj_segs, n_cast=n_cast),
        grid=(n_steps,), in_specs=in_specs, out_specs=out_specs, out_shape=out_shape, scratch_shapes=scratch,
        compiler_params=_cparams("arbitrary"), name="ffn")(*args)
    n_proj = len(proj_segs)
    return res_out[0], tuple(res_out[1:1 + n_proj]), tuple(res_out[1 + n_proj:])


def _proj_ab_kernel(x_ref, xp_ref, xn_ref, g_ref, wqk_ref, wr_ref, wg_ref, cw_ref, cb_ref,
                    qa_ref, ka_ref, vm_ref, om_ref, gr_ref, *rest, tiles_per_seq):
    tok_refs = rest[0:3]
    res_refs = {d: rest[3 * (n + 1):3 * (n + 2)] for n, d in enumerate(DILATIONS[1:])}
    stage_ref = rest[-1]
    tm = x_ref.shape[0]
    halo = SUBLANES
    g = g_ref[...]
    i = pl.program_id(0)
    first = i % tiles_per_seq == 0
    last = i % tiles_per_seq == tiles_per_seq - 1

    x = x_ref[...]
    h = _rms(x, g).astype(BF16)
    x_ext = jnp.concatenate([xp_ref[...], x, xn_ref[...]], axis=0)
    h_ext = _rms(x_ext, g).astype(BF16)
    n_ext = tm + 2 * halo
    row = lax.broadcasted_iota(jnp.int32, (n_ext, 1), 0)
    outside = ((row < halo) & first) | ((row >= halo + tm) & last)
    n_conv = 2 * M_WIDTH // CONV_COLS
    qk_blocks = [jnp.dot(h_ext, wqk_ref[:, blk * CONV_COLS:(blk + 1) * CONV_COLS], preferred_element_type=F32)
                 for blk in range(n_conv)]
    y_vm = jnp.dot(h, wr_ref[:, 0:M_WIDTH], preferred_element_type=F32)
    y_om = jnp.dot(h, wr_ref[:, M_WIDTH:2 * M_WIDTH], preferred_element_type=F32)
    y_g = lax.dot_general(wg_ref[...], h, (((1,), (1,)), ((), ())), preferred_element_type=F32)
    ys = [jnp.dot(h, wr_ref[:, 2 * M_WIDTH + s * B_WIDTH:2 * M_WIDTH + (s + 1) * B_WIDTH],
                  preferred_element_type=F32) for s in range(len(tok_refs))]

    for blk in range(n_conv):
        cols = slice(blk * CONV_COLS, (blk + 1) * CONV_COLS)
        qk = jnp.where(outside, 0.0, qk_blocks[blk])
        acc = cb_ref[:, cols] + qk[halo:halo + tm] * cw_ref[M_CONV // 2:M_CONV // 2 + 1, cols]
        for j in range(M_CONV):
            off = j - M_CONV // 2
            if off != 0:
                shifted = pltpu.roll(qk, (-off) % n_ext, axis=0)
                acc = acc + shifted[halo:halo + tm] * cw_ref[j:j + 1, cols]
        act = acc * jax.nn.sigmoid(acc)
        if blk * CONV_COLS < M_WIDTH:
            qa_ref[:, cols] = act.astype(BF16)
        else:
            kcols = slice(blk * CONV_COLS - M_WIDTH, (blk + 1) * CONV_COLS - M_WIDTH)
            ka_ref[:, kcols] = (act * (M_HEAD_DIM ** -0.5)).astype(BF16)

    vm_ref[...] = y_vm.astype(BF16)
    om_ref[...] = y_om.astype(BF16)
    gr_ref[...] = y_g

    n_col = B_WIDTH // LANES
    for s, tok_ref in enumerate(tok_refs):
        y = ys[s] * ATTN_Q_SCALE if s == 0 else ys[s]
        tok_ref[...] = y.astype(BF16)
        for c in range(n_col):
            stage_ref[s, c] = y[:, c * LANES:(c + 1) * LANES]
        for d, refs in res_refs.items():
            for r in range(d):
                for c in range(n_col):
                    lo = (c * d + r) * LANES
                    refs[s][:, lo:lo + LANES] = stage_ref[s, c, pl.ds(r, tm // d, stride=d), :].astype(BF16)


def _proj_ab(x, g, w_qk, w_rest, wg_t, conv_w, conv_b, seq_len):
    T, D = x.shape
    tm = PROJ_AB_ROW_TILE
    halo = SUBLANES
    hb = tm // halo
    row = lambda w: pl.BlockSpec((tm, w), lambda i: (i, 0))
    res = lambda d: pl.BlockSpec((tm // d, d * B_WIDTH), lambda i: (i, 0))
    full = lambda a: _resident(a.shape, lambda i: (0, 0))
    conv_b = conv_b.reshape(1, -1)
    g = g.reshape(1, D)
    out_specs = [row(M_WIDTH)] * 4 + [pl.BlockSpec((wg_t.shape[0], tm), lambda i: (0, i))] + [row(B_WIDTH)] * 3
    out_shape = ([jax.ShapeDtypeStruct((T, M_WIDTH), BF16)] * 4
                 + [jax.ShapeDtypeStruct((wg_t.shape[0], T), F32)]
                 + [jax.ShapeDtypeStruct((T, B_WIDTH), BF16)] * 3)
    for d in DILATIONS[1:]:
        out_specs += [res(d)] * 3
        out_shape += [jax.ShapeDtypeStruct((T // d, d * B_WIDTH), BF16)] * 3
    return pl.pallas_call(
        functools.partial(_proj_ab_kernel, tiles_per_seq=seq_len // tm),
        grid=(T // tm,),
        in_specs=[row(D),
                  pl.BlockSpec((halo, D), lambda i: (jnp.maximum(i * hb - 1, 0), 0)),
                  pl.BlockSpec((halo, D), lambda i: (jnp.minimum((i + 1) * hb, T // halo - 1), 0)),
                  full(g), full(w_qk), full(w_rest), full(wg_t), full(conv_w), full(conv_b)],
        out_specs=out_specs, out_shape=out_shape,
        scratch_shapes=[pltpu.VMEM((3, B_WIDTH // LANES, tm, LANES), F32)],
        compiler_params=_cparams("parallel"), name="proj_ab")(x, x, x, g, w_qk, w_rest, wg_t, conv_w, conv_b)


def _attn_kernel(*refs, groups, kv_tiles, q_tile, q_block, k_window, radius, dist_unit, seq_len, has_sink,
                 with_lse):
    slope_ref = refs[0]
    pos = 1
    sink_ref = None
    if has_sink:
        sink_ref = refs[pos]
        pos += 1
    q_ref, k_ref, v_ref, o_ref = refs[pos:pos + 4]
    pos += 4
    lse_ref = None
    if with_lse:
        lse_ref = refs[pos]
        pos += 1
    bias_ref = refs[pos]
    pair = pl.program_id(0)
    tile = pl.program_id(3)

    shift_step = min(q_block, radius)

    @pl.when((pl.program_id(1) == 0) & (pl.program_id(2) == 0) & (tile == 0))
    def _():
        sub = lax.broadcasted_iota(jnp.int32, (q_block, k_window), 0)
        ln = lax.broadcasted_iota(jnp.int32, (q_block, k_window), 1)
        for variant in range(2 * radius // shift_step + 1):
            adist = jnp.abs(ln - sub - variant * shift_step)
            negd = jnp.where(adist <= radius, -(dist_unit * adist).astype(F32), -jnp.inf)
            for g in range(groups):
                for j in range(2):
                    bias_ref[variant, 2 * g + j] = (slope_ref[(2 * pair + j) * groups + g] * LOG2E) * negd

    lane = lax.broadcasted_iota(jnp.int32, (q_block, LANES), 1)
    low = lane < HEAD_DIM
    ones = jnp.ones((k_window, LANES), BF16)
    chains = []
    for blk in range(q_tile // q_block):
        q0 = tile * q_tile + blk * q_block
        ks = pl.multiple_of(jnp.clip(q0 - radius, 0, seq_len - k_window), HEAD_DIM)
        variant = (q0 - ks) // shift_step
        rows = slice(blk * q_block, (blk + 1) * q_block)
        for t in range(kv_tiles):
            kt = k_ref[pl.ds(ks, k_window), t * LANES:(t + 1) * LANES]
            vt = jnp.concatenate([v_ref[pl.ds(ks, k_window), t * LANES:(t + 1) * LANES], ones], axis=1)
            for g in range(groups):
                qt = q_ref[rows, (t * groups + g) * LANES:(t * groups + g + 1) * LANES]
                for j in range(2):
                    qm = jnp.where(low if j == 0 else jnp.logical_not(low), qt, jnp.zeros_like(qt))
                    chains.append(dict(qm=qm, kt=kt, vt=vt, variant=variant, slot=2 * g + j,
                                       head=(2 * pair + j) * groups + g))
    s = [lax.dot_general(c["qm"], c["kt"], (((1,), (1,)), ((), ())), preferred_element_type=F32)
         + bias_ref[c["variant"], c["slot"]] for c in chains]
    m = [jnp.max(x, axis=-1, keepdims=True) for x in s]
    if has_sink:
        m = [jnp.maximum(x, sink_ref[c["head"]] * LOG2E) for c, x in zip(chains, m)]
    p = [jnp.exp2(x - y).astype(BF16) for x, y in zip(s, m)]
    od = [jnp.dot(x, c["vt"], preferred_element_type=F32) for c, x in zip(chains, p)]
    den = [x[:, LANES:] for x in od]
    if has_sink:
        den = [x + jnp.exp2(sink_ref[c["head"]] * LOG2E - y) for c, x, y in zip(chains, den, m)]
    o = [x[:, :LANES] / y for x, y in zip(od, den)]
    lse = [x + jnp.log2(y) for x, y in zip(m, den)] if with_lse else None
    for n in range(0, len(chains), 2):
        blk, tile_col = divmod(n // 2, kv_tiles * groups)
        rows = slice(blk * q_block, (blk + 1) * q_block)
        cols = slice(tile_col * LANES, (tile_col + 1) * LANES)
        o_ref[rows, cols] = jnp.where(low, o[n], o[n + 1]).astype(o_ref.dtype)
        if with_lse:
            lse_ref[rows, cols] = jnp.where(low, lse[n], lse[n + 1])


def _banded_attention(qv, kv, vv, slopes, sink, *, dilation, groups, radius, with_lse, out_dtype, q_block, chains):
    Bn, Lv, _ = qv.shape
    d = dilation
    pairs = kv.shape[2] // (d * LANES)
    q_tile = min(chains // (2 * groups) * q_block, Lv)
    kv_tiles = min(d, max(1, chains // (2 * groups * (q_tile // q_block))))
    k_window = q_block + 2 * radius
    shift_step = min(q_block, radius)
    assert max(q_block, radius) % shift_step == 0 and Lv % q_tile == 0 and k_window <= Lv
    assert d % kv_tiles == 0
    rsteps = d // kv_tiles
    smem = pl.BlockSpec(memory_space=pltpu.SMEM)
    qspec = pl.BlockSpec((None, q_tile, kv_tiles * groups * LANES), lambda p, b, r, i: (b, i, p * rsteps + r))
    kspec = pl.BlockSpec((None, Lv, kv_tiles * LANES), lambda p, b, r, i: (b, 0, p * rsteps + r))
    in_specs = [smem] + ([smem] if sink is not None else []) + [qspec, kspec, kspec]
    args = [slopes] + ([sink] if sink is not None else []) + [qv, kv, vv]
    out_specs = [qspec]
    out_shape = [jax.ShapeDtypeStruct(qv.shape, out_dtype)]
    if with_lse:
        out_specs.append(qspec)
        out_shape.append(jax.ShapeDtypeStruct(qv.shape, F32))
    return pl.pallas_call(
        functools.partial(_attn_kernel, groups=groups, kv_tiles=kv_tiles, q_tile=q_tile, q_block=q_block,
                          k_window=k_window,
                          radius=radius, dist_unit=d, seq_len=Lv, has_sink=sink is not None, with_lse=with_lse),
        grid=(pairs, Bn, rsteps, Lv // q_tile), in_specs=in_specs, out_specs=out_specs, out_shape=out_shape,
        scratch_shapes=[pltpu.VMEM((2 * radius // shift_step + 1, 2 * groups, q_block, k_window), F32)],
        compiler_params=_cparams("arbitrary", "arbitrary", "arbitrary", "arbitrary"),
        name=f"banded_attention_d{d}")(*args)


def _alibi_slopes(n):
    return jnp.exp2(-8.0 * jnp.arange(1, n + 1, dtype=F32) / n)


def _log_sigmoid(x):
    return jnp.minimum(x, 0.0) - jnp.log1p(jnp.exp(-jnp.abs(x)))


def _mlstm_chunks(chains, seen, diag):
    nt = (((1,), (1,)), ((), ()))
    E = chains[0]["q"].shape[1]
    mask = [seen[c["rev"]] for c in chains]
    b_c = [jnp.sum(jnp.where(mk, c["lf_r"], 0.0), axis=1, keepdims=True) for c, mk in zip(chains, mask)]
    qk = [lax.dot_general(c["q"], c["k"], nt, preferred_element_type=F32) for c in chains]
    inter = [jnp.dot(c["q"], c["ct"].astype(BF16), preferred_element_type=F32) for c in chains]
    k_t = [c["k"].astype(F32).T for c in chains]
    b_last = [jnp.sum(c["lf_r"], axis=1, keepdims=True) for c in chains]
    b_r = [jnp.sum(jnp.where(diag, x, 0.0), axis=0, keepdims=True) for x in b_c]
    u_r = [c["i_r"] - x for c, x in zip(chains, b_r)]
    um = [jnp.where(mk, x, -jnp.inf) for mk, x in zip(mask, u_r)]
    g_t = [jnp.maximum(c["m"], jnp.max(x, axis=1, keepdims=True)) for c, x in zip(chains, um)]
    a_r = [x + y for x, y in zip(b_last, u_r)]
    m_new = [jnp.maximum(x + c["m"], jnp.max(y, axis=1, keepdims=True)) for c, x, y in zip(chains, b_last, a_r)]
    kw_t = [(x * jnp.exp2(y - z)).astype(BF16) for x, y, z in zip(k_t, a_r, m_new)]
    ct = [jnp.exp2(x + c["m"] - z) * c["ct"] + jnp.dot(y, c["v1"], preferred_element_type=F32)
          for c, x, y, z in zip(chains, b_last, kw_t, m_new)]
    w = [(jnp.exp2(x - y) * z).astype(BF16) for x, y, z in zip(um, g_t, qk)]
    tot = [jnp.dot(x, c["v1"], preferred_element_type=F32) + jnp.exp2(c["m"] - y) * z
           for c, x, y, z in zip(chains, w, g_t, inter)]
    h = [x[:, :E] / jnp.maximum(jnp.abs(x[:, E:]), jnp.exp2(-(y + z))) for x, y, z in zip(tot, b_c, g_t)]
    return list(zip(h, ct, m_new))


def _mlstm_kernel(q_ref, k_ref, v_ref, gr_ref, gb_ref, hg_ref, o_ref, acc_ref, *, seq_len, chunk, heads):
    S, L, E = seq_len, chunk, M_HEAD_DIM
    n_chunks = S // L
    row = lax.broadcasted_iota(jnp.int32, (L, L), 0)
    col = lax.broadcasted_iota(jnp.int32, (L, L), 1)
    seen = (col <= row, col >= row)
    diag = col == row
    ones = jnp.ones((L, E), BF16)

    def finish(tot, lanes):
        return tot * lax.rsqrt(jnp.mean(tot * tot, axis=-1, keepdims=True) + EPS) * hg_ref[:, lanes]

    def sweep(second):
        def body(it, states):
            chains, where = [], []
            for n, (ct, m) in enumerate(states):
                hh, rev = divmod(n, 2)
                c = it if rev == 0 else n_chunks - 1 - it
                rows = pl.ds(pl.multiple_of(c * L, L), L)
                lanes = slice(hh * E, (hh + 1) * E)
                g = gr_ref[4 * hh:4 * hh + 4, rows] + gb_ref[4 * hh:4 * hh + 4, :]
                where.append((rows, lanes, acc_ref[rows, lanes] if second else None))
                chains.append(dict(q=q_ref[rows, lanes], k=k_ref[rows, lanes],
                                   v1=jnp.concatenate([v_ref[rows, lanes], ones], axis=1),
                                   i_r=g[rev:rev + 1] * LOG2E, lf_r=_log_sigmoid(g[2 + rev:3 + rev]) * LOG2E,
                                   ct=ct, m=m, rev=rev))
            out = _mlstm_chunks(chains, seen, diag)
            for (rows, lanes, prev), (h, _, _) in zip(where, out):
                if second:
                    o_ref[rows, lanes] = finish(prev + h, lanes).astype(o_ref.dtype)
                else:
                    acc_ref[rows, lanes] = h
            return tuple((ct, m) for _, ct, m in out)
        return body

    zero = (jnp.zeros((E, 2 * E), F32), jnp.zeros((1, 1), F32))
    states = lax.fori_loop(0, n_chunks // 2, sweep(False), (zero,) * (2 * heads))
    lax.fori_loop(n_chunks // 2, n_chunks, sweep(True), states)


def _mlstm(q, k, v, g_rows, gate_b, hnorm_g, Bn, S):
    E = M_HEAD_DIM
    T = Bn * S
    hs = MLSTM_HEADS_PER_STEP
    groups = M_HEADS // hs
    seq = pl.BlockSpec((None, S, hs * E), lambda b, h: (b, 0, h))
    q3, k3, v3 = (t.reshape(Bn, S, M_WIDTH) for t in (q, k, v))
    out = pl.pallas_call(
        functools.partial(_mlstm_kernel, seq_len=S, chunk=MLSTM_CHUNK, heads=hs),
        grid=(Bn, groups),
        in_specs=[seq, seq, seq,
                  pl.BlockSpec((None, 4 * hs, S), lambda b, h: (h, 0, b)),
                  pl.BlockSpec((None, 4 * hs, 1), lambda b, h: (h, 0, 0)),
                  pl.BlockSpec((1, hs * E), lambda b, h: (0, h))],
        out_specs=seq,
        out_shape=jax.ShapeDtypeStruct((Bn, S, M_WIDTH), BF16),
        scratch_shapes=[pltpu.VMEM((S, hs * E), F32)],
        compiler_params=_cparams("parallel", "parallel"), name="mlstm")(
            q3, k3, v3, g_rows.reshape(groups, 4 * hs, T), gate_b.reshape(groups, 4 * hs, 1),
            hnorm_g.reshape(1, M_WIDTH))
    return out.reshape(T, M_WIDTH)


def _mixer_ab(x, g, w_in, conv_w, conv_b, gate_b, hnorm_g, w_out, Bn, S):
    T = Bn * S
    MW, BW = M_WIDTH, B_WIDTH
    g0 = 4 * MW
    n_gate = 4 * M_HEADS
    w_qk = w_in[:, :2 * MW].astype(BF16)
    w_rest = jnp.concatenate([w_in[:, 2 * MW:g0], w_in[:, g0 + n_gate:]], axis=1).astype(BF16)
    wg_t = w_in[:, g0:g0 + n_gate].reshape(-1, 4, M_HEADS).transpose(2, 1, 0).reshape(n_gate, -1).astype(BF16)
    gate_b = gate_b.reshape(4, M_HEADS).T.astype(F32)
    (q_m, k_m, v_m, o_m, g_rows, q1, k1, v1, q4, k4, v4, q16, k16, v16) = _proj_ab(
        x, g, w_qk, w_rest, wg_t, conv_w, conv_b, S)

    hn = _mlstm(q_m, k_m, v_m, g_rows, gate_b, hnorm_g, Bn, S)

    slopes = _alibi_slopes(B_HEADS)
    outs, lses = [], []
    for d, qkv in zip(DILATIONS, ((q1, k1, v1), (q4, k4, v4), (q16, k16, v16))):
        qv, kv, vv = (t.reshape(Bn, S // d, d * BW) for t in qkv)
        o, lse = _banded_attention(qv, kv, vv, slopes, None, dilation=d, groups=1, radius=B_RADIUS,
                                   with_lse=True, out_dtype=BF16, q_block=B_Q_BLOCK, chains=B_CHAINS)
        outs.append(o.reshape(T // d, d * BW))
        lses.append(lse.reshape(T // d, d * BW))
    w_out = w_out.astype(BF16)
    return ("ab", hn, o_m, outs, lses, w_out[:MW], w_out[MW:])


def _c_head_order():
    order = []
    for pair in range(C_KV_HEADS // 2):
        for slot in range(C_GROUP):
            for j in range(2):
                order.append((2 * pair + j) * C_GROUP + slot)
    return jnp.array(order, dtype=jnp.int32)


def _c_in_proj(g, w_in):
    D = w_in.shape[0]
    QW = C_HEADS * HEAD_DIM
    KW = C_KV_HEADS * HEAD_DIM
    wq = w_in[:, :QW].reshape(D, C_HEADS, HEAD_DIM)[:, _c_head_order()].reshape(D, QW)
    w_main = jnp.concatenate([wq, w_in[:, QW:]], axis=1).astype(BF16)
    return g, w_main, [(QW, BF16, ATTN_Q_SCALE), (KW, BF16, 1.0), (KW, BF16, 1.0)]


def _mixer_c(q, k, v, sink, w_out, Bn, S):
    T, QW = q.shape
    KW = k.shape[1]
    D = w_out.shape[1]
    w_out = w_out.reshape(C_HEADS, HEAD_DIM, D)[_c_head_order()].reshape(QW, D).astype(BF16)
    (o,) = _banded_attention(q.reshape(Bn, S, QW), k.reshape(Bn, S, KW), v.reshape(Bn, S, KW),
                             _alibi_slopes(C_HEADS), sink.astype(F32), dilation=1, groups=C_GROUP,
                             radius=C_RADIUS, with_lse=False, out_dtype=BF16, q_block=C_Q_BLOCK, chains=C_CHAINS)
    return ("c", o.reshape(T, QW), w_out)


def kernel(x, norm_g, ffn_w1, ffn_w3, ffn_w2, ab_w_in, ab_conv_w, ab_conv_b, ab_gate_b, ab_hnorm_g, ab_w_out,
           c_w_in, c_sink, c_w_out, final_g):
    Bn, S, D = x.shape
    depth = norm_g.shape[0]
    stacks = (ffn_w1, ffn_w3, ffn_w2)
    weights = tuple(w[0, 0].astype(BF16) for w in stacks)
    x = x.reshape(Bn * S, D)
    for l in range(depth):
        j = l // 2
        proj = _c_in_proj(norm_g[l, 1], c_w_in[j]) if l % 2 else None
        x, qkv, weights = _ffn(x, norm_g[l, 0], weights, proj=proj, cast_next=stacks + ((l, 1),))
        if l % 2 == 0:
            mixed = _mixer_ab(x, norm_g[l, 1], ab_w_in[j], ab_conv_w[j], ab_conv_b[j], ab_gate_b[j], ab_hnorm_g[j],
                              ab_w_out[j], Bn, S)
        else:
            mixed = _mixer_c(*qkv, c_sink[j], c_w_out[j], Bn, S)
        last = l == depth - 1
        x, _, weights = _ffn(x, norm_g[l, 2], weights, final_g if last else None, mixed,
                             cast_next=None if last else stacks + ((l + 1, 0),))
    return x.reshape(Bn, S, D)
```

```python
import functools

import jax
import jax.numpy as jnp
from jax import lax
from jax.experimental import pallas as pl
from jax.experimental.pallas import tpu as pltpu

F32 = jnp.float32
BF16 = jnp.bfloat16

EPS = 1e-6
M_HEADS = 4
M_HEAD_DIM = 128
M_WIDTH = M_HEADS * M_HEAD_DIM
M_CONV = 5
B_HEADS = 8
B_WIDTH = 512
DILATIONS = (1, 4, 16)
B_RADIUS = 64
C_HEADS = 16
C_KV_HEADS = 4
C_GROUP = C_HEADS // C_KV_HEADS
C_RADIUS = 128
HEAD_DIM = 64
LANES = 128
SUBLANES = 8
LOG2E = 1.4426950408889634
ATTN_Q_SCALE = HEAD_DIM ** -0.5 * LOG2E

VMEM_LIMIT_BYTES = 56 * 1024 * 1024
ROW_TILE = 512
PROJ_AB_ROW_TILE = 1024
B_Q_BLOCK, B_CHAINS = 128, 128
C_Q_BLOCK, C_CHAINS = 128, 128
MLSTM_CHUNK = 256
CONV_COLS = 256
MLSTM_HEADS_PER_STEP = 2


def _cparams(*sem):
    return pltpu.CompilerParams(dimension_semantics=sem, vmem_limit_bytes=VMEM_LIMIT_BYTES)


def _resident(shape, index_map):
    return pl.BlockSpec(shape, index_map, pipeline_mode=pl.Buffered(1))


def _rms(x, g):
    return x * lax.rsqrt(jnp.mean(x * x, axis=-1, keepdims=True) + EPS) * g


def _mix_c(a_ref, w_ref):
    return jnp.dot(a_ref[...], w_ref[...], preferred_element_type=F32)


def _mix_ab(hn_ref, om_ref, o1_ref, l1_ref, o4_ref, l4_ref, o16_ref, l16_ref, wa_ref, wb_ref,
            so4_ref, sl4_ref, so16_ref, sl16_ref):
    tm = hn_ref.shape[0]
    n_col = B_WIDTH // LANES
    for d, pairs in ((DILATIONS[1], ((o4_ref, so4_ref), (l4_ref, sl4_ref))),
                     (DILATIONS[2], ((o16_ref, so16_ref), (l16_ref, sl16_ref)))):
        for src_ref, dst_ref in pairs:
            for r in range(d):
                for c in range(n_col):
                    lo = (c * d + r) * LANES
                    dst_ref[c, pl.ds(r, tm // d, stride=d), :] = src_ref[:, lo:lo + LANES].astype(F32)
    cols = []
    for c in range(n_col):
        lanes = slice(c * LANES, (c + 1) * LANES)
        l1, l2, l3 = l1_ref[:, lanes], sl4_ref[c], sl16_ref[c]
        lm = jnp.maximum(jnp.maximum(l1, l2), l3)
        e1, e2, e3 = jnp.exp2(l1 - lm), jnp.exp2(l2 - lm), jnp.exp2(l3 - lm)
        ob = (e1 * o1_ref[:, lanes].astype(F32) + e2 * so4_ref[c] + e3 * so16_ref[c]) / (e1 + e2 + e3)
        cols.append(ob.astype(BF16))
    ob = jnp.concatenate(cols, axis=1)
    ma = (hn_ref[...].astype(F32) * jax.nn.sigmoid(om_ref[...].astype(F32))).astype(BF16)
    return (jnp.dot(ma, wa_ref[...], preferred_element_type=F32)
            + jnp.dot(ob, wb_ref[...], preferred_element_type=F32))


def _ffn_kernel(x_ref, g_ref, w1_ref, w3_ref, w2_ref, *rest, final_norm, mix, n_mix_in, n_scratch, proj_segs,
                n_cast):
    mix_in = rest[:n_mix_in]
    rest = rest[n_mix_in:]
    n_in = final_norm + (2 if proj_segs else 0)
    cast_in = rest[n_in:n_in + n_cast]
    scratch = rest[len(rest) - n_scratch:]
    cast_out = rest[len(rest) - n_scratch - n_cast:len(rest) - n_scratch]
    proj_outs = rest[len(rest) - n_scratch - n_cast - len(proj_segs):len(rest) - n_scratch - n_cast]
    o_ref = rest[len(rest) - n_scratch - n_cast - len(proj_segs) - 1]
    for src_ref, dst_ref in zip(cast_in, cast_out):
        dst_ref[...] = src_ref[...].astype(BF16)
    x = x_ref[...]
    if mix is not None:
        x = x + mix(*mix_in, *scratch)
    h = _rms(x, g_ref[...]).astype(BF16)
    a = jnp.dot(h, w1_ref[...], preferred_element_type=F32)
    b = jnp.dot(h, w3_ref[...], preferred_element_type=F32)
    act = (a * jax.nn.sigmoid(a) * b).astype(BF16)
    y = x + 0.5 * jnp.dot(act, w2_ref[...], preferred_element_type=F32)
    if final_norm:
        y = _rms(y, rest[0][...])
    o_ref[...] = y
    if proj_segs:
        pg_ref, pw_ref = rest[final_norm:final_norm + 2]
        h2 = _rms(y, pg_ref[...]).astype(BF16)
        off = 0
        for (width, scale), p_ref in zip(proj_segs, proj_outs):
            z = jnp.dot(h2, pw_ref[:, off:off + width], preferred_element_type=F32)
            p_ref[...] = (z * scale if scale != 1.0 else z).astype(p_ref.dtype)
            off += width


def _cast_blocks(n_rows, n_steps):
    n_blk = max(n for n in range(1, n_steps + 1) if n_steps % n == 0 and n_rows % (16 * n) == 0)
    return n_blk, n_rows // n_blk


def _ffn(x, g, weights, final_g=None, mixer=None, proj=None, cast_next=None):
    T, D = x.shape
    tm = ROW_TILE
    n_steps = T // tm
    w1, w3, w2 = weights
    row = lambda w: pl.BlockSpec((tm, w), lambda i: (i, 0))
    res = lambda d: pl.BlockSpec((tm // d, d * B_WIDTH), lambda i: (i, 0))
    full = lambda a: _resident(a.shape, lambda i: (0, 0))
    g = g.reshape(1, D)
    in_specs = [row(D), full(g), full(w1), full(w3), full(w2)]
    args = [x, g, w1, w3, w2]
    mix, n_mix_in, scratch = None, 0, []
    if mixer is not None and mixer[0] == "c":
        _, att, w_out = mixer
        mix, mix_args, mix_specs = _mix_c, [att, w_out], [row(att.shape[1]), full(w_out)]
    elif mixer is not None:
        _, hn, om, outs, lses, wa, wb = mixer
        d4, d16 = DILATIONS[1:]
        mix = _mix_ab
        mix_args = [hn, om, outs[0], lses[0], outs[1], lses[1], outs[2], lses[2], wa, wb]
        mix_specs = [row(M_WIDTH)] * 4 + [res(d4)] * 2 + [res(d16)] * 2 + [full(wa), full(wb)]
        scratch = [pltpu.VMEM((B_WIDTH // LANES, tm, LANES), F32)] * 4
    if mix is not None:
        n_mix_in = len(mix_args)
        in_specs += mix_specs
        args += mix_args
    if final_g is not None:
        final_g = final_g.reshape(1, D)
        in_specs.append(full(final_g))
        args.append(final_g)
    out_specs, out_shape, proj_segs = [row(D)], [jax.ShapeDtypeStruct((T, D), F32)], ()
    if proj is not None:
        pg, pw, segs = proj
        pg = pg.reshape(1, D)
        in_specs += [full(pg), full(pw)]
        args += [pg, pw]
        proj_segs = tuple((wd, sc) for wd, _, sc in segs)
        out_specs += [row(wd) for wd, _, _ in segs]
        out_shape += [jax.ShapeDtypeStruct((T, wd), dt) for wd, dt, _ in segs]
    n_cast = 0
    if cast_next is not None:
        *stacks, (layer, half) = cast_next
        n_cast = len(stacks)
        for w in stacks:
            n_rows, n_cols = w.shape[2:]
            n_blk, blk_rows = _cast_blocks(n_rows, n_steps)
            in_specs.append(pl.BlockSpec((None, None, blk_rows, n_cols),
                                         lambda i, n_blk=n_blk: (layer, half, jnp.minimum(i, n_blk - 1), 0)))
            args.append(w)
            out_specs.append(pl.BlockSpec((blk_rows, n_cols), lambda i, n_blk=n_blk: (jnp.minimum(i, n_blk - 1), 0)))
            out_shape.append(jax.ShapeDtypeStruct((n_rows, n_cols), BF16))
    res_out = pl.pallas_call(
        functools.partial(_ffn_kernel, final_norm=final_g is not None, mix=mix, n_mix_in=n_mix_in,
                          n_scratch=len(scratch), proj_segs=proj_segs, n_cast=n_cast),
        grid=(n_steps,), in_specs=in_specs, out_specs=out_specs, out_shape=out_shape, scratch_shapes=scratch,
        compiler_params=_cparams("arbitrary"), name="ffn")(*args)
    n_proj = len(proj_segs)
    return res_out[0], tuple(res_out[1:1 + n_proj]), tuple(res_out[1 + n_proj:])


def _proj_ab_kernel(x_ref, xp_ref, xn_ref, g_ref, wqk_ref, wr_ref, wg_ref, cw_ref, cb_ref,
                    qa_ref, ka_ref, vm_ref, om_ref, gr_ref, *rest, tiles_per_seq):
    tok_refs = rest[0:3]
    res_refs = {d: rest[3 * (n + 1):3 * (n + 2)] for n, d in enumerate(DILATIONS[1:])}
    stage_ref = rest[-1]
    tm = x_ref.shape[0]
    halo = SUBLANES
    g = g_ref[...]
    i = pl.program_id(0)
    first = i % tiles_per_seq == 0
    last = i % tiles_per_seq == tiles_per_seq - 1

    x = x_ref[...]
    h = _rms(x, g).astype(BF16)
    x_ext = jnp.concatenate([xp_ref[...], x, xn_ref[...]], axis=0)
    h_ext = _rms(x_ext, g).astype(BF16)
    n_ext = tm + 2 * halo
    row = lax.broadcasted_iota(jnp.int32, (n_ext, 1), 0)
    outside = ((row < halo) & first) | ((row >= halo + tm) & last)
    n_conv = 2 * M_WIDTH // CONV_COLS
    qk_blocks = [jnp.dot(h_ext, wqk_ref[:, blk * CONV_COLS:(blk + 1) * CONV_COLS], preferred_element_type=F32)
                 for blk in range(n_conv)]
    y_vm = jnp.dot(h, wr_ref[:, 0:M_WIDTH], preferred_element_type=F32)
    y_om = jnp.dot(h, wr_ref[:, M_WIDTH:2 * M_WIDTH], preferred_element_type=F32)
    y_g = lax.dot_general(wg_ref[...], h, (((1,), (1,)), ((), ())), preferred_element_type=F32)
    ys = [jnp.dot(h, wr_ref[:, 2 * M_WIDTH + s * B_WIDTH:2 * M_WIDTH + (s + 1) * B_WIDTH],
                  preferred_element_type=F32) for s in range(len(tok_refs))]

    for blk in range(n_conv):
        cols = slice(blk * CONV_COLS, (blk + 1) * CONV_COLS)
        qk = jnp.where(outside, 0.0, qk_blocks[blk])
        acc = cb_ref[:, cols] + qk[halo:halo + tm] * cw_ref[M_CONV // 2:M_CONV // 2 + 1, cols]
        for j in range(M_CONV):
            off = j - M_CONV // 2
            if off != 0:
                shifted = pltpu.roll(qk, (-off) % n_ext, axis=0)
                acc = acc + shifted[halo:halo + tm] * cw_ref[j:j + 1, cols]
        act = acc * jax.nn.sigmoid(acc)
        if blk * CONV_COLS < M_WIDTH:
            qa_ref[:, cols] = act.astype(BF16)
        else:
            kcols = slice(blk * CONV_COLS - M_WIDTH, (blk + 1) * CONV_COLS - M_WIDTH)
            ka_ref[:, kcols] = (act * (M_HEAD_DIM ** -0.5)).astype(BF16)

    vm_ref[...] = y_vm.astype(BF16)
    om_ref[...] = y_om.astype(BF16)
    gr_ref[...] = y_g

    n_col = B_WIDTH // LANES
    for s, tok_ref in enumerate(tok_refs):
        y = ys[s] * ATTN_Q_SCALE if s == 0 else ys[s]
        tok_ref[...] = y.astype(BF16)
        for c in range(n_col):
            stage_ref[s, c] = y[:, c * LANES:(c + 1) * LANES]
        for d, refs in res_refs.items():
            for r in range(d):
                for c in range(n_col):
                    lo = (c * d + r) * LANES
                    refs[s][:, lo:lo + LANES] = stage_ref[s, c, pl.ds(r, tm // d, stride=d), :].astype(BF16)


def _proj_ab(x, g, w_qk, w_rest, wg_t, conv_w, conv_b, seq_len):
    T, D = x.shape
    tm = PROJ_AB_ROW_TILE
    halo = SUBLANES
    hb = tm // halo
    row = lambda w: pl.BlockSpec((tm, w), lambda i: (i, 0))
    res = lambda d: pl.BlockSpec((tm // d, d * B_WIDTH), lambda i: (i, 0))
    full = lambda a: _resident(a.shape, lambda i: (0, 0))
    conv_b = conv_b.reshape(1, -1)
    g = g.reshape(1, D)
    out_specs = [row(M_WIDTH)] * 4 + [pl.BlockSpec((wg_t.shape[0], tm), lambda i: (0, i))] + [row(B_WIDTH)] * 3
    out_shape = ([jax.ShapeDtypeStruct((T, M_WIDTH), BF16)] * 4
                 + [jax.ShapeDtypeStruct((wg_t.shape[0], T), F32)]
                 + [jax.ShapeDtypeStruct((T, B_WIDTH), BF16)] * 3)
    for d in DILATIONS[1:]:
        out_specs += [res(d)] * 3
        out_shape += [jax.ShapeDtypeStruct((T // d, d * B_WIDTH), BF16)] * 3
    return pl.pallas_call(
        functools.partial(_proj_ab_kernel, tiles_per_seq=seq_len // tm),
        grid=(T // tm,),
        in_specs=[row(D),
                  pl.BlockSpec((halo, D), lambda i: (jnp.maximum(i * hb - 1, 0), 0)),
                  pl.BlockSpec((halo, D), lambda i: (jnp.minimum((i + 1) * hb, T // halo - 1), 0)),
                  full(g), full(w_qk), full(w_rest), full(wg_t), full(conv_w), full(conv_b)],
        out_specs=out_specs, out_shape=out_shape,
        scratch_shapes=[pltpu.VMEM((3, B_WIDTH // LANES, tm, LANES), F32)],
        compiler_params=_cparams("parallel"), name="proj_ab")(x, x, x, g, w_qk, w_rest, wg_t, conv_w, conv_b)


def _attn_kernel(*refs, groups, kv_tiles, q_tile, q_block, k_window, radius, dist_unit, seq_len, has_sink,
                 with_lse):
    slope_ref = refs[0]
    pos = 1
    sink_ref = None
    if has_sink:
        sink_ref = refs[pos]
        pos += 1
    q_ref, k_ref, v_ref, o_ref = refs[pos:pos + 4]
    pos += 4
    lse_ref = None
    if with_lse:
        lse_ref = refs[pos]
        pos += 1
    bias_ref = refs[pos]
    pair = pl.program_id(0)
    tile = pl.program_id(3)

    shift_step = min(q_block, radius)

    @pl.when((pl.program_id(1) == 0) & (pl.program_id(2) == 0) & (tile == 0))
    def _():
        sub = lax.broadcasted_iota(jnp.int32, (q_block, k_window), 0)
        ln = lax.broadcasted_iota(jnp.int32, (q_block, k_window), 1)
        for variant in range(2 * radius // shift_step + 1):
            adist = jnp.abs(ln - sub - variant * shift_step)
            negd = jnp.where(adist <= radius, -(dist_unit * adist).astype(F32), -jnp.inf)
            for g in range(groups):
                for j in range(2):
                    bias_ref[variant, 2 * g + j] = (slope_ref[(2 * pair + j) * groups + g] * LOG2E) * negd

    lane = lax.broadcasted_iota(jnp.int32, (q_block, LANES), 1)
    low = lane < HEAD_DIM
    ones = jnp.ones((k_window, LANES), BF16)
    chains = []
    for blk in range(q_tile // q_block):
        q0 = tile * q_tile + blk * q_block
        ks = pl.multiple_of(jnp.clip(q0 - radius, 0, seq_len - k_window), HEAD_DIM)
        variant = (q0 - ks) // shift_step
        rows = slice(blk * q_block, (blk + 1) * q_block)
        for t in range(kv_tiles):
            kt = k_ref[pl.ds(ks, k_window), t * LANES:(t + 1) * LANES]
            vt = jnp.concatenate([v_ref[pl.ds(ks, k_window), t * LANES:(t + 1) * LANES], ones], axis=1)
            for g in range(groups):
                qt = q_ref[rows, (t * groups + g) * LANES:(t * groups + g + 1) * LANES]
                for j in range(2):
                    qm = jnp.where(low if j == 0 else jnp.logical_not(low), qt, jnp.zeros_like(qt))
                    chains.append(dict(qm=qm, kt=kt, vt=vt, variant=variant, slot=2 * g + j,
                                       head=(2 * pair + j) * groups + g))
    s = [lax.dot_general(c["qm"], c["kt"], (((1,), (1,)), ((), ())), preferred_element_type=F32)
         + bias_ref[c["variant"], c["slot"]] for c in chains]
    m = [jnp.max(x, axis=-1, keepdims=True) for x in s]
    if has_sink:
        m = [jnp.maximum(x, sink_ref[c["head"]] * LOG2E) for c, x in zip(chains, m)]
    p = [jnp.exp2(x - y).astype(BF16) for x, y in zip(s, m)]
    od = [jnp.dot(x, c["vt"], preferred_element_type=F32) for c, x in zip(chains, p)]
    den = [x[:, LANES:] for x in od]
    if has_sink:
        den = [x + jnp.exp2(sink_ref[c["head"]] * LOG2E - y) for c, x, y in zip(chains, den, m)]
    o = [x[:, :LANES] / y for x, y in zip(od, den)]
    lse = [x + jnp.log2(y) for x, y in zip(m, den)] if with_lse else None
    for n in range(0, len(chains), 2):
        blk, tile_col = divmod(n // 2, kv_tiles * groups)
        rows = slice(blk * q_block, (blk + 1) * q_block)
        cols = slice(tile_col * LANES, (tile_col + 1) * LANES)
        o_ref[rows, cols] = jnp.where(low, o[n], o[n + 1]).astype(o_ref.dtype)
        if with_lse:
            lse_ref[rows, cols] = jnp.where(low, lse[n], lse[n + 1])


def _banded_attention(qv, kv, vv, slopes, sink, *, dilation, groups, radius, with_lse, out_dtype, q_block, chains):
    Bn, Lv, _ = qv.shape
    d = dilation
    pairs = kv.shape[2] // (d * LANES)
    q_tile = min(chains // (2 * groups) * q_block, Lv)
    kv_tiles = min(d, max(1, chains // (2 * groups * (q_tile // q_block))))
    k_window = q_block + 2 * radius
    shift_step = min(q_block, radius)
    assert max(q_block, radius) % shift_step == 0 and Lv % q_tile == 0 and k_window <= Lv
    assert d % kv_tiles == 0
    rsteps = d // kv_tiles
    smem = pl.BlockSpec(memory_space=pltpu.SMEM)
    qspec = pl.BlockSpec((None, q_tile, kv_tiles * groups * LANES), lambda p, b, r, i: (b, i, p * rsteps + r))
    kspec = pl.BlockSpec((None, Lv, kv_tiles * LANES), lambda p, b, r, i: (b, 0, p * rsteps + r))
    in_specs = [smem] + ([smem] if sink is not None else []) + [qspec, kspec, kspec]
    args = [slopes] + ([sink] if sink is not None else []) + [qv, kv, vv]
    out_specs = [qspec]
    out_shape = [jax.ShapeDtypeStruct(qv.shape, out_dtype)]
    if with_lse:
        out_specs.append(qspec)
        out_shape.append(jax.ShapeDtypeStruct(qv.shape, F32))
    return pl.pallas_call(
        functools.partial(_attn_kernel, groups=groups, kv_tiles=kv_tiles, q_tile=q_tile, q_block=q_block,
                          k_window=k_window,
                          radius=radius, dist_unit=d, seq_len=Lv, has_sink=sink is not None, with_lse=with_lse),
        grid=(pairs, Bn, rsteps, Lv // q_tile), in_specs=in_specs, out_specs=out_specs, out_shape=out_shape,
        scratch_shapes=[pltpu.VMEM((2 * radius // shift_step + 1, 2 * groups, q_block, k_window), F32)],
        compiler_params=_cparams("arbitrary", "arbitrary", "arbitrary", "arbitrary"),
        name=f"banded_attention_d{d}")(*args)


def _alibi_slopes(n):
    return jnp.exp2(-8.0 * jnp.arange(1, n + 1, dtype=F32) / n)


def _log_sigmoid(x):
    return jnp.minimum(x, 0.0) - jnp.log1p(jnp.exp(-jnp.abs(x)))


def _mlstm_chunks(chains, seen, diag):
    nt = (((1,), (1,)), ((), ()))
    E = chains[0]["q"].shape[1]
    mask = [seen[c["rev"]] for c in chains]
    b_c = [jnp.sum(jnp.where(mk, c["lf_r"], 0.0), axis=1, keepdims=True) for c, mk in zip(chains, mask)]
    qk = [lax.dot_general(c["q"], c["k"], nt, preferred_element_type=F32) for c in chains]
    inter = [jnp.dot(c["q"], c["ct"].astype(BF16), preferred_element_type=F32) for c in chains]
    k_t = [c["k"].astype(F32).T for c in chains]
    b_last = [jnp.sum(c["lf_r"], axis=1, keepdims=True) for c in chains]
    b_r = [jnp.sum(jnp.where(diag, x, 0.0), axis=0, keepdims=True) for x in b_c]
    u_r = [c["i_r"] - x for c, x in zip(chains, b_r)]
    um = [jnp.where(mk, x, -jnp.inf) for mk, x in zip(mask, u_r)]
    g_t = [jnp.maximum(c["m"], jnp.max(x, axis=1, keepdims=True)) for c, x in zip(chains, um)]
    a_r = [x + y for x, y in zip(b_last, u_r)]
    m_new = [jnp.maximum(x + c["m"], jnp.max(y, axis=1, keepdims=True)) for c, x, y in zip(chains, b_last, a_r)]
    kw_t = [(x * jnp.exp2(y - z)).astype(BF16) for x, y, z in zip(k_t, a_r, m_new)]
    ct = [jnp.exp2(x + c["m"] - z) * c["ct"] + jnp.dot(y, c["v1"], preferred_element_type=F32)
          for c, x, y, z in zip(chains, b_last, kw_t, m_new)]
    w = [(jnp.exp2(x - y) * z).astype(BF16) for x, y, z in zip(um, g_t, qk)]
    tot = [jnp.dot(x, c["v1"], preferred_element_type=F32) + jnp.exp2(c["m"] - y) * z
           for c, x, y, z in zip(chains, w, g_t, inter)]
    h = [x[:, :E] / jnp.maximum(jnp.abs(x[:, E:]), jnp.exp2(-(y + z))) for x, y, z in zip(tot, b_c, g_t)]
    return list(zip(h, ct, m_new))


def _mlstm_kernel(q_ref, k_ref, v_ref, gr_ref, gb_ref, hg_ref, o_ref, acc_ref, *, seq_len, chunk, heads):
    S, L, E = seq_len, chunk, M_HEAD_DIM
    n_chunks = S // L
    row = lax.broadcasted_iota(jnp.int32, (L, L), 0)
    col = lax.broadcasted_iota(jnp.int32, (L, L), 1)
    seen = (col <= row, col >= row)
    diag = col == row
    ones = jnp.ones((L, E), BF16)

    def finish(tot, lanes):
        return tot * lax.rsqrt(jnp.mean(tot * tot, axis=-1, keepdims=True) + EPS) * hg_ref[:, lanes]

    def sweep(second):
        def body(it, states):
            chains, where = [], []
            for n, (ct, m) in enumerate(states):
                hh, rev = divmod(n, 2)
                c = it if rev == 0 else n_chunks - 1 - it
                rows = pl.ds(pl.multiple_of(c * L, L), L)
                lanes = slice(hh * E, (hh + 1) * E)
                g = gr_ref[4 * hh:4 * hh + 4, rows] + gb_ref[4 * hh:4 * hh + 4, :]
                where.append((rows, lanes, acc_ref[rows, lanes] if second else None))
                chains.append(dict(q=q_ref[rows, lanes], k=k_ref[rows, lanes],
                                   v1=jnp.concatenate([v_ref[rows, lanes], ones], axis=1),
                                   i_r=g[rev:rev + 1] * LOG2E, lf_r=_log_sigmoid(g[2 + rev:3 + rev]) * LOG2E,
                                   ct=ct, m=m, rev=rev))
            out = _mlstm_chunks(chains, seen, diag)
            for (rows, lanes, prev), (h, _, _) in zip(where, out):
                if second:
                    o_ref[rows, lanes] = finish(prev + h, lanes).astype(o_ref.dtype)
                else:
                    acc_ref[rows, lanes] = h
            return tuple((ct, m) for _, ct, m in out)
        return body

    zero = (jnp.zeros((E, 2 * E), F32), jnp.zeros((1, 1), F32))
    states = lax.fori_loop(0, n_chunks // 2, sweep(False), (zero,) * (2 * heads))
    lax.fori_loop(n_chunks // 2, n_chunks, sweep(True), states)


def _mlstm(q, k, v, g_rows, gate_b, hnorm_g, Bn, S):
    E = M_HEAD_DIM
    T = Bn * S
    hs = MLSTM_HEADS_PER_STEP
    groups = M_HEADS // hs
    seq = pl.BlockSpec((None, S, hs * E), lambda b, h: (b, 0, h))
    q3, k3, v3 = (t.reshape(Bn, S, M_WIDTH) for t in (q, k, v))
    out = pl.pallas_call(
        functools.partial(_mlstm_kernel, seq_len=S, chunk=MLSTM_CHUNK, heads=hs),
        grid=(Bn, groups),
        in_specs=[seq, seq, seq,
                  pl.BlockSpec((None, 4 * hs, S), lambda b, h: (h, 0, b)),
                  pl.BlockSpec((None, 4 * hs, 1), lambda b, h: (h, 0, 0)),
                  pl.BlockSpec((1, hs * E), lambda b, h: (0, h))],
        out_specs=seq,
        out_shape=jax.ShapeDtypeStruct((Bn, S, M_WIDTH), BF16),
        scratch_shapes=[pltpu.VMEM((S, hs * E), F32)],
        compiler_params=_cparams("parallel", "parallel"), name="mlstm")(
            q3, k3, v3, g_rows.reshape(groups, 4 * hs, T), gate_b.reshape(groups, 4 * hs, 1),
            hnorm_g.reshape(1, M_WIDTH))
    return out.reshape(T, M_WIDTH)


def _mixer_ab(x, g, w_in, conv_w, conv_b, gate_b, hnorm_g, w_out, Bn, S):
    T = Bn * S
    MW, BW = M_WIDTH, B_WIDTH
    g0 = 4 * MW
    n_gate = 4 * M_HEADS
    w_qk = w_in[:, :2 * MW].astype(BF16)
    w_rest = jnp.concatenate([w_in[:, 2 * MW:g0], w_in[:, g0 + n_gate:]], axis=1).astype(BF16)
    wg_t = w_in[:, g0:g0 + n_gate].reshape(-1, 4, M_HEADS).transpose(2, 1, 0).reshape(n_gate, -1).astype(BF16)
    gate_b = gate_b.reshape(4, M_HEADS).T.astype(F32)
    (q_m, k_m, v_m, o_m, g_rows, q1, k1, v1, q4, k4, v4, q16, k16, v16) = _proj_ab(
        x, g, w_qk, w_rest, wg_t, conv_w, conv_b, S)

    hn = _mlstm(q_m, k_m, v_m, g_rows, gate_b, hnorm_g, Bn, S)

    slopes = _alibi_slopes(B_HEADS)
    outs, lses = [], []
    for d, qkv in zip(DILATIONS, ((q1, k1, v1), (q4, k4, v4), (q16, k16, v16))):
        qv, kv, vv = (t.reshape(Bn, S // d, d * BW) for t in qkv)
        o, lse = _banded_attention(qv, kv, vv, slopes, None, dilation=d, groups=1, radius=B_RADIUS,
                                   with_lse=True, out_dtype=BF16, q_block=B_Q_BLOCK, chains=B_CHAINS)
        outs.append(o.reshape(T // d, d * BW))
        lses.append(lse.reshape(T // d, d * BW))
    w_out = w_out.astype(BF16)
    return ("ab", hn, o_m, outs, lses, w_out[:MW], w_out[MW:])


def _c_head_order():
    order = []
    for pair in range(C_KV_HEADS // 2):
        for slot in range(C_GROUP):
            for j in range(2):
                order.append((2 * pair + j) * C_GROUP + slot)
    return jnp.array(order, dtype=jnp.int32)


def _c_in_proj(g, w_in):
    D = w_in.shape[0]
    QW = C_HEADS * HEAD_DIM
    KW = C_KV_HEADS * HEAD_DIM
    wq = w_in[:, :QW].reshape(D, C_HEADS, HEAD_DIM)[:, _c_head_order()].reshape(D, QW)
    w_main = jnp.concatenate([wq, w_in[:, QW:]], axis=1).astype(BF16)
    return g, w_main, [(QW, BF16, ATTN_Q_SCALE), (KW, BF16, 1.0), (KW, BF16, 1.0)]


def _mixer_c(q, k, v, sink, w_out, Bn, S):
    T, QW = q.shape
    KW = k.shape[1]
    D = w_out.shape[1]
    w_out = w_out.reshape(C_HEADS, HEAD_DIM, D)[_c_head_order()].reshape(QW, D).astype(BF16)
    (o,) = _banded_attention(q.reshape(Bn, S, QW), k.reshape(Bn, S, KW), v.reshape(Bn, S, KW),
                             _alibi_slopes(C_HEADS), sink.astype(F32), dilation=1, groups=C_GROUP,
                             radius=C_RADIUS, with_lse=False, out_dtype=BF16, q_block=C_Q_BLOCK, chains=C_CHAINS)
    return ("c", o.reshape(T, QW), w_out)


def kernel(x, norm_g, ffn_w1, ffn_w3, ffn_w2, ab_w_in, ab_conv_w, ab_conv_b, ab_gate_b, ab_hnorm_g, ab_w_out,
           c_w_in, c_sink, c_w_out, final_g):
    Bn, S, D = x.shape
    depth = norm_g.shape[0]
    stacks = (ffn_w1, ffn_w3, ffn_w2)
    weights = tuple(w[0, 0].astype(BF16) for w in stacks)
    x = x.reshape(Bn * S, D)
    for l in range(depth):
        j = l // 2
        proj = _c_in_proj(norm_g[l, 1], c_w_in[j]) if l % 2 else None
        x, qkv, weights = _ffn(x, norm_g[l, 0], weights, proj=proj, cast_next=stacks + ((l, 1),))
        if l % 2 == 0:
            mixed = _mixer_ab(x, norm_g[l, 1], ab_w_in[j], ab_conv_w[j], ab_conv_b[j], ab_gate_b[j], ab_hnorm_g[j],
                              ab_w_out[j], Bn, S)
        else:
            mixed = _mixer_c(*qkv, c_sink[j], c_w_out[j], Bn, S)
        last = l == depth - 1
        x, _, weights = _ffn(x, norm_g[l, 2], weights, final_g if last else None, mixed,
                             cast_next=None if last else stacks + ((l + 1, 0),))
    return x.reshape(Bn, S, D)
```
